```python
import math
import jax
import jax.numpy as jnp
from jax import lax
import numpy as np

D_MODEL = 2048
BATCH = 2
SEQ = 4096
DEPTH = 4
DEC_BATCH = 8
DEC_SEQ = 8
PAST_LEN = 16384
PAGE_SIZE = 128

HEAD_DIM = 128
D_M = D_MODEL // 4
D_F = D_MODEL // 2
D_G = D_MODEL - D_M - D_F
H_M = D_M // HEAD_DIM
H_F = D_F // HEAD_DIM
H_G = D_G // HEAD_DIM
D_MIX = D_M + D_F + D_G
D_FF = 5632
CONV_W = 4
CHUNK = 64
Q_BLOCK = 128
N_SUB = 3
DEEPNORM_ALPHA = (2 * DEPTH) ** 0.25
DEEPNORM_BETA = (8 * DEPTH) ** -0.25
MACARON_W = 0.5
LN_EPS = 1e-5
NORM_EPS = 1e-6
SPLIT_SIZES = (D_M, D_M, D_M, D_M, H_M, H_M, D_F, D_F, D_F, H_F, D_G, D_G, D_G, D_G, H_G, H_G)
N_IN = sum(SPLIT_SIZES)
SPLIT_POINTS = tuple(sum(SPLIT_SIZES[:i + 1]) for i in range(len(SPLIT_SIZES) - 1))

kernel_name = 'hymba_mlstm_fox_gdn_macaron_step'


def layer_norm(x, g, b):
    xf = x.astype(jnp.float32)
    mu = jnp.mean(xf, -1, keepdims=True)
    var = jnp.mean(jnp.square(xf - mu), -1, keepdims=True)
    return ((xf - mu) * lax.rsqrt(var + LN_EPS)).astype(x.dtype) * g + b


def head_layer_norm(x, g):
    mu = jnp.mean(x, -1, keepdims=True)
    var = jnp.mean(jnp.square(x - mu), -1, keepdims=True)
    return (x - mu) * lax.rsqrt(var + LN_EPS) * g


def rms_norm(x, g):
    return x * lax.rsqrt(jnp.mean(jnp.square(x), -1, keepdims=True) + NORM_EPS) * g


def l2_normalize(x):
    return x * lax.rsqrt(jnp.sum(jnp.square(x), -1, keepdims=True) + NORM_EPS)


def swiglu(h, w_gate, w_up, w_down):
    return (jax.nn.silu(h @ w_gate) * (h @ w_up)) @ w_down


def causal_conv(x, buf, w):
    T = x.shape[1]
    xp = jnp.concatenate([buf.astype(x.dtype), x], axis=1)
    out = xp[:, 0:T] * w[0]
    for j in range(1, CONV_W):
        out = out + xp[:, j:j + T] * w[j]
    return out, xp[:, xp.shape[1] - (CONV_W - 1):]


def to_chunks(a, L):
    B, T, H = a.shape[:3]
    a = a.reshape((B, T // L, L, H) + a.shape[3:])
    return jnp.moveaxis(jnp.moveaxis(a, 1, 0), 3, 2)


def from_chunks(o):
    nc, B, H, L, D = o.shape
    return jnp.transpose(o, (1, 0, 3, 2, 4)).reshape(B, nc * L, H, D)


def mlstm_chunked(q, k, v, log_i, log_f, C0, n0, m0):
    T = q.shape[1]
    L = math.gcd(T, CHUNK)
    causal = jnp.tril(jnp.ones((L, L), bool))

    def step(carry, inp):
        C, n, m = carry
        qc, kc, vc, li, lf = inp
        b = jnp.cumsum(lf, axis=-1)
        dmat = jnp.where(causal, b[..., :, None] - b[..., None, :] + li[..., None, :], -jnp.inf)
        inter = b + m[..., None]
        m_row = jnp.maximum(inter, jnp.max(dmat, -1))
        w_intra = jnp.exp(dmat - m_row[..., None])
        w_inter = jnp.exp(inter - m_row)
        s = jnp.einsum('bhtd,bhsd->bhts', qc, kc) * w_intra
        num = jnp.einsum('bhts,bhse->bhte', s, vc) + w_inter[..., None] * jnp.einsum('bhtd,bhde->bhte', qc, C)
        den = jnp.sum(s, -1) + w_inter * jnp.einsum('bhtd,bhd->bht', qc, n)
        h = num / jnp.maximum(jnp.abs(den), jnp.exp(-m_row))[..., None]
        b_last = b[..., -1]
        d_last = b_last[..., None] - b + li
        m_new = jnp.maximum(b_last + m, jnp.max(d_last, -1))
        w_k = jnp.exp(d_last - m_new[..., None])
        decay = jnp.exp(b_last + m - m_new)
        C_new = decay[..., None, None] * C + jnp.einsum('bhs,bhsd,bhse->bhde', w_k, kc, vc)
        n_new = decay[..., None] * n + jnp.einsum('bhs,bhsd->bhd', w_k, kc)
        return (C_new, n_new, m_new), h

    f32 = jnp.float32
    carry0 = (C0.astype(f32), n0.astype(f32), m0.astype(f32))
    xs = (to_chunks(q, L), to_chunks(k, L), to_chunks(v, L), to_chunks(log_i, L), to_chunks(log_f, L))
    (C1, n1, m1), h = lax.scan(step, carry0, xs)
    return from_chunks(h), (C1, n1, m1)


def gdn_chunked(q, k, v, g, beta, S0):
    T = q.shape[1]
    L = math.gcd(T, CHUNK)
    incl = jnp.tril(jnp.ones((L, L), bool))
    strict = jnp.tril(jnp.ones((L, L), bool), -1)
    eye = jnp.eye(L, dtype=jnp.float32)
    dk = q.shape[-1]

    def step(S, inp):
        qc, kc, vc, gc, bc = inp
        G = jnp.cumsum(gc, axis=-1)
        decay = jnp.exp(jnp.where(incl, G[..., :, None] - G[..., None, :], -jnp.inf))
        A = jnp.where(strict, bc[..., :, None] * jnp.einsum('bhtd,bhsd->bhts', kc, kc) * decay, 0.0)
        rhs = jnp.concatenate([kc * (bc * jnp.exp(G))[..., None], vc * bc[..., None]], -1)
        sol = lax.linalg.triangular_solve(eye + A, rhs, left_side=True, lower=True)
        w, u = sol[..., :dk], sol[..., dk:]
        u = u - jnp.einsum('bhtd,bhde->bhte', w, S)
        o = (jnp.einsum('bhtd,bhde->bhte', qc * jnp.exp(G)[..., None], S)
             + jnp.einsum('bhts,bhse->bhte', jnp.einsum('bhtd,bhsd->bhts', qc, kc) * decay, u))
        G_last = G[..., -1]
        S_new = (jnp.exp(G_last)[..., None, None] * S
                 + jnp.einsum('bhsd,bhse->bhde', kc * jnp.exp(G_last[..., None] - G)[..., None], u))
        return S_new, o

    xs = (to_chunks(q, L), to_chunks(k, L), to_chunks(v, L), to_chunks(g, L), to_chunks(beta, L))
    S1, o = lax.scan(step, S0.astype(jnp.float32), xs)
    return from_chunks(o), S1


def fox_prompt(q, k, v, logf):
    B, S, H, D = q.shape
    blk = math.gcd(S, Q_BLOCK)
    nb = S // blk
    c = jnp.transpose(jnp.cumsum(logf, axis=1), (0, 2, 1))
    qb = jnp.moveaxis(q.reshape(B, nb, blk, H, D), 1, 0)
    cb = jnp.moveaxis(c.reshape(B, H, nb, blk), 2, 0)
    kpos = jnp.arange(S)
    scale = HEAD_DIM ** -0.5

    def one_block(args):
        qi, ci, i = args
        qpos = i * blk + jnp.arange(blk)
        s = jnp.einsum('bqhd,bkhd->bhqk', qi, k) * scale + (ci[..., :, None] - c[:, :, None, :])
        s = jnp.where(kpos[None, :] <= qpos[:, None], s, -jnp.inf)
        p = jax.nn.softmax(s.astype(jnp.float32), axis=-1)
        return jnp.einsum('bhqk,bkhd->bqhd', p, v)

    out = lax.map(one_block, (qb, cb, jnp.arange(nb)))
    return jnp.moveaxis(out, 0, 1).reshape(B, S, H, D)


def fox_sample(q, k, v, logf, k_past, v_past, logf_past):
    T = q.shape[1]
    P = k_past.shape[1]
    lp = logf_past.astype(jnp.float32)
    suffix = jnp.transpose(lax.cumsum(lp, axis=1, reverse=True) - lp, (0, 2, 1))
    cum = jnp.transpose(jnp.cumsum(logf, axis=1), (0, 2, 1))
    scale = HEAD_DIM ** -0.5
    s_past = jnp.einsum('bqhd,bkhd->bhqk', q, k_past) * scale + cum[..., :, None] + suffix[..., None, :]
    s_new = jnp.einsum('bqhd,bkhd->bhqk', q, k) * scale + cum[..., :, None] - cum[..., None, :]
    s_new = jnp.where(jnp.tril(jnp.ones((T, T), bool)), s_new, -jnp.inf)
    p = jax.nn.softmax(jnp.concatenate([s_past, s_new], -1).astype(jnp.float32), axis=-1)
    return (jnp.einsum('bhqk,bkhd->bqhd', p[..., :P], v_past)
            + jnp.einsum('bhqk,bkhd->bqhd', p[..., P:], v))


def token_mixer(h, w_in, b_in, mlstm_norm_g, conv_w, A_log, dt_bias, gdn_norm_g, w_out,
                mlstm_state, gdn_state, fox_past):
    B, T, _ = h.shape
    proj = (h @ w_in + b_in).astype(jnp.float32)
    (mq, mk, mv, mo, mi, mf, fq, fk, fv, ff, gq, gk, gv, gz, ga, gb) = jnp.split(proj, SPLIT_POINTS, axis=-1)

    def heads(a):
        return a.reshape(B, T, -1, HEAD_DIM)

    hm, (C1, n1, m1) = mlstm_chunked(heads(mq), heads(mk) * HEAD_DIM ** -0.5, heads(mv), mi,
                                     jax.nn.log_sigmoid(mf), *mlstm_state)
    hm = head_layer_norm(hm, mlstm_norm_g) * jax.nn.sigmoid(heads(mo))
    f_logf = jax.nn.log_sigmoid(ff)
    fq, fk, fv = heads(fq), heads(fk), heads(fv)
    if fox_past is None:
        hf = fox_prompt(fq, fk, fv, f_logf)
    else:
        hf = fox_sample(fq, fk, fv, f_logf, *fox_past)
    S0, buf0 = gdn_state
    qkv, buf1 = causal_conv(jnp.concatenate([gq, gk, gv], -1), buf0, conv_w)
    cq, ck, cv = jnp.split(jax.nn.silu(qkv), 3, axis=-1)
    g = -jnp.exp(A_log) * jax.nn.softplus(ga + dt_bias)
    hg, S1 = gdn_chunked(l2_normalize(heads(cq)) * HEAD_DIM ** -0.5, l2_normalize(heads(ck)), heads(cv),
                         g, jax.nn.sigmoid(gb), S0)
    hg = rms_norm(hg, gdn_norm_g) * jax.nn.silu(heads(gz))
    mixed = jnp.concatenate([hm.reshape(B, T, D_M), hf.reshape(B, T, D_F), hg.reshape(B, T, D_G)], -1)
    y = mixed.astype(h.dtype) @ w_out
    dt = h.dtype
    return y, (fk.astype(dt), fv.astype(dt), f_logf.astype(dt), C1, n1, m1, S1, buf1)


def decoder_layer(x, c, w_ada, b_ada, ln_g, ln_b, w_gate, w_up, w_down, w_in, b_in, mlstm_norm_g,
                  conv_w, A_log, dt_bias, gdn_norm_g, w_out, mlstm_state, gdn_state, fox_past):
    B, _, D = x.shape
    mod = (jax.nn.silu(c) @ w_ada + b_ada).reshape(B, 1, N_SUB, 3, D)
    shift, scale, gate = mod[:, :, :, 0], mod[:, :, :, 1], 1.0 + mod[:, :, :, 2]

    def modulate(z, j):
        return z * (1.0 + scale[:, :, j]) + shift[:, :, j]

    f1 = swiglu(modulate(x, 0), w_gate[0], w_up[0], w_down[0])
    x = layer_norm(DEEPNORM_ALPHA * x + MACARON_W * gate[:, :, 0] * f1, ln_g[0], ln_b[0])
    y, new_state = token_mixer(modulate(x, 1), w_in, b_in, mlstm_norm_g, conv_w, A_log, dt_bias,
                               gdn_norm_g, w_out, mlstm_state, gdn_state, fox_past)
    x = layer_norm(DEEPNORM_ALPHA * x + gate[:, :, 1] * y, ln_g[1], ln_b[1])
    f2 = swiglu(modulate(x, 2), w_gate[1], w_up[1], w_down[1])
    x = layer_norm(DEEPNORM_ALPHA * x + MACARON_W * gate[:, :, 2] * f2, ln_g[2], ln_b[2])
    return x, new_state


def setup_inputs(seed: int = 0) -> dict:
    key = jax.random.key(seed)
    ks = jax.random.split(key, 32)
    f32 = jnp.float32

    def nrm(k, shape, s):
        return jax.random.normal(k, shape, f32) * s

    n_pages = PAST_LEN // PAGE_SIZE
    n_phys = (DEC_BATCH * n_pages * 5) // 4
    fox_f_bias = jnp.linspace(1.0, 6.0, H_F, dtype=f32)
    x_prompt = nrm(ks[0], (BATCH, SEQ, D_MODEL), 1.0)
    x_sample = nrm(ks[1], (DEC_BATCH, DEC_SEQ, D_MODEL), 1.0)
    cache_fox_k = nrm(ks[2], (DEPTH, n_phys, PAGE_SIZE, H_F, HEAD_DIM), 1.0)
    cache_fox_v = nrm(ks[3], (DEPTH, n_phys, PAGE_SIZE, H_F, HEAD_DIM), DEEPNORM_BETA)
    cache_fox_logf = jax.nn.log_sigmoid(nrm(ks[4], (DEPTH, n_phys, PAGE_SIZE, H_F), 1.0) + fox_f_bias)
    state_mlstm_C = nrm(ks[5], (DEPTH, DEC_BATCH, H_M, HEAD_DIM, HEAD_DIM), 0.1)
    state_mlstm_n = nrm(ks[6], (DEPTH, DEC_BATCH, H_M, HEAD_DIM), 0.1)
    state_mlstm_m = nrm(ks[7], (DEPTH, DEC_BATCH, H_M), 1.0)
    state_gdn_S = nrm(ks[8], (DEPTH, DEC_BATCH, H_G, HEAD_DIM, HEAD_DIM), 0.1)
    state_gdn_conv = nrm(ks[9], (DEPTH, DEC_BATCH, CONV_W - 1, 3 * D_G), 1.0)
    perm = jax.random.permutation(ks[10], n_phys)
    page_table = perm[:DEC_BATCH * n_pages].reshape(DEC_BATCH, n_pages).astype(jnp.int32)
    c_prompt = nrm(ks[11], (BATCH, D_MODEL), 1.0)
    c_sample = nrm(ks[12], (DEC_BATCH, D_MODEL), 1.0)
    w_ada = nrm(ks[13], (DEPTH, D_MODEL, 3 * N_SUB * D_MODEL), 0.3 * D_MODEL ** -0.5)
    b_ada = nrm(ks[14], (DEPTH, 3 * N_SUB * D_MODEL), 0.01)
    ln_g = 1.0 + nrm(ks[15], (DEPTH, N_SUB, D_MODEL), 0.02)
    ln_b = nrm(ks[16], (DEPTH, N_SUB, D_MODEL), 0.02)
    ffn_w_gate = nrm(ks[17], (DEPTH, 2, D_MODEL, D_FF), D_MODEL ** -0.5)
    ffn_w_up = nrm(ks[18], (DEPTH, 2, D_MODEL, D_FF), D_MODEL ** -0.5)
    ffn_w_down = nrm(ks[19], (DEPTH, 2, D_FF, D_MODEL), DEEPNORM_BETA * D_FF ** -0.5)
    offs = [sum(SPLIT_SIZES[:i]) for i in range(len(SPLIT_SIZES) + 1)]
    col_scale = np.ones((N_IN,), np.float32)
    for i in (2, 8, 12):
        col_scale[offs[i]:offs[i + 1]] = DEEPNORM_BETA
    w_in = nrm(ks[20], (DEPTH, D_MODEL, N_IN), D_MODEL ** -0.5) * jnp.asarray(col_scale)
    b_in = nrm(ks[21], (DEPTH, N_IN), 0.02)
    b_in = b_in.at[:, offs[5]:offs[6]].add(3.0)
    b_in = b_in.at[:, offs[9]:offs[10]].add(fox_f_bias)
    mlstm_norm_g = 1.0 + nrm(ks[22], (DEPTH, H_M, HEAD_DIM), 0.02)
    gdn_conv_w = nrm(ks[23], (DEPTH, CONV_W, 3 * D_G), CONV_W ** -0.5)
    gdn_A_log = jnp.log(jax.random.uniform(ks[24], (DEPTH, H_G), f32, 1.0, 16.0))
    dt = jnp.exp(jax.random.uniform(ks[25], (DEPTH, H_G), f32, math.log(1e-3), math.log(1e-1)))
    gdn_dt_bias = dt + jnp.log(-jnp.expm1(-dt))
    gdn_norm_g = 1.0 + nrm(ks[26], (DEPTH, HEAD_DIM), 0.02)
    w_out = nrm(ks[27], (DEPTH, D_MIX, D_MODEL), DEEPNORM_BETA * D_MIX ** -0.5)
    return {'x_prompt': x_prompt, 'x_sample': x_sample,
            'cache_fox_k': cache_fox_k, 'cache_fox_v': cache_fox_v, 'cache_fox_logf': cache_fox_logf,
            'state_mlstm_C': state_mlstm_C, 'state_mlstm_n': state_mlstm_n, 'state_mlstm_m': state_mlstm_m,
            'state_gdn_S': state_gdn_S, 'state_gdn_conv': state_gdn_conv, 'page_table': page_table,
            'c_prompt': c_prompt, 'c_sample': c_sample, 'w_ada': w_ada, 'b_ada': b_ada,
            'ln_g': ln_g, 'ln_b': ln_b, 'ffn_w_gate': ffn_w_gate, 'ffn_w_up': ffn_w_up, 'ffn_w_down': ffn_w_down,
            'w_in': w_in, 'b_in': b_in, 'mlstm_norm_g': mlstm_norm_g, 'gdn_conv_w': gdn_conv_w,
            'gdn_A_log': gdn_A_log, 'gdn_dt_bias': gdn_dt_bias, 'gdn_norm_g': gdn_norm_g, 'w_out': w_out}


def reference(x_prompt, x_sample, cache_fox_k, cache_fox_v, cache_fox_logf, state_mlstm_C, state_mlstm_n,
              state_mlstm_m, state_gdn_S, state_gdn_conv, page_table, c_prompt, c_sample, w_ada, b_ada,
              ln_g, ln_b, ffn_w_gate, ffn_w_up, ffn_w_down, w_in, b_in, mlstm_norm_g, gdn_conv_w,
              gdn_A_log, gdn_dt_bias, gdn_norm_g, w_out):
    B = x_prompt.shape[0]
    DB, n_pages = page_table.shape
    f32 = jnp.float32

    def gather_pages(pool):
        rows = pool[page_table]
        return rows.reshape((DB, n_pages * PAGE_SIZE) + pool.shape[2:])

    zero_mlstm = (jnp.zeros((B, H_M, HEAD_DIM, HEAD_DIM), f32), jnp.zeros((B, H_M, HEAD_DIM), f32),
                  jnp.zeros((B, H_M), f32))
    zero_gdn = (jnp.zeros((B, H_G, HEAD_DIM, HEAD_DIM), f32), jnp.zeros((B, CONV_W - 1, 3 * D_G), f32))
    xp, xs = x_prompt, x_sample
    st_p, st_s = [], []
    for l in range(DEPTH):
        lw = (w_ada[l], b_ada[l], ln_g[l], ln_b[l], ffn_w_gate[l], ffn_w_up[l], ffn_w_down[l], w_in[l],
              b_in[l], mlstm_norm_g[l], gdn_conv_w[l], gdn_A_log[l], gdn_dt_bias[l], gdn_norm_g[l], w_out[l])
        xp, sp = decoder_layer(xp, c_prompt, *lw, zero_mlstm, zero_gdn, None)
        fox_past = (gather_pages(cache_fox_k[l]), gather_pages(cache_fox_v[l]), gather_pages(cache_fox_logf[l]))
        xs, ss = decoder_layer(xs, c_sample, *lw, (state_mlstm_C[l], state_mlstm_n[l], state_mlstm_m[l]),
                               (state_gdn_S[l], state_gdn_conv[l]), fox_past)
        st_p.append(sp)
        st_s.append(ss)
    Pn = [jnp.stack(a) for a in zip(*st_p)]
    Sn = [jnp.stack(a) for a in zip(*st_s)]
    return (xp, xs, Pn[0], Pn[1], Pn[2], Pn[3], Pn[4], Pn[5], Pn[6], Pn[7],
            Sn[0], Sn[1], Sn[2], Sn[3], Sn[4], Sn[5], Sn[6], Sn[7])
```

```python
import functools
import math

import jax
import jax.numpy as jnp
from jax import lax
from jax.experimental import pallas as pl
from jax.experimental.pallas import tpu as pltpu

F32 = jnp.float32
BF16 = jnp.bfloat16

HEAD_DIM = 128
LANES = 128
SUBLANES = 8
CHUNK = 64
N_SUB = 3
LN_EPS = 1e-5
NORM_EPS = 1e-6
MACARON_W = 0.5
GATE_ROWS = 32
VMEM_LIMIT_BYTES = 56 * 1024 * 1024
HI = lax.Precision.HIGHEST
NEG_INF = float("-inf")


def _params(*sem):
    return pltpu.CompilerParams(dimension_semantics=sem, vmem_limit_bytes=VMEM_LIMIT_BYTES)


def _dot(a, b, precision=None):
    return jnp.dot(a, b, preferred_element_type=F32, precision=precision)


def _dot_nt(a, b, precision=None):
    return lax.dot_general(a, b, (((1,), (1,)), ((), ())), preferred_element_type=F32, precision=precision)


def _dot_tn(a, b, precision=None):
    return lax.dot_general(a, b, (((0,), (0,)), ((), ())), preferred_element_type=F32, precision=precision)


def _sigmoid(x):
    return 1.0 / (1.0 + jnp.exp(-x))


def _silu(x):
    return x * _sigmoid(x)


def _softplus(x):
    return jnp.maximum(x, 0.0) + jnp.log1p(jnp.exp(-jnp.abs(x)))


def _log_sigmoid(x):
    return -_softplus(-x)


def _pick(m, candidates):
    for c in candidates:
        if m % c == 0:
            return c
    return m


def _ada_kernel(c_ref, w_ref, b_ref, o_ref):
    c = c_ref[...]
    sc = _silu(c).astype(BF16)
    o_ref[...] = _dot(sc, w_ref[...].astype(BF16)) + b_ref[...]


def _ada(c_all, w_ada, b_ada):
    depth, d, n = w_ada.shape
    rows = c_all.shape[0]
    bn = _pick(n, (1024, 512, 256, 128))
    return pl.pallas_call(
        _ada_kernel,
        grid=(depth, n // bn),
        in_specs=[pl.BlockSpec((rows, d), lambda l, j: (0, 0)),
                  pl.BlockSpec((None, d, bn), lambda l, j: (l, 0, j)),
                  pl.BlockSpec((None, 1, bn), lambda l, j: (l, 0, j))],
        out_specs=pl.BlockSpec((None, rows, bn), lambda l, j: (l, 0, j)),
        out_shape=jax.ShapeDtypeStruct((depth, rows, n), F32),
        compiler_params=_params("arbitrary", "arbitrary"),
    )(c_all, w_ada, b_ada.reshape(depth, 1, n))


def _modulate_kernel(x_ref, sc_ref, sh_ref, o_ref):
    o_ref[...] = (x_ref[...] * (1.0 + sc_ref[...]) + sh_ref[...]).astype(BF16)


def _mod_spec(mod, layer, slot, rows_per_group, bm):
    r, d = mod.shape[3], mod.shape[4]
    if r == 1:
        return pl.BlockSpec((None, None, None, 1, d),
                            lambda m, *_: (layer, slot, (m * bm) // rows_per_group, 0, 0))
    return pl.BlockSpec((None, None, None, bm, d), lambda m, *_: (layer, slot, 0, m, 0))


def _modulate(x, mod, layer, j, rows_per_group):
    m, d = x.shape
    bm = _pick(rows_per_group, (512, 256, 128, 64))
    return pl.pallas_call(
        _modulate_kernel,
        grid=(m // bm,),
        in_specs=[pl.BlockSpec((bm, d), lambda i: (i, 0)),
                  _mod_spec(mod, layer, 3 * j + 1, rows_per_group, bm),
                  _mod_spec(mod, layer, 3 * j + 0, rows_per_group, bm)],
        out_specs=pl.BlockSpec((bm, d), lambda i: (i, 0)),
        out_shape=jax.ShapeDtypeStruct((m, d), BF16),
        compiler_params=_params("arbitrary"),
    )(x, mod, mod)


def _gateup_kernel(h_ref, wg_ref, wu_ref, o_ref, wg_s, wu_s):
    @pl.when(pl.program_id(1) == 0)
    def _():
        wg_s[...] = wg_ref[...].astype(BF16)
        wu_s[...] = wu_ref[...].astype(BF16)

    h = h_ref[...]
    g = _dot(h, wg_s[...])
    u = _dot(h, wu_s[...])
    o_ref[...] = (_silu(g) * u).astype(BF16)


def _gateup(h, w_gate, w_up, layer, j):
    m, d = h.shape
    f = w_gate.shape[-1]
    bm = _pick(m, (1024, 512, 256, 128, 64))
    bn = _pick(f, (512, 256, 128))
    wspec = pl.BlockSpec((None, None, d, bn), lambda n, i: (layer, j, 0, n))
    return pl.pallas_call(
        _gateup_kernel,
        grid=(f // bn, m // bm),
        in_specs=[pl.BlockSpec((bm, d), lambda n, i: (i, 0)), wspec, wspec],
        out_specs=pl.BlockSpec((bm, bn), lambda n, i: (i, n)),
        out_shape=jax.ShapeDtypeStruct((m, f), BF16),
        scratch_shapes=[pltpu.VMEM((d, bn), BF16), pltpu.VMEM((d, bn), BF16)],
        compiler_params=_params("arbitrary", "arbitrary"),
    )(h, w_gate, w_up)


def _proj_ln_kernel(a_ref, w_ref, x_ref, gate_ref, lng_ref, lnb_ref, *rest, nk, bk, alpha, coef, emit_h):
    if emit_h:
        sc_ref, sh_ref, xo_ref, ho_ref, w_s = rest
    else:
        xo_ref, w_s = rest
    i = pl.program_id(0)

    @pl.when(i < nk)
    def _():
        w_s[pl.ds(pl.multiple_of(i * bk, bk), bk), :] = w_ref[...].astype(BF16)

    @pl.when(i >= nk)
    def _():
        acc = _dot(a_ref[...], w_s[...])
        y = alpha * x_ref[...] + (coef * (1.0 + gate_ref[...])) * acc
        mu = jnp.mean(y, axis=-1, keepdims=True)
        yc = y - mu
        var = jnp.mean(yc * yc, axis=-1, keepdims=True)
        xn = yc * lax.rsqrt(var + LN_EPS) * lng_ref[...] + lnb_ref[...]
        xo_ref[...] = xn
        if emit_h:
            ho_ref[...] = (xn * (1.0 + sc_ref[...]) + sh_ref[...]).astype(BF16)


def _proj_ln(a, w, w_prefix, x, mod, layer, j, nxt, ln_g, ln_b, rows_per_group, alpha, coef):
    m, k = a.shape
    d = x.shape[1]
    bm = _pick(rows_per_group, (256, 128, 64))
    bk = _pick(k, (512, 256, 128))
    nk = k // bk
    emit_h = nxt is not None
    npre = len(w_prefix)

    def row(i):
        return jnp.maximum(i - nk, 0)

    def shifted(spec_fn):
        return spec_fn

    in_specs = [
        pl.BlockSpec((bm, k), lambda i: (row(i), 0)),
        pl.BlockSpec((None,) * npre + (bk, d), lambda i: tuple(w_prefix) + (jnp.minimum(i, nk - 1), 0)),
        pl.BlockSpec((bm, d), lambda i: (row(i), 0)),
    ]

    def modspec(lyr, slot):
        r = mod.shape[3]
        if r == 1:
            return pl.BlockSpec((None, None, None, 1, d),
                                lambda i: (lyr, slot, (row(i) * bm) // rows_per_group, 0, 0))
        return pl.BlockSpec((None, None, None, bm, d), lambda i: (lyr, slot, 0, row(i), 0))

    in_specs.append(modspec(layer, 3 * j + 2))
    in_specs += [pl.BlockSpec((None, None, 1, d), lambda i: (layer, j, 0, 0)),
                 pl.BlockSpec((None, None, 1, d), lambda i: (layer, j, 0, 0))]
    args = [a, w, x, mod, ln_g, ln_b]
    out_specs = [pl.BlockSpec((bm, d), lambda i: (row(i), 0))]
    out_shape = [jax.ShapeDtypeStruct((m, d), F32)]
    if emit_h:
        in_specs += [modspec(nxt[0], 3 * nxt[1] + 1), modspec(nxt[0], 3 * nxt[1] + 0)]
        args += [mod, mod]
        out_specs.append(pl.BlockSpec((bm, d), lambda i: (row(i), 0)))
        out_shape.append(jax.ShapeDtypeStruct((m, d), BF16))
    out = pl.pallas_call(
        functools.partial(_proj_ln_kernel, nk=nk, bk=bk, alpha=alpha, coef=coef, emit_h=emit_h),
        grid=(nk + m // bm,),
        in_specs=in_specs,
        out_specs=out_specs,
        out_shape=out_shape,
        scratch_shapes=[pltpu.VMEM((k, d), BF16)],
        compiler_params=_params("arbitrary"),
    )(*args)
    return (out[0], out[1]) if emit_h else (out[0], None)


def _win_kernel(h_ref, w_ref, b_ref, o_ref):
    o_ref[...] = _dot(h_ref[...], w_ref[...]) + b_ref[...]


def _win(h, w, b, layer):
    m, d = h.shape
    n = w.shape[-1]
    bm = _pick(m, (1024, 512, 256, 128, 64))
    bn = _pick(n, (512, 256, 128))
    return pl.pallas_call(
        _win_kernel,
        grid=(m // bm, n // bn),
        in_specs=[pl.BlockSpec((bm, d), lambda i, c: (i, 0)),
                  pl.BlockSpec((None, d, bn), lambda i, c: (layer, 0, c)),
                  pl.BlockSpec((None, 1, bn), lambda i, c: (layer, 0, c))],
        out_specs=pl.BlockSpec((bm, bn), lambda i, c: (i, c)),
        out_shape=jax.ShapeDtypeStruct((m, n), F32),
        compiler_params=_params("arbitrary", "arbitrary"),
    )(h, w, b)


def _gates_kernel(x_ref, a_ref, dt_ref, gc_ref, grc_ref, grf_ref, carry_ref, *, hm, hf, hg, lc, seg, tb):
    t = pl.program_id(1)

    @pl.when(t == 0)
    def _():
        carry_ref[...] = jnp.zeros_like(carry_ref)

    z = jnp.transpose(x_ref[...])[:GATE_ROWS]
    ch = lax.broadcasted_iota(jnp.int32, (GATE_ROWS, 1), 0)
    is_mi = ch < hm
    is_mf = (ch >= hm) & (ch < 2 * hm)
    is_ff = (ch >= 2 * hm) & (ch < 2 * hm + hf)
    is_ga = (ch >= 2 * hm + hf) & (ch < 2 * hm + hf + hg)
    is_gb = (ch >= 2 * hm + hf + hg) & (ch < 2 * hm + hf + 2 * hg)
    ls = _log_sigmoid(z)
    gg = -jnp.exp(a_ref[...]) * _softplus(z + dt_ref[...])
    sg = _sigmoid(z)
    s_i = lax.broadcasted_iota(jnp.int32, (tb, tb), 0)
    t_i = lax.broadcasted_iota(jnp.int32, (tb, tb), 1)
    m_loc = jnp.where((s_i <= t_i) & ((s_i // lc) == (t_i // lc)), 1.0, 0.0).astype(F32)
    m_seg = jnp.where((s_i <= t_i) & ((s_i // seg) == (t_i // seg)), 1.0, 0.0).astype(F32)
    v_loc = jnp.where(is_mf, ls, jnp.where(is_ga, gg, 0.0))
    v_seg = jnp.where(is_ff, ls, 0.0)
    cum_loc = _dot(v_loc, m_loc, HI)
    cum_seg = _dot(v_seg, m_seg, HI) + carry_ref[...]
    if seg > tb:
        carry_ref[...] = cum_seg[:, tb - 1:tb]
    bank0 = jnp.where(is_mi, z, jnp.where(is_mf | is_ga, cum_loc,
                                          jnp.where(is_ff, cum_seg, jnp.where(is_gb, sg, 0.0))))
    bank1 = jnp.where(is_ff, ls, 0.0)
    rows = jnp.concatenate([bank0, bank1], axis=0)
    full = jnp.concatenate([rows, jnp.zeros((LANES - 2 * GATE_ROWS, tb), F32)], axis=0)
    gc_ref[...] = jnp.transpose(full)
    for c in range(tb // lc):
        grc_ref[c] = rows[:, c * lc:(c + 1) * lc]
    grf_ref[...] = bank0[2 * hm:2 * hm + hf]


def _gates(proj, gate_blk, a_col, dt_col, layer, nb, t, lc, seg, dims):
    hm, hf, hg = dims
    tb = _pick(t, (512, 256, 128, 64))
    nt = t // tb
    return pl.pallas_call(
        functools.partial(_gates_kernel, hm=hm, hf=hf, hg=hg, lc=lc, seg=seg, tb=tb),
        grid=(nb, nt),
        in_specs=[pl.BlockSpec((tb, LANES), lambda b, i: (b * nt + i, gate_blk)),
                  pl.BlockSpec((None, GATE_ROWS, 1), lambda b, i: (layer, 0, 0)),
                  pl.BlockSpec((None, GATE_ROWS, 1), lambda b, i: (layer, 0, 0))],
        out_specs=[pl.BlockSpec((tb, LANES), lambda b, i: (b * nt + i, 0)),
                   pl.BlockSpec((None, tb // lc, 2 * GATE_ROWS, lc), lambda b, i: (b, i, 0, 0)),
                   pl.BlockSpec((None, hf, tb), lambda b, i: (b, 0, i))],
        out_shape=[jax.ShapeDtypeStruct((nb * t, LANES), F32),
                   jax.ShapeDtypeStruct((nb, t // lc, 2 * GATE_ROWS, lc), F32),
                   jax.ShapeDtypeStruct((nb, hf, t), F32)],
        scratch_shapes=[pltpu.VMEM((GATE_ROWS, 1), F32)],
        compiler_params=_params("arbitrary", "arbitrary"),
    )(proj, a_col, dt_col)


def _mlstm_kernel(q_ref, k_ref, v_ref, o_ref, gc_ref, gr_ref, ng_ref, c0_ref, n0_ref, m0_ref,
                  h_ref, c_ref, n_ref, m_ref, *, hm, lc, mxu_dtype):
    @pl.when(pl.program_id(1) == 0)
    def _():
        c_ref[...] = c0_ref[...]
        n_ref[...] = n0_ref[...]
        m_ref[...] = m0_ref[...]

    gc = gc_ref[...]
    gr = gr_ref[...]
    r_i = lax.broadcasted_iota(jnp.int32, (lc, lc), 0)
    c_i = lax.broadcasted_iota(jnp.int32, (lc, lc), 1)
    causal = c_i <= r_i
    scale = HEAD_DIM ** -0.5
    for h in range(hm):
        sl = slice(h * HEAD_DIM, (h + 1) * HEAD_DIM)
        q = q_ref[:, sl]
        k = k_ref[:, sl] * scale
        v = v_ref[:, sl]
        li_col = gc[:, h:h + 1]
        b_col = gc[:, hm + h:hm + h + 1]
        li_row = gr[h:h + 1, :]
        b_row = gr[hm + h:hm + h + 1, :]
        cst = c_ref[h]
        nst = n_ref[h]
        mst = m_ref[h]
        dmat = jnp.where(causal, b_col - b_row + li_row, NEG_INF)
        inter = b_col + mst
        m_row = jnp.maximum(inter, jnp.max(dmat, axis=-1, keepdims=True))
        w_intra = jnp.exp(dmat - m_row)
        w_inter = jnp.exp(inter - m_row)
        qx = q.astype(mxu_dtype)
        kx = k.astype(mxu_dtype)
        vx = v.astype(mxu_dtype)
        s = _dot_nt(qx, kx) * w_intra
        num = _dot(s.astype(mxu_dtype), vx) + w_inter * _dot(qx, cst.astype(mxu_dtype))
        den = jnp.sum(s, axis=-1, keepdims=True) + w_inter * jnp.sum(q * nst, axis=-1, keepdims=True)
        hh = num / jnp.maximum(jnp.abs(den), jnp.exp(-m_row))
        b_last = b_col[lc - 1:lc, :]
        d_last = b_last - b_col + li_col
        m_new = jnp.maximum(b_last + mst, jnp.max(d_last, axis=0, keepdims=True))
        w_k = jnp.exp(d_last - m_new)
        decay = jnp.exp(b_last + mst - m_new)
        kw = k * w_k
        c_ref[h] = decay * cst + _dot_tn(kw.astype(mxu_dtype), vx)
        n_ref[h] = decay * nst + jnp.sum(kw, axis=0, keepdims=True)
        m_ref[h] = m_new
        mu = jnp.mean(hh, axis=-1, keepdims=True)
        hc = hh - mu
        var = jnp.mean(hc * hc, axis=-1, keepdims=True)
        hn = hc * lax.rsqrt(var + LN_EPS) * ng_ref[h]
        h_ref[:, sl] = (hn * _sigmoid(o_ref[:, sl])).astype(h_ref.dtype)


def _mlstm(proj, gc, grc, norm_g, c0, n0, m0, layer, nb, t, dims):
    hm = dims[0]
    dm = hm * HEAD_DIM
    lc = math.gcd(t, CHUNK)
    nc = t // lc
    mxu_dtype = BF16 if lc >= 16 else F32

    def col(j):
        return pl.BlockSpec((lc, dm), lambda b, c: (b * nc + c, j))

    st4 = pl.BlockSpec((None, hm, HEAD_DIM, HEAD_DIM), lambda b, c: (b, 0, 0, 0))
    stn = pl.BlockSpec((None, hm, 1, HEAD_DIM), lambda b, c: (b, 0, 0, 0))
    stm = pl.BlockSpec((None, hm, 1, 1), lambda b, c: (b, 0, 0, 0))
    return pl.pallas_call(
        functools.partial(_mlstm_kernel, hm=hm, lc=lc, mxu_dtype=mxu_dtype),
        grid=(nb, nc),
        in_specs=[col(0), col(1), col(2), col(3),
                  pl.BlockSpec((lc, LANES), lambda b, c: (b * nc + c, 0)),
                  pl.BlockSpec((None, None, 2 * GATE_ROWS, lc), lambda b, c: (b, c, 0, 0)),
                  pl.BlockSpec((None, hm, 1, HEAD_DIM), lambda b, c: (layer, 0, 0, 0)),
                  st4, stn, stm],
        out_specs=[pl.BlockSpec((lc, dm), lambda b, c: (b * nc + c, 0)), st4, stn, stm],
        out_shape=[jax.ShapeDtypeStruct((nb * t, dm), mxu_dtype),
                   jax.ShapeDtypeStruct((nb, hm, HEAD_DIM, HEAD_DIM), F32),
                   jax.ShapeDtypeStruct((nb, hm, 1, HEAD_DIM), F32),
                   jax.ShapeDtypeStruct((nb, hm, 1, 1), F32)],
        compiler_params=_params("arbitrary", "arbitrary"),
    )(proj, proj, proj, proj, gc, grc, norm_g, c0, n0, m0)


def _unit_lower_inverse(a, lc):
    r_i = lax.broadcasted_iota(jnp.int32, (lc, lc), 0)
    c_i = lax.broadcasted_iota(jnp.int32, (lc, lc), 1)
    eye = jnp.where(r_i == c_i, 1.0, 0.0).astype(F32)
    npow = -a
    inv = eye + npow
    width = 2
    while width < lc:
        npow = _dot(npow, npow, HI)
        inv = inv + _dot(inv, npow, HI)
        width *= 2
    return inv


def _gdn_kernel(q_ref, k_ref, v_ref, z_ref, gc_ref, gr_ref, cw_ref, ng_ref, s0_ref, b0_ref,
                h_ref, s_ref, buf_ref, xp_ref, *, hm, hf, hg, lc, mxu_dtype):
    dg = hg * HEAD_DIM
    keep = SUBLANES

    @pl.when(pl.program_id(1) == 0)
    def _():
        s_ref[...] = s0_ref[...]
        xp_ref[0:keep, :] = jnp.zeros((keep, 3 * dg), F32)
        xp_ref[keep - 3:keep, :] = b0_ref[...]

    xp_ref[keep:keep + lc, 0:dg] = q_ref[...]
    xp_ref[keep:keep + lc, dg:2 * dg] = k_ref[...]
    xp_ref[keep:keep + lc, 2 * dg:3 * dg] = v_ref[...]
    conv = xp_ref[keep:keep + lc, :] * cw_ref[3:4, :]
    for j in range(3):
        conv = conv + xp_ref[keep - 3 + j:keep - 3 + j + lc, :] * cw_ref[j:j + 1, :]
    buf_ref[...] = xp_ref[keep + lc - 3:keep + lc, :]
    tail = xp_ref[lc:lc + keep, :]
    xp_ref[0:keep, :] = tail
    act = _silu(conv)

    gc = gc_ref[...]
    gr = gr_ref[...]
    r_i = lax.broadcasted_iota(jnp.int32, (lc, lc), 0)
    c_i = lax.broadcasted_iota(jnp.int32, (lc, lc), 1)
    incl = c_i <= r_i
    strict = c_i < r_i
    ch_g = 2 * hm + hf
    ch_b = ch_g + hg
    scale = HEAD_DIM ** -0.5
    for h in range(hg):
        sl = slice(h * HEAD_DIM, (h + 1) * HEAD_DIM)
        cq = act[:, h * HEAD_DIM:(h + 1) * HEAD_DIM]
        ck = act[:, dg + h * HEAD_DIM:dg + (h + 1) * HEAD_DIM]
        cv = act[:, 2 * dg + h * HEAD_DIM:2 * dg + (h + 1) * HEAD_DIM]
        qn = cq * lax.rsqrt(jnp.sum(cq * cq, axis=-1, keepdims=True) + NORM_EPS) * scale
        kn = ck * lax.rsqrt(jnp.sum(ck * ck, axis=-1, keepdims=True) + NORM_EPS)
        g_col = gc[:, ch_g + h:ch_g + h + 1]
        g_row = gr[ch_g + h:ch_g + h + 1, :]
        beta = gc[:, ch_b + h:ch_b + h + 1]
        sst = s_ref[h]
        decay = jnp.exp(jnp.where(incl, g_col - g_row, NEG_INF))
        kk = _dot_nt(kn, kn, HI)
        amat = jnp.where(strict, beta * kk * decay, 0.0)
        eg = jnp.exp(g_col)
        rhs = jnp.concatenate([kn * (beta * eg), cv * beta], axis=-1)
        sol = _dot(_unit_lower_inverse(amat, lc), rhs, HI)
        w = sol[:, :HEAD_DIM]
        sx = sst.astype(mxu_dtype)
        u = sol[:, HEAD_DIM:] - _dot(w.astype(mxu_dtype), sx)
        ux = u.astype(mxu_dtype)
        qk = _dot_nt(qn.astype(mxu_dtype), kn.astype(mxu_dtype)) * decay
        o = _dot((qn * eg).astype(mxu_dtype), sx) + _dot(qk.astype(mxu_dtype), ux)
        g_last = g_col[lc - 1:lc, :]
        s_ref[h] = jnp.exp(g_last) * sst + _dot_tn((kn * jnp.exp(g_last - g_col)).astype(mxu_dtype), ux)
        on = o * lax.rsqrt(jnp.mean(o * o, axis=-1, keepdims=True) + NORM_EPS) * ng_ref[...]
        h_ref[:, sl] = (on * _silu(z_ref[:, sl])).astype(h_ref.dtype)


def _gdn(proj, gc, grc, conv_w, norm_g, s0, b0, layer, nb, t, dims):
    hm, hf, hg = dims
    dg = hg * HEAD_DIM
    lc = math.gcd(t, CHUNK)
    nc = t // lc
    mxu_dtype = BF16 if lc >= 16 else F32
    base = (4 * hm * HEAD_DIM + 3 * hf * HEAD_DIM) // dg

    def col(j):
        return pl.BlockSpec((lc, dg), lambda b, c: (b * nc + c, base + j))

    st4 = pl.BlockSpec((None, hg, HEAD_DIM, HEAD_DIM), lambda b, c: (b, 0, 0, 0))
    stb = pl.BlockSpec((None, 3, 3 * dg), lambda b, c: (b, 0, 0))
    return pl.pallas_call(
        functools.partial(_gdn_kernel, hm=hm, hf=hf, hg=hg, lc=lc, mxu_dtype=mxu_dtype),
        grid=(nb, nc),
        in_specs=[col(0), col(1), col(2), col(3),
                  pl.BlockSpec((lc, LANES), lambda b, c: (b * nc + c, 0)),
                  pl.BlockSpec((None, None, 2 * GATE_ROWS, lc), lambda b, c: (b, c, 0, 0)),
                  pl.BlockSpec((None, 4, 3 * dg), lambda b, c: (layer, 0, 0)),
                  pl.BlockSpec((None, 1, HEAD_DIM), lambda b, c: (layer, 0, 0)),
                  st4, stb],
        out_specs=[pl.BlockSpec((lc, dg), lambda b, c: (b * nc + c, 0)), st4, stb],
        out_shape=[jax.ShapeDtypeStruct((nb * t, dg), mxu_dtype),
                   jax.ShapeDtypeStruct((nb, hg, HEAD_DIM, HEAD_DIM), F32),
                   jax.ShapeDtypeStruct((nb, 3, 3 * dg), F32)],
        scratch_shapes=[pltpu.VMEM((lc + SUBLANES, 3 * dg), F32)],
        compiler_params=_params("arbitrary", "arbitrary"),
    )(proj, proj, proj, proj, gc, grc, conv_w, norm_g, s0, b0)


def _fox_prompt_kernel(q_ref, k_ref, v_ref, gc_ref, gr_ref, o_ref, m_s, l_s, acc_s, *, hm, hf, tq, tk):
    qi = pl.program_id(1)
    ki = pl.program_id(2)

    @pl.when(ki == 0)
    def _():
        m_s[...] = jnp.full(m_s.shape, NEG_INF, F32)
        l_s[...] = jnp.zeros(l_s.shape, F32)
        acc_s[...] = jnp.zeros(acc_s.shape, F32)

    @pl.when(ki * tk < (qi + 1) * tq)
    def _():
        scale = HEAD_DIM ** -0.5
        row = qi * tq + lax.broadcasted_iota(jnp.int32, (tq, tk), 0)
        colp = ki * tk + lax.broadcasted_iota(jnp.int32, (tq, tk), 1)
        visible = colp <= row
        for h in range(hf):
            sl = slice(h * HEAD_DIM, (h + 1) * HEAD_DIM)
            q = q_ref[:, sl].astype(BF16)
            k = k_ref[:, sl].astype(BF16)
            v = v_ref[:, sl].astype(BF16)
            c_col = gc_ref[:, 2 * hm + h:2 * hm + h + 1]
            c_row = gr_ref[h:h + 1, :]
            s = _dot_nt(q, k) * scale + (c_col - c_row)
            s = jnp.where(visible, s, NEG_INF)
            m_prev = m_s[h]
            m_new = jnp.maximum(m_prev, jnp.max(s, axis=-1, keepdims=True))
            alpha = jnp.exp(m_prev - m_new)
            p = jnp.exp(s - m_new)
            l_s[h] = alpha * l_s[h] + jnp.sum(p, axis=-1, keepdims=True)
            acc_s[:, sl] = alpha * acc_s[:, sl] + _dot(p.astype(BF16), v)
            m_s[h] = m_new

    @pl.when(ki == ((qi + 1) * tq - 1) // tk)
    def _():
        for h in range(hf):
            sl = slice(h * HEAD_DIM, (h + 1) * HEAD_DIM)
            o_ref[:, sl] = (acc_s[:, sl] / l_s[h]).astype(BF16)


def _fox_prompt(proj, gc, grf, nb, t, dims):
    hm, hf, hg = dims
    df = hf * HEAD_DIM
    base = (4 * hm * HEAD_DIM) // df
    tq = _pick(t, (512, 256, 128))
    tk = tq
    nq = t // tq

    def last_k(i):
        return ((i + 1) * tq - 1) // tk

    return pl.pallas_call(
        functools.partial(_fox_prompt_kernel, hm=hm, hf=hf, tq=tq, tk=tk),
        grid=(nb, nq, t // tk),
        in_specs=[pl.BlockSpec((tq, df), lambda b, i, j: (b * nq + i, base)),
                  pl.BlockSpec((tk, df), lambda b, i, j: (b * (t // tk) + jnp.minimum(j, last_k(i)), base + 1)),
                  pl.BlockSpec((tk, df), lambda b, i, j: (b * (t // tk) + jnp.minimum(j, last_k(i)), base + 2)),
                  pl.BlockSpec((tq, LANES), lambda b, i, j: (b * nq + i, 0)),
                  pl.BlockSpec((None, hf, tk), lambda b, i, j: (b, 0, jnp.minimum(j, last_k(i))))],
        out_specs=pl.BlockSpec((tq, df), lambda b, i, j: (b * nq + i, 0)),
        out_shape=jax.ShapeDtypeStruct((nb * t, df), BF16),
        scratch_shapes=[pltpu.VMEM((hf, tq, 1), F32), pltpu.VMEM((hf, tq, 1), F32), pltpu.VMEM((tq, df), F32)],
        compiler_params=_params("arbitrary", "arbitrary", "arbitrary"),
    )(proj, proj, proj, gc, grf)


def _fox_pool_kernel(x_ref, w_ref, t_ref, *, hf, rpp, rb):
    x = x_ref[...]
    li = lax.broadcasted_iota(jnp.int32, (LANES, LANES), 0)
    lj = lax.broadcasted_iota(jnp.int32, (LANES, LANES), 1)
    same_head = (li % hf) == (lj % hf)
    m_row = jnp.where(same_head & (li > lj), 1.0, 0.0).astype(F32)
    h_row = jnp.where(same_head, 1.0, 0.0).astype(F32)
    ri = lax.broadcasted_iota(jnp.int32, (rb, rb), 0)
    rj = lax.broadcasted_iota(jnp.int32, (rb, rb), 1)
    same_page = (ri // rpp) == (rj // rpp)
    s_below = jnp.where(same_page & (rj > ri), 1.0, 0.0).astype(F32)
    s_all = jnp.where(same_page, 1.0, 0.0).astype(F32)
    row_tot = _dot(x, h_row, HI)
    w_ref[...] = _dot(x, m_row, HI) + _dot(s_below, row_tot, HI)
    t_ref[...] = _dot(s_all, row_tot, HI)


def _fox_pool(logf2d, hf, rpp):
    r = logf2d.shape[0]
    rb = _pick(r, (512, 256, 128, 64, 32, 16, 8))
    spec = pl.BlockSpec((rb, LANES), lambda i: (i, 0))
    return pl.pallas_call(
        functools.partial(_fox_pool_kernel, hf=hf, rpp=rpp, rb=rb),
        grid=(r // rb,),
        in_specs=[spec],
        out_specs=[spec, spec],
        out_shape=[jax.ShapeDtypeStruct((r, LANES), F32)] * 2,
        compiler_params=_params("arbitrary"),
    )(logf2d)


def _fox_sample_kernel(pt_ref, q_ref, cc_ref, cr_ref, kn_ref, vn_ref, kp_ref, vp_ref, w_ref, t_ref,
                       o_ref, m_s, l_s, acc_s, carry_s, *, hf, npg, rpp, mxu_dtype):
    p = pl.program_id(1)
    scale = HEAD_DIM ** -0.5
    q = q_ref[...].astype(mxu_dtype)
    nr = q.shape[0]
    c_col = cc_ref[...]

    @pl.when(p == 0)
    def _():
        ri = lax.broadcasted_iota(jnp.int32, (nr, nr), 0)
        ci = lax.broadcasted_iota(jnp.int32, (nr, nr), 1)
        ok = ((ri % hf) == (ci % hf)) & ((ci // hf) <= (ri // hf))
        s = _dot_nt(q, kn_ref[...].astype(mxu_dtype)) * scale + (c_col - cr_ref[...])
        s = jnp.where(ok, s, NEG_INF)
        m0 = jnp.max(s, axis=-1, keepdims=True)
        e = jnp.exp(s - m0)
        m_s[...] = m0
        l_s[...] = jnp.sum(e, axis=-1, keepdims=True)
        acc_s[...] = _dot(e.astype(mxu_dtype), vn_ref[...].astype(mxu_dtype))
        carry_s[...] = jnp.zeros_like(carry_s)

    ncol = rpp * LANES
    carry = carry_s[...]
    bias_row = jnp.concatenate([w_ref[r:r + 1, :] + carry for r in range(rpp)], axis=1)
    ri = lax.broadcasted_iota(jnp.int32, (nr, ncol), 0)
    ci = lax.broadcasted_iota(jnp.int32, (nr, ncol), 1)
    ok = (ri % hf) == (ci % hf)
    s = _dot_nt(q, kp_ref[...].astype(mxu_dtype)) * scale + (c_col + bias_row)
    s = jnp.where(ok, s, NEG_INF)
    m_prev = m_s[...]
    m_new = jnp.maximum(m_prev, jnp.max(s, axis=-1, keepdims=True))
    alpha = jnp.exp(m_prev - m_new)
    e = jnp.exp(s - m_new)
    l_s[...] = alpha * l_s[...] + jnp.sum(e, axis=-1, keepdims=True)
    acc_s[...] = alpha * acc_s[...] + _dot(e.astype(mxu_dtype), vp_ref[...].astype(mxu_dtype))
    m_s[...] = m_new
    carry_s[...] = carry + t_ref[0:1, :]

    @pl.when(p == npg - 1)
    def _():
        o_ref[...] = (acc_s[...] / l_s[...]).astype(BF16)


def _fox_sample(page_table, q2, cum_col, cum_row, kn2, vn2, kpool, vpool, wpool, tpool, layer, hf):
    nb, npg = page_table.shape
    nr = q2.shape[1]
    nphys, pcols = kpool.shape[1], kpool.shape[2]
    rpp = pcols // LANES

    def page(b, p, pt):
        return pt[b * npg + (npg - 1 - p)]

    per_b3 = lambda b, p, pt: (b, 0, 0)
    return pl.pallas_call(
        functools.partial(_fox_sample_kernel, hf=hf, npg=npg, rpp=rpp, mxu_dtype=BF16),
        grid_spec=pltpu.PrefetchScalarGridSpec(
            num_scalar_prefetch=1,
            grid=(nb, npg),
            in_specs=[pl.BlockSpec((None, nr, HEAD_DIM), per_b3),
                      pl.BlockSpec((None, nr, 1), per_b3),
                      pl.BlockSpec((None, 1, nr), per_b3),
                      pl.BlockSpec((None, nr, HEAD_DIM), per_b3),
                      pl.BlockSpec((None, nr, HEAD_DIM), per_b3),
                      pl.BlockSpec((None, None, pcols, HEAD_DIM), lambda b, p, pt: (layer, page(b, p, pt), 0, 0)),
                      pl.BlockSpec((None, None, pcols, HEAD_DIM), lambda b, p, pt: (layer, page(b, p, pt), 0, 0)),
                      pl.BlockSpec((None, rpp, LANES), lambda b, p, pt: (layer * nphys + page(b, p, pt), 0, 0)),
                      pl.BlockSpec((None, rpp, LANES), lambda b, p, pt: (layer * nphys + page(b, p, pt), 0, 0))],
            out_specs=pl.BlockSpec((None, nr, HEAD_DIM), per_b3),
            scratch_shapes=[pltpu.VMEM((nr, 1), F32), pltpu.VMEM((nr, 1), F32),
                            pltpu.VMEM((nr, HEAD_DIM), F32), pltpu.VMEM((1, LANES), F32)]),
        out_shape=jax.ShapeDtypeStruct((nb, nr, HEAD_DIM), BF16),
        compiler_params=_params("arbitrary", "arbitrary"),
    )(page_table.reshape(-1), q2, cum_col, cum_row, kn2, vn2, kpool, vpool, wpool, tpool)


def _regroup_w_in(w_in, b_in, dims):
    hm, hf, hg = dims
    dm, df, dg = hm * HEAD_DIM, hf * HEAD_DIM, hg * HEAD_DIM
    sizes = (dm, dm, dm, dm, hm, hm, df, df, df, hf, dg, dg, dg, dg, hg, hg)
    offs = [0]
    for s in sizes:
        offs.append(offs[-1] + s)
    order = (0, 1, 2, 3, 6, 7, 8, 10, 11, 12, 13, 4, 5, 9, 14, 15)
    n_main = 4 * dm + 3 * df + 4 * dg
    n_gate = 2 * hm + hf + 2 * hg
    n_tot = -(-(n_main + LANES) // 512) * 512
    pad = n_tot - n_main - n_gate

    def regroup(a):
        parts = [a[..., offs[i]:offs[i + 1]] for i in order]
        parts.append(jnp.zeros(a.shape[:-1] + (pad,), a.dtype))
        return jnp.concatenate(parts, axis=-1)

    return regroup(w_in).astype(BF16), regroup(b_in)[:, None, :], n_main // LANES


def _gate_param_cols(a_log, dt_bias, dims):
    hm, hf, hg = dims
    depth = a_log.shape[0]
    lead = 2 * hm + hf
    z0 = jnp.zeros((depth, lead), F32)
    z1 = jnp.zeros((depth, GATE_ROWS - lead - hg), F32)
    a_col = jnp.concatenate([z0, a_log, z1], axis=1)[:, :, None]
    dt_col = jnp.concatenate([z0, dt_bias, z1], axis=1)[:, :, None]
    return a_col, dt_col


def kernel(x_prompt, x_sample, cache_fox_k, cache_fox_v, cache_fox_logf, state_mlstm_C, state_mlstm_n, state_mlstm_m, state_gdn_S, state_gdn_conv, page_table, c_prompt, c_sample, w_ada, b_ada, ln_g, ln_b, ffn_w_gate, ffn_w_up, ffn_w_down, w_in, b_in, mlstm_norm_g, gdn_conv_w, gdn_A_log, gdn_dt_bias, gdn_norm_g, w_out):
    bp, tp, d = x_prompt.shape
    bs, ts, _ = x_sample.shape
    depth = w_ada.shape[0]
    hm, hf, hg = state_mlstm_C.shape[2], cache_fox_k.shape[3], state_gdn_S.shape[2]
    dims = (hm, hf, hg)
    dm, df, dg = hm * HEAD_DIM, hf * HEAD_DIM, hg * HEAD_DIM
    nphys, page = cache_fox_k.shape[1], cache_fox_k.shape[2]
    alpha = (2 * depth) ** 0.25
    assert 2 * hm + hf + 2 * hg <= GATE_ROWS and (page * hf) % LANES == 0 and LANES % hf == 0

    nrow = -(-(bp + bs) // 16) * 16
    c_all = jnp.concatenate([c_prompt, c_sample, jnp.zeros((nrow - bp - bs, d), F32)], axis=0)
    mod = _ada(c_all, w_ada, b_ada).reshape(depth, nrow, 3 * N_SUB, d)
    mod = jnp.transpose(mod, (0, 2, 1, 3))
    mod_p = mod[:, :, :bp, None, :]
    mod_s = jnp.repeat(mod[:, :, bp:bp + bs], ts, axis=2)[:, :, None, :, :]

    w_in_r, b_in_r, gate_blk = _regroup_w_in(w_in, b_in, dims)
    a_col, dt_col = _gate_param_cols(gdn_A_log, gdn_dt_bias, dims)
    ln_g4 = ln_g[:, :, None, :]
    ln_b4 = ln_b[:, :, None, :]
    norm_m = mlstm_norm_g[:, :, None, :]
    norm_g = gdn_norm_g[:, None, :]

    rpp = (page * hf) // LANES
    wpool, tpool = _fox_pool(cache_fox_logf.reshape(depth * nphys * rpp, LANES), hf, rpp)
    wpool = wpool.reshape(depth * nphys, rpp, LANES)
    tpool = tpool.reshape(depth * nphys, rpp, LANES)
    kpool = cache_fox_k.reshape(depth, nphys, page * hf, HEAD_DIM)
    vpool = cache_fox_v.reshape(depth, nphys, page * hf, HEAD_DIM)

    o_f = 4 * dm
    o_g = o_f + 3 * df

    def layer_step(l, x, h, mod_x, nb, t, mlstm_state, gdn_state, sample):
        m_rows = nb * t
        grp = m_rows if sample else t
        a = _gateup(h, ffn_w_gate, ffn_w_up, l, 0)
        x, h = _proj_ln(a, ffn_w_down, (l, 0), x, mod_x, l, 0, (l, 1), ln_g4, ln_b4, grp, alpha, MACARON_W)
        proj = _win(h, w_in_r, b_in_r, l)
        lc = math.gcd(t, CHUNK)
        if sample:
            gc, grc, grf = _gates(proj, gate_blk, a_col, dt_col, l, 1, m_rows, lc, t, dims)
            grc = grc.reshape(nb, t // lc, 2 * GATE_ROWS, lc)
        else:
            gc, grc, grf = _gates(proj, gate_blk, a_col, dt_col, l, nb, t, lc, t, dims)
        c0, n0, m0 = mlstm_state
        hmx, c1, n1, m1 = _mlstm(proj, gc, grc, norm_m, c0, n0[:, :, None, :], m0[:, :, None, None], l, nb, t, dims)
        s0, b0 = gdn_state
        hgx, s1, b1 = _gdn(proj, gc, grc, gdn_conv_w, norm_g, s0, b0, l, nb, t, dims)
        fk = proj[:, o_f + df:o_f + 2 * df]
        fv = proj[:, o_f + 2 * df:o_f + 3 * df]
        logf = gc[:, GATE_ROWS + 2 * hm:GATE_ROWS + 2 * hm + hf]
        if sample:
            nr = t * hf
            q2 = proj[:, o_f:o_f + df].reshape(nb, nr, HEAD_DIM)
            cum = gc[:, 2 * hm:2 * hm + hf].reshape(nb, nr)
            hfx = _fox_sample(page_table, q2, cum[:, :, None], cum[:, None, :], fk.reshape(nb, nr, HEAD_DIM),
                              fv.reshape(nb, nr, HEAD_DIM), kpool, vpool, wpool, tpool, l, hf)
            hfx = hfx.reshape(m_rows, df)
        else:
            hfx = _fox_prompt(proj, gc, grf, nb, t, dims)
        mixed = jnp.concatenate([hmx.astype(BF16), hfx, hgx.astype(BF16)], axis=-1)
        x, h = _proj_ln(mixed, w_out, (l,), x, mod_x, l, 1, (l, 2), ln_g4, ln_b4, grp, alpha, 1.0)
        a = _gateup(h, ffn_w_gate, ffn_w_up, l, 1)
        nxt = (l + 1, 0) if l + 1 < depth else None
        x, h = _proj_ln(a, ffn_w_down, (l, 1), x, mod_x, l, 2, nxt, ln_g4, ln_b4, grp, alpha, MACARON_W)
        new_state = (fk.reshape(nb, t, hf, HEAD_DIM), fv.reshape(nb, t, hf, HEAD_DIM), logf.reshape(nb, t, hf),
                     c1, n1[:, :, 0, :], m1[:, :, 0, 0], s1, b1)
        return x, h, new_state

    xp = x_prompt.reshape(bp * tp, d)
    xs = x_sample.reshape(bs * ts, d)
    hp = _modulate(xp, mod_p, 0, 0, tp)
    hs = _modulate(xs, mod_s, 0, 0, bs * ts)
    zero_mlstm = (jnp.zeros((bp, hm, HEAD_DIM, HEAD_DIM), F32), jnp.zeros((bp, hm, HEAD_DIM), F32),
                  jnp.zeros((bp, hm), F32))
    zero_gdn = (jnp.zeros((bp, hg, HEAD_DIM, HEAD_DIM), F32), jnp.zeros((bp, 3, 3 * dg), F32))
    st_p, st_s = [], []
    for l in range(depth):
        xp, hp, sp = layer_step(l, xp, hp, mod_p, bp, tp, zero_mlstm, zero_gdn, False)
        xs, hs, ss = layer_step(l, xs, hs, mod_s, bs, ts,
                                (state_mlstm_C[l], state_mlstm_n[l], state_mlstm_m[l]),
                                (state_gdn_S[l], state_gdn_conv[l]), True)
        st_p.append(sp)
        st_s.append(ss)
    pn = [jnp.stack(a) for a in zip(*st_p)]
    sn = [jnp.stack(a) for a in zip(*st_s)]
    return (xp.reshape(bp, tp, d), xs.reshape(bs, ts, d), pn[0], pn[1], pn[2], pn[3], pn[4], pn[5], pn[6], pn[7],
            sn[0], sn[1], sn[2], sn[3], sn[4], sn[5], sn[6], sn[7])
```

```python
import functools
import math

import jax
import jax.numpy as jnp
from jax import lax
from jax.experimental import pallas as pl
from jax.experimental.pallas import tpu as pltpu

F32 = jnp.float32
BF16 = jnp.bfloat16

HEAD_DIM = 128
LANES = 128
SUBLANES = 8
CHUNK = 64
N_SUB = 3
LN_EPS = 1e-5
NORM_EPS = 1e-6
MACARON_W = 0.5
GATE_ROWS = 32
VMEM_LIMIT_BYTES = 56 * 1024 * 1024
HI = lax.Precision.HIGHEST
NEG_INF = float("-inf")


def _params(*sem):
    return pltpu.CompilerParams(dimension_semantics=sem, vmem_limit_bytes=VMEM_LIMIT_BYTES)


def _dot(a, b, precision=None):
    return jnp.dot(a, b, preferred_element_type=F32, precision=precision)


def _dot_nt(a, b, precision=None):
    return lax.dot_general(a, b, (((1,), (1,)), ((), ())), preferred_element_type=F32, precision=precision)


def _dot_tn(a, b, precision=None):
    return lax.dot_general(a, b, (((0,), (0,)), ((), ())), preferred_element_type=F32, precision=precision)


def _sigmoid(x):
    return 1.0 / (1.0 + jnp.exp(-x))


def _silu(x):
    return x * _sigmoid(x)


def _softplus(x):
    return jnp.maximum(x, 0.0) + jnp.log1p(jnp.exp(-jnp.abs(x)))


def _log_sigmoid(x):
    return -_softplus(-x)


def _pick(m, candidates):
    for c in candidates:
        if m % c == 0:
            return c
    return m


def _ada_kernel(c_ref, w_ref, b_ref, o_ref):
    c = c_ref[...]
    sc = _silu(c).astype(BF16)
    o_ref[...] = _dot(sc, w_ref[...].astype(BF16)) + b_ref[...]


def _ada(c_all, w_ada, b_ada):
    depth, d, n = w_ada.shape
    rows = c_all.shape[0]
    bn = _pick(n, (1024, 512, 256, 128))
    return pl.pallas_call(
        _ada_kernel,
        grid=(depth, n // bn),
        in_specs=[pl.BlockSpec((rows, d), lambda l, j: (0, 0)),
                  pl.BlockSpec((None, d, bn), lambda l, j: (l, 0, j)),
                  pl.BlockSpec((None, 1, bn), lambda l, j: (l, 0, j))],
        out_specs=pl.BlockSpec((None, rows, bn), lambda l, j: (l, 0, j)),
        out_shape=jax.ShapeDtypeStruct((depth, rows, n), F32),
        name="ada",
        compiler_params=_params("arbitrary", "arbitrary"),
    )(c_all, w_ada, b_ada.reshape(depth, 1, n))


def _modulate_kernel(x_ref, sc_ref, sh_ref, o_ref):
    o_ref[...] = (x_ref[...] * (1.0 + sc_ref[...]) + sh_ref[...]).astype(BF16)


def _mod_spec(mod, layer, slot, rows_per_group, bm):
    r, d = mod.shape[3], mod.shape[4]
    if r == 1:
        return pl.BlockSpec((None, None, None, 1, d),
                            lambda m, *_: (layer, slot, (m * bm) // rows_per_group, 0, 0))
    return pl.BlockSpec((None, None, None, bm, d), lambda m, *_: (layer, slot, 0, m, 0))


def _modulate(x, mod, layer, j, rows_per_group):
    m, d = x.shape
    bm = _pick(rows_per_group, (512, 256, 128, 64))
    return pl.pallas_call(
        _modulate_kernel,
        grid=(m // bm,),
        in_specs=[pl.BlockSpec((bm, d), lambda i: (i, 0)),
                  _mod_spec(mod, layer, 3 * j + 1, rows_per_group, bm),
                  _mod_spec(mod, layer, 3 * j + 0, rows_per_group, bm)],
        out_specs=pl.BlockSpec((bm, d), lambda i: (i, 0)),
        out_shape=jax.ShapeDtypeStruct((m, d), BF16),
        name="modulate",
        compiler_params=_params("arbitrary"),
    )(x, mod, mod)


def _gateup_kernel(h_ref, wg_ref, wu_ref, o_ref, wg_s, wu_s):
    @pl.when(pl.program_id(1) == 0)
    def _():
        wg_s[...] = wg_ref[...].astype(BF16)
        wu_s[...] = wu_ref[...].astype(BF16)

    h = h_ref[...]
    g = _dot(h, wg_s[...])
    u = _dot(h, wu_s[...])
    o_ref[...] = (_silu(g) * u).astype(BF16)


def _gateup(h, w_gate, w_up, layer, j):
    m, d = h.shape
    f = w_gate.shape[-1]
    bm = _pick(m, (1024, 512, 256, 128, 64))
    bn = _pick(f, (512, 256, 128))
    wspec = pl.BlockSpec((None, None, d, bn), lambda n, i: (layer, j, 0, n))
    return pl.pallas_call(
        _gateup_kernel,
        grid=(f // bn, m // bm),
        in_specs=[pl.BlockSpec((bm, d), lambda n, i: (i, 0)), wspec, wspec],
        out_specs=pl.BlockSpec((bm, bn), lambda n, i: (i, n)),
        out_shape=jax.ShapeDtypeStruct((m, f), BF16),
        scratch_shapes=[pltpu.VMEM((d, bn), BF16), pltpu.VMEM((d, bn), BF16)],
        name="gateup",
        compiler_params=_params("arbitrary", "arbitrary"),
    )(h, w_gate, w_up)


def _proj_ln_kernel(a_ref, w_ref, x_ref, gate_ref, lng_ref, lnb_ref, *rest, nk, bk, alpha, coef, emit_h):
    if emit_h:
        sc_ref, sh_ref, xo_ref, ho_ref, w_s = rest
    else:
        xo_ref, w_s = rest
    i = pl.program_id(0)

    @pl.when(i < nk)
    def _():
        w_s[pl.ds(pl.multiple_of(i * bk, bk), bk), :] = w_ref[...].astype(BF16)

    @pl.when(i >= nk)
    def _():
        acc = _dot(a_ref[...], w_s[...])
        y = alpha * x_ref[...] + (coef * (1.0 + gate_ref[...])) * acc
        mu = jnp.mean(y, axis=-1, keepdims=True)
        yc = y - mu
        var = jnp.mean(yc * yc, axis=-1, keepdims=True)
        xn = yc * lax.rsqrt(var + LN_EPS) * lng_ref[...] + lnb_ref[...]
        xo_ref[...] = xn
        if emit_h:
            ho_ref[...] = (xn * (1.0 + sc_ref[...]) + sh_ref[...]).astype(BF16)


def _proj_ln(a, w, w_prefix, x, mod, layer, j, nxt, ln_g, ln_b, rows_per_group, alpha, coef):
    m, k = a.shape
    d = x.shape[1]
    bm = _pick(rows_per_group, (256, 128, 64))
    bk = _pick(k, (512, 256, 128))
    nk = k // bk
    emit_h = nxt is not None
    npre = len(w_prefix)

    def row(i):
        return jnp.maximum(i - nk, 0)

    def shifted(spec_fn):
        return spec_fn

    in_specs = [
        pl.BlockSpec((bm, k), lambda i: (row(i), 0)),
        pl.BlockSpec((None,) * npre + (bk, d), lambda i: tuple(w_prefix) + (jnp.minimum(i, nk - 1), 0)),
        pl.BlockSpec((bm, d), lambda i: (row(i), 0)),
    ]

    def modspec(lyr, slot):
        r = mod.shape[3]
        if r == 1:
            return pl.BlockSpec((None, None, None, 1, d),
                                lambda i: (lyr, slot, (row(i) * bm) // rows_per_group, 0, 0))
        return pl.BlockSpec((None, None, None, bm, d), lambda i: (lyr, slot, 0, row(i), 0))

    in_specs.append(modspec(layer, 3 * j + 2))
    in_specs += [pl.BlockSpec((None, None, 1, d), lambda i: (layer, j, 0, 0)),
                 pl.BlockSpec((None, None, 1, d), lambda i: (layer, j, 0, 0))]
    args = [a, w, x, mod, ln_g, ln_b]
    out_specs = [pl.BlockSpec((bm, d), lambda i: (row(i), 0))]
    out_shape = [jax.ShapeDtypeStruct((m, d), F32)]
    if emit_h:
        in_specs += [modspec(nxt[0], 3 * nxt[1] + 1), modspec(nxt[0], 3 * nxt[1] + 0)]
        args += [mod, mod]
        out_specs.append(pl.BlockSpec((bm, d), lambda i: (row(i), 0)))
        out_shape.append(jax.ShapeDtypeStruct((m, d), BF16))
    out = pl.pallas_call(
        functools.partial(_proj_ln_kernel, nk=nk, bk=bk, alpha=alpha, coef=coef, emit_h=emit_h),
        grid=(nk + m // bm,),
        in_specs=in_specs,
        out_specs=out_specs,
        out_shape=out_shape,
        scratch_shapes=[pltpu.VMEM((k, d), BF16)],
        name="proj_ln",
        compiler_params=_params("arbitrary"),
    )(*args)
    return (out[0], out[1]) if emit_h else (out[0], None)


def _win_kernel(h_ref, w_ref, b_ref, o_ref):
    o_ref[...] = _dot(h_ref[...], w_ref[...]) + b_ref[...]


def _win(h, w, b, layer):
    m, d = h.shape
    n = w.shape[-1]
    bm = _pick(m, (1024, 512, 256, 128, 64))
    bn = _pick(n, (512, 256, 128))
    return pl.pallas_call(
        _win_kernel,
        grid=(m // bm, n // bn),
        in_specs=[pl.BlockSpec((bm, d), lambda i, c: (i, 0)),
                  pl.BlockSpec((None, d, bn), lambda i, c: (layer, 0, c)),
                  pl.BlockSpec((None, 1, bn), lambda i, c: (layer, 0, c))],
        out_specs=pl.BlockSpec((bm, bn), lambda i, c: (i, c)),
        out_shape=jax.ShapeDtypeStruct((m, n), F32),
        name="w_in",
        compiler_params=_params("arbitrary", "arbitrary"),
    )(h, w, b)


def _gates_kernel(x_ref, a_ref, dt_ref, gc_ref, grc_ref, grf_ref, carry_ref, *, hm, hf, hg, lc, seg, tb):
    t = pl.program_id(1)

    @pl.when(t == 0)
    def _():
        carry_ref[...] = jnp.zeros_like(carry_ref)

    z = jnp.transpose(x_ref[...])[:GATE_ROWS]
    ch = lax.broadcasted_iota(jnp.int32, (GATE_ROWS, 1), 0)
    is_mi = ch < hm
    is_mf = (ch >= hm) & (ch < 2 * hm)
    is_ff = (ch >= 2 * hm) & (ch < 2 * hm + hf)
    is_ga = (ch >= 2 * hm + hf) & (ch < 2 * hm + hf + hg)
    is_gb = (ch >= 2 * hm + hf + hg) & (ch < 2 * hm + hf + 2 * hg)
    ls = _log_sigmoid(z)
    gg = -jnp.exp(a_ref[...]) * _softplus(z + dt_ref[...])
    sg = _sigmoid(z)
    s_i = lax.broadcasted_iota(jnp.int32, (tb, tb), 0)
    t_i = lax.broadcasted_iota(jnp.int32, (tb, tb), 1)
    m_loc = jnp.where((s_i <= t_i) & ((s_i // lc) == (t_i // lc)), 1.0, 0.0).astype(F32)
    m_seg = jnp.where((s_i <= t_i) & ((s_i // seg) == (t_i // seg)), 1.0, 0.0).astype(F32)
    v_loc = jnp.where(is_mf, ls, jnp.where(is_ga, gg, 0.0))
    v_seg = jnp.where(is_ff, ls, 0.0)
    cum_loc = _dot(v_loc, m_loc, HI)
    cum_seg = _dot(v_seg, m_seg, HI) + carry_ref[...]
    if seg > tb:
        carry_ref[...] = cum_seg[:, tb - 1:tb]
    bank0 = jnp.where(is_mi, z, jnp.where(is_mf | is_ga, cum_loc,
                                          jnp.where(is_ff, cum_seg, jnp.where(is_gb, sg, 0.0))))
    bank1 = jnp.where(is_ff, ls, 0.0)
    rows = jnp.concatenate([bank0, bank1], axis=0)
    full = jnp.concatenate([rows, jnp.zeros((LANES - 2 * GATE_ROWS, tb), F32)], axis=0)
    gc_ref[...] = jnp.transpose(full)
    for c in range(tb // lc):
        grc_ref[c] = rows[:, c * lc:(c + 1) * lc]
    grf_ref[...] = bank0[2 * hm:2 * hm + hf]


def _gates(proj, gate_blk, a_col, dt_col, layer, nb, t, lc, seg, dims):
    hm, hf, hg = dims
    tb = _pick(t, (512, 256, 128, 64))
    nt = t // tb
    return pl.pallas_call(
        functools.partial(_gates_kernel, hm=hm, hf=hf, hg=hg, lc=lc, seg=seg, tb=tb),
        grid=(nb, nt),
        in_specs=[pl.BlockSpec((tb, LANES), lambda b, i: (b * nt + i, gate_blk)),
                  pl.BlockSpec((None, GATE_ROWS, 1), lambda b, i: (layer, 0, 0)),
                  pl.BlockSpec((None, GATE_ROWS, 1), lambda b, i: (layer, 0, 0))],
        out_specs=[pl.BlockSpec((tb, LANES), lambda b, i: (b * nt + i, 0)),
                   pl.BlockSpec((None, tb // lc, 2 * GATE_ROWS, lc), lambda b, i: (b, i, 0, 0)),
                   pl.BlockSpec((None, hf, tb), lambda b, i: (b, 0, i))],
        out_shape=[jax.ShapeDtypeStruct((nb * t, LANES), F32),
                   jax.ShapeDtypeStruct((nb, t // lc, 2 * GATE_ROWS, lc), F32),
                   jax.ShapeDtypeStruct((nb, hf, t), F32)],
        scratch_shapes=[pltpu.VMEM((GATE_ROWS, 1), F32)],
        name="gates",
        compiler_params=_params("arbitrary", "arbitrary"),
    )(proj, a_col, dt_col)


def _mlstm_kernel(q_ref, k_ref, v_ref, o_ref, gc_ref, gr_ref, ng_ref, c0_ref, n0_ref, m0_ref,
                  h_ref, c_ref, n_ref, m_ref, *, hm, lc, bpb, mxu_dtype):
    @pl.when(pl.program_id(1) == 0)
    def _():
        c_ref[...] = c0_ref[...]
        n_ref[...] = n0_ref[...]
        m_ref[...] = m0_ref[...]

    r_i = lax.broadcasted_iota(jnp.int32, (lc, lc), 0)
    c_i = lax.broadcasted_iota(jnp.int32, (lc, lc), 1)
    causal = c_i <= r_i
    scale = HEAD_DIM ** -0.5
    for b, h in [(b, h) for b in range(bpb) for h in range(hm)]:
        gc = gc_ref[b]
        gr = gr_ref[b]
        sl = slice(h * HEAD_DIM, (h + 1) * HEAD_DIM)
        q = q_ref[b, :, sl]
        k = k_ref[b, :, sl] * scale
        v = v_ref[b, :, sl]
        li_col = gc[:, h:h + 1]
        b_col = gc[:, hm + h:hm + h + 1]
        li_row = gr[h:h + 1, :]
        b_row = gr[hm + h:hm + h + 1, :]
        cst = c_ref[b, h]
        nst = n_ref[b, h]
        mst = m_ref[b, h]
        dmat = jnp.where(causal, b_col - b_row + li_row, NEG_INF)
        inter = b_col + mst
        m_row = jnp.maximum(inter, jnp.max(dmat, axis=-1, keepdims=True))
        w_intra = jnp.exp(dmat - m_row)
        w_inter = jnp.exp(inter - m_row)
        qx = q.astype(mxu_dtype)
        kx = k.astype(mxu_dtype)
        vx = v.astype(mxu_dtype)
        s = _dot_nt(qx, kx) * w_intra
        num = _dot(s.astype(mxu_dtype), vx) + w_inter * _dot(qx, cst.astype(mxu_dtype))
        den = jnp.sum(s, axis=-1, keepdims=True) + w_inter * jnp.sum(q * nst, axis=-1, keepdims=True)
        hh = num / jnp.maximum(jnp.abs(den), jnp.exp(-m_row))
        b_last = b_col[lc - 1:lc, :]
        d_last = b_last - b_col + li_col
        m_new = jnp.maximum(b_last + mst, jnp.max(d_last, axis=0, keepdims=True))
        w_k = jnp.exp(d_last - m_new)
        decay = jnp.exp(b_last + mst - m_new)
        kw = k * w_k
        c_ref[b, h] = decay * cst + _dot_tn(kw.astype(mxu_dtype), vx)
        n_ref[b, h] = decay * nst + jnp.sum(kw, axis=0, keepdims=True)
        m_ref[b, h] = m_new
        mu = jnp.mean(hh, axis=-1, keepdims=True)
        hc = hh - mu
        var = jnp.mean(hc * hc, axis=-1, keepdims=True)
        hn = hc * lax.rsqrt(var + LN_EPS) * ng_ref[h]
        h_ref[b, :, sl] = (hn * _sigmoid(o_ref[b, :, sl])).astype(h_ref.dtype)


def _mlstm(proj, gc, grc, norm_g, c0, n0, m0, layer, nb, t, bpb, dims):
    hm = dims[0]
    dm = hm * HEAD_DIM
    lc = math.gcd(t, CHUNK)
    nc = t // lc
    mxu_dtype = BF16 if lc >= 16 else F32

    def col(j):
        return pl.BlockSpec((bpb, lc, dm), lambda b, c: (b, c, j))

    st4 = pl.BlockSpec((bpb, hm, HEAD_DIM, HEAD_DIM), lambda b, c: (b, 0, 0, 0))
    stn = pl.BlockSpec((bpb, hm, 1, HEAD_DIM), lambda b, c: (b, 0, 0, 0))
    stm = pl.BlockSpec((bpb, hm, 1, 1), lambda b, c: (b, 0, 0, 0))
    return pl.pallas_call(
        functools.partial(_mlstm_kernel, hm=hm, lc=lc, bpb=bpb, mxu_dtype=mxu_dtype),
        grid=(nb // bpb, nc),
        in_specs=[col(0), col(1), col(2), col(3),
                  pl.BlockSpec((bpb, lc, LANES), lambda b, c: (b, c, 0)),
                  pl.BlockSpec((bpb, None, 2 * GATE_ROWS, lc), lambda b, c: (b, c, 0, 0)),
                  pl.BlockSpec((None, hm, 1, HEAD_DIM), lambda b, c: (layer, 0, 0, 0)),
                  st4, stn, stm],
        out_specs=[pl.BlockSpec((bpb, lc, dm), lambda b, c: (b, c, 0)), st4, stn, stm],
        out_shape=[jax.ShapeDtypeStruct((nb, t, dm), mxu_dtype),
                   jax.ShapeDtypeStruct((nb, hm, HEAD_DIM, HEAD_DIM), F32),
                   jax.ShapeDtypeStruct((nb, hm, 1, HEAD_DIM), F32),
                   jax.ShapeDtypeStruct((nb, hm, 1, 1), F32)],
        name="mlstm",
        compiler_params=_params("arbitrary", "arbitrary"),
    )(proj, proj, proj, proj, gc, grc, norm_g, c0, n0, m0)


def _split(a):
    hi = a.astype(BF16)
    return hi, (a - hi.astype(F32)).astype(BF16)


def _mm3(a, b):
    return _dot(a[0], b[0]) + (_dot(a[0], b[1]) + _dot(a[1], b[0]))


def _mm3_nt(a, b):
    return _dot_nt(a[0], b[0]) + (_dot_nt(a[0], b[1]) + _dot_nt(a[1], b[0]))


def _unit_lower_solve(a, rhs, lc, split_bf16):
    r_i = lax.broadcasted_iota(jnp.int32, (lc, lc), 0)
    c_i = lax.broadcasted_iota(jnp.int32, (lc, lc), 1)
    eye = jnp.where(r_i == c_i, 1.0, 0.0).astype(F32)
    if split_bf16:
        prep, mm = _split, _mm3
    else:
        prep, mm = (lambda x: x), (lambda x, y: _dot(x, y, HI))
    npow = -a
    inv = eye + npow
    width = 2
    while width < lc:
        ns = prep(npow)
        npow = mm(ns, ns)
        inv = inv + mm(prep(inv), prep(npow))
        width *= 2
    return mm(prep(inv), prep(rhs))


def _gdn_prep_kernel(q_ref, k_ref, v_ref, pq_ref, pk_ref, pv_ref, b0_ref, gc_ref, gr_ref, cw_ref,
                     w_ref, u_ref, qg_ref, kd_ref, qk_ref, buf_ref, xp_ref, *, hm, hf, hg, lc, cb):
    dg = hg * HEAD_DIM
    keep = SUBLANES
    tb = cb * lc
    split_bf16 = lc >= 16
    prev = jnp.concatenate([pq_ref[keep - 3:keep, :], pk_ref[keep - 3:keep, :], pv_ref[keep - 3:keep, :]], axis=1)
    xp_ref[keep - 3:keep, :] = jnp.where(pl.program_id(1) == 0, b0_ref[...], prev)
    xp_ref[keep:keep + tb, 0:dg] = q_ref[...]
    xp_ref[keep:keep + tb, dg:2 * dg] = k_ref[...]
    xp_ref[keep:keep + tb, 2 * dg:3 * dg] = v_ref[...]
    conv = xp_ref[keep:keep + tb, :] * cw_ref[3:4, :]
    for j in range(3):
        conv = conv + xp_ref[keep - 3 + j:keep - 3 + j + tb, :] * cw_ref[j:j + 1, :]
    buf_ref[...] = xp_ref[keep + tb - 3:keep + tb, :]
    act = _silu(conv)

    r_i = lax.broadcasted_iota(jnp.int32, (lc, lc), 0)
    c_i = lax.broadcasted_iota(jnp.int32, (lc, lc), 1)
    incl = c_i <= r_i
    strict = c_i < r_i
    ch_g = 2 * hm + hf
    ch_b = ch_g + hg
    scale = HEAD_DIM ** -0.5
    qks = []
    for c, h in [(c, h) for c in range(cb) for h in range(hg)]:
        rows = slice(c * lc, (c + 1) * lc)
        sl = slice(h * HEAD_DIM, (h + 1) * HEAD_DIM)
        gc = gc_ref[rows, :]
        gr = gr_ref[c]
        cq = act[rows, h * HEAD_DIM:(h + 1) * HEAD_DIM]
        ck = act[rows, dg + h * HEAD_DIM:dg + (h + 1) * HEAD_DIM]
        cv = act[rows, 2 * dg + h * HEAD_DIM:2 * dg + (h + 1) * HEAD_DIM]
        qn = cq * lax.rsqrt(jnp.sum(cq * cq, axis=-1, keepdims=True) + NORM_EPS) * scale
        kn = ck * lax.rsqrt(jnp.sum(ck * ck, axis=-1, keepdims=True) + NORM_EPS)
        g_col = gc[:, ch_g + h:ch_g + h + 1]
        g_row = gr[ch_g + h:ch_g + h + 1, :]
        beta = gc[:, ch_b + h:ch_b + h + 1]
        decay = jnp.exp(jnp.where(incl, g_col - g_row, NEG_INF))
        if split_bf16:
            ks = _split(kn)
            kk = _mm3_nt(ks, ks)
            qk = _dot_nt(qn.astype(BF16), ks[0])
        else:
            kk = _dot_nt(kn, kn, HI)
            qk = _dot_nt(qn, kn)
        amat = jnp.where(strict, beta * kk * decay, 0.0)
        eg = jnp.exp(g_col)
        rhs = jnp.concatenate([kn * (beta * eg), cv * beta], axis=-1)
        sol = _unit_lower_solve(amat, rhs, lc, split_bf16)
        g_last = g_col[lc - 1:lc, :]
        w_ref[rows, sl] = sol[:, :HEAD_DIM].astype(w_ref.dtype)
        u_ref[rows, sl] = sol[:, HEAD_DIM:]
        qg_ref[rows, sl] = (qn * eg).astype(qg_ref.dtype)
        kd_ref[rows, sl] = (kn * jnp.exp(g_last - g_col)).astype(kd_ref.dtype)
        qks.append((qk * decay).astype(qk_ref.dtype))
        if h == hg - 1:
            qk_ref[rows, :] = jnp.concatenate(qks[-hg:], axis=1)


def _gdn_scan_kernel(w_ref, u_ref, qg_ref, kd_ref, qk_ref, z_ref, gc_ref, ng_ref, s0_ref,
                     h_ref, s_ref, *, hm, hf, hg, lc, bpb):
    @pl.when(pl.program_id(1) == 0)
    def _():
        s_ref[...] = s0_ref[...]

    ch_g = 2 * hm + hf
    xd = w_ref.dtype
    for b, h in [(b, h) for b in range(bpb) for h in range(hg)]:
        sl = slice(h * HEAD_DIM, (h + 1) * HEAD_DIM)
        sst = s_ref[b, h]
        sx = sst.astype(xd)
        u = u_ref[b, :, sl] - _dot(w_ref[b, :, sl], sx)
        ux = u.astype(xd)
        o = _dot(qg_ref[b, :, sl], sx) + _dot(qk_ref[b][:, h * lc:(h + 1) * lc], ux)
        g_last = gc_ref[b, lc - SUBLANES:lc, :][SUBLANES - 1:SUBLANES, ch_g + h:ch_g + h + 1]
        s_ref[b, h] = jnp.exp(g_last) * sst + _dot_tn(kd_ref[b, :, sl], ux)
        on = o * lax.rsqrt(jnp.mean(o * o, axis=-1, keepdims=True) + NORM_EPS) * ng_ref[...]
        h_ref[b, :, sl] = (on * _silu(z_ref[b, :, sl])).astype(h_ref.dtype)


def _gdn(proj, gc, grc, conv_w, norm_g, s0, b0, layer, nb, t, bpb, dims):
    hm, hf, hg = dims
    dg = hg * HEAD_DIM
    lc = math.gcd(t, CHUNK)
    nc = t // lc
    xd = BF16 if lc >= 16 else F32
    base = (4 * hm * HEAD_DIM + 3 * hf * HEAD_DIM) // dg
    cb = _pick(nc, (2, 1))
    tb = cb * lc
    nblk = t // tb
    m = nb * t
    n_all = proj.shape[-1]
    proj2 = proj.reshape(m, n_all)

    def col(j):
        return pl.BlockSpec((tb, dg), lambda b, i: (b * nblk + i, base + j))

    def prev(j):
        return pl.BlockSpec((SUBLANES, dg),
                            lambda b, i: (jnp.maximum((b * t + i * tb) // SUBLANES - 1, 0), base + j))

    def rowblk(width):
        return pl.BlockSpec((tb, width), lambda b, i: (b * nblk + i, 0))

    w, u0, qg, kd, qk, b1 = pl.pallas_call(
        functools.partial(_gdn_prep_kernel, hm=hm, hf=hf, hg=hg, lc=lc, cb=cb),
        grid=(nb, nblk),
        in_specs=[col(0), col(1), col(2), prev(0), prev(1), prev(2),
                  pl.BlockSpec((None, 3, 3 * dg), lambda b, i: (b, 0, 0)),
                  rowblk(LANES),
                  pl.BlockSpec((None, cb, 2 * GATE_ROWS, lc), lambda b, i: (b, i, 0, 0)),
                  pl.BlockSpec((None, 4, 3 * dg), lambda b, i: (layer, 0, 0))],
        out_specs=[rowblk(dg), rowblk(dg), rowblk(dg), rowblk(dg), rowblk(hg * lc),
                   pl.BlockSpec((None, 3, 3 * dg), lambda b, i: (b, 0, 0))],
        out_shape=[jax.ShapeDtypeStruct((m, dg), xd), jax.ShapeDtypeStruct((m, dg), F32),
                   jax.ShapeDtypeStruct((m, dg), xd), jax.ShapeDtypeStruct((m, dg), xd),
                   jax.ShapeDtypeStruct((m, hg * lc), xd), jax.ShapeDtypeStruct((nb, 3, 3 * dg), F32)],
        scratch_shapes=[pltpu.VMEM((tb + SUBLANES, 3 * dg), F32)],
        name="gdn_prep",
        compiler_params=_params("arbitrary", "arbitrary"),
    )(proj2, proj2, proj2, proj2, proj2, proj2, b0, gc.reshape(m, LANES), grc, conv_w)

    def seq(width):
        return pl.BlockSpec((bpb, lc, width), lambda b, c: (b, c, 0))

    st4 = pl.BlockSpec((bpb, hg, HEAD_DIM, HEAD_DIM), lambda b, c: (b, 0, 0, 0))
    hgx, s1 = pl.pallas_call(
        functools.partial(_gdn_scan_kernel, hm=hm, hf=hf, hg=hg, lc=lc, bpb=bpb),
        grid=(nb // bpb, nc),
        in_specs=[seq(dg), seq(dg), seq(dg), seq(dg), seq(hg * lc),
                  pl.BlockSpec((bpb, lc, dg), lambda b, c: (b, c, base + 3)),
                  seq(LANES),
                  pl.BlockSpec((None, 1, HEAD_DIM), lambda b, c: (layer, 0, 0)),
                  st4],
        out_specs=[seq(dg), st4],
        out_shape=[jax.ShapeDtypeStruct((nb, t, dg), xd),
                   jax.ShapeDtypeStruct((nb, hg, HEAD_DIM, HEAD_DIM), F32)],
        name="gdn_scan",
        compiler_params=_params("arbitrary", "arbitrary"),
    )(w.reshape(nb, t, dg), u0.reshape(nb, t, dg), qg.reshape(nb, t, dg), kd.reshape(nb, t, dg),
      qk.reshape(nb, t, hg * lc), proj, gc, norm_g, s0)
    return hgx, s1, b1


def _fox_prompt_kernel(q_ref, k_ref, v_ref, gr_ref, o_ref, q_s, m_s, l_s, acc_s, *, hf, tb):
    qi = pl.program_id(1)
    ki = pl.program_id(2)
    rep = tb // LANES

    @pl.when(ki == 0)
    def _():
        q_s[...] = (q_ref[...] * HEAD_DIM ** -0.5).astype(BF16)
        m_s[...] = jnp.full(m_s.shape, NEG_INF, F32)
        l_s[...] = jnp.zeros(l_s.shape, F32)
        acc_s[...] = jnp.zeros(acc_s.shape, F32)

    def block(masked):
        if masked:
            visible = (lax.broadcasted_iota(jnp.int32, (tb, tb), 1) <= lax.broadcasted_iota(jnp.int32, (tb, tb), 0))
        for h in range(hf):
            sl = slice(h * HEAD_DIM, (h + 1) * HEAD_DIM)
            k = k_ref[:, sl].astype(BF16)
            v = v_ref[:, sl].astype(BF16)
            s = _dot_nt(q_s[:, sl], k) - gr_ref[h:h + 1, :]
            if masked:
                s = jnp.where(visible, s, NEG_INF)
            m_prev = m_s[h]
            m_new = jnp.maximum(m_prev, jnp.max(s, axis=-1, keepdims=True))
            alpha = jnp.exp(m_prev - m_new)
            p = jnp.exp(s - pltpu.repeat(m_new, rep, axis=1))
            l_s[h] = alpha * l_s[h] + jnp.sum(p, axis=-1, keepdims=True)
            acc_s[:, sl] = alpha * acc_s[:, sl] + _dot(p.astype(BF16), v)
            m_s[h] = m_new

    @pl.when(ki < qi)
    def _():
        block(False)

    @pl.when(ki == qi)
    def _():
        block(True)
        for h in range(hf):
            sl = slice(h * HEAD_DIM, (h + 1) * HEAD_DIM)
            o_ref[:, sl] = (acc_s[:, sl] / l_s[h]).astype(BF16)


def _fox_prompt(proj, grf, nb, t, dims):
    hm, hf, hg = dims
    df = hf * HEAD_DIM
    base = (4 * hm * HEAD_DIM) // df
    tb = _pick(t, (512, 256, 128))
    nq = t // tb
    return pl.pallas_call(
        functools.partial(_fox_prompt_kernel, hf=hf, tb=tb),
        grid=(nb, nq, nq),
        in_specs=[pl.BlockSpec((tb, df), lambda b, i, j: (b * nq + i, base)),
                  pl.BlockSpec((tb, df), lambda b, i, j: (b * nq + jnp.minimum(j, i), base + 1)),
                  pl.BlockSpec((tb, df), lambda b, i, j: (b * nq + jnp.minimum(j, i), base + 2)),
                  pl.BlockSpec((None, hf, tb), lambda b, i, j: (b, 0, jnp.minimum(j, i)))],
        out_specs=pl.BlockSpec((tb, df), lambda b, i, j: (b * nq + i, 0)),
        out_shape=jax.ShapeDtypeStruct((nb * t, df), BF16),
        scratch_shapes=[pltpu.VMEM((tb, df), BF16), pltpu.VMEM((hf, tb, LANES), F32),
                        pltpu.VMEM((hf, tb, LANES), F32), pltpu.VMEM((tb, df), F32)],
        name="fox_prompt",
        compiler_params=_params("arbitrary", "arbitrary", "arbitrary"),
    )(proj, proj, proj, grf)


def _fox_pool_kernel(x_ref, w_ref, t_ref, *, hf, rpp, rb):
    x = x_ref[...]
    li = lax.broadcasted_iota(jnp.int32, (LANES, LANES), 0)
    lj = lax.broadcasted_iota(jnp.int32, (LANES, LANES), 1)
    same_head = (li % hf) == (lj % hf)
    m_row = jnp.where(same_head & (li > lj), 1.0, 0.0).astype(F32)
    h_row = jnp.where(same_head, 1.0, 0.0).astype(F32)
    ri = lax.broadcasted_iota(jnp.int32, (rb, rb), 0)
    rj = lax.broadcasted_iota(jnp.int32, (rb, rb), 1)
    same_page = (ri // rpp) == (rj // rpp)
    s_below = jnp.where(same_page & (rj > ri), 1.0, 0.0).astype(F32)
    s_all = jnp.where(same_page, 1.0, 0.0).astype(F32)
    row_tot = _dot(x, h_row, HI)
    w_ref[...] = _dot(x, m_row, HI) + _dot(s_below, row_tot, HI)
    t_ref[...] = _dot(s_all, row_tot, HI)


def _fox_pool(logf2d, hf, rpp):
    r = logf2d.shape[0]
    rb = _pick(r, (512, 256, 128, 64, 32, 16, 8))
    spec = pl.BlockSpec((rb, LANES), lambda i: (i, 0))
    return pl.pallas_call(
        functools.partial(_fox_pool_kernel, hf=hf, rpp=rpp, rb=rb),
        grid=(r // rb,),
        in_specs=[spec],
        out_specs=[spec, spec],
        out_shape=[jax.ShapeDtypeStruct((r, LANES), F32)] * 2,
        name="fox_pool",
        compiler_params=_params("arbitrary"),
    )(logf2d)


def _fox_sample_kernel(pt_ref, q_ref, cr_ref, kn_ref, vn_ref, *rest, hf, nsteps, rpp, npp):
    kp = rest[0:npp]
    vp = rest[npp:2 * npp]
    wp = rest[2 * npp:3 * npp]
    tp = rest[3 * npp:4 * npp]
    o_ref, m_s, l_s, acc_s, carry_s = rest[4 * npp:]
    p = pl.program_id(1)
    q = (q_ref[...] * HEAD_DIM ** -0.5).astype(BF16)
    nr = q.shape[0]

    @pl.when(p == 0)
    def _():
        ri = lax.broadcasted_iota(jnp.int32, (nr, nr), 0)
        ci = lax.broadcasted_iota(jnp.int32, (nr, nr), 1)
        ok = ((ri % hf) == (ci % hf)) & ((ci // hf) <= (ri // hf))
        s = jnp.where(ok, _dot_nt(q, kn_ref[...].astype(BF16)) - cr_ref[...], NEG_INF)
        m0 = jnp.max(s, axis=-1, keepdims=True)
        e = jnp.exp(s - m0)
        m_s[...] = m0
        l_s[...] = jnp.sum(e, axis=-1, keepdims=True)
        acc_s[...] = _dot(e.astype(BF16), vn_ref[...].astype(BF16))
        carry_s[...] = jnp.zeros_like(carry_s)

    ncol = rpp * LANES
    ri = lax.broadcasted_iota(jnp.int32, (nr, ncol), 0)
    ci = lax.broadcasted_iota(jnp.int32, (nr, ncol), 1)
    head_mask = jnp.where((ri % hf) == (ci % hf), 0.0, NEG_INF).astype(F32)
    carry = carry_s[...]
    scores = []
    for g in range(npp):
        bias_row = jnp.concatenate([wp[g][r:r + 1, :] + carry for r in range(rpp)], axis=1)
        scores.append(_dot_nt(q, kp[g][...].astype(BF16)) + (head_mask + bias_row))
        carry = carry + tp[g][0:1, :]
    carry_s[...] = carry
    top = scores[0]
    for g in range(1, npp):
        top = jnp.maximum(top, scores[g])
    m_prev = m_s[...]
    m_new = jnp.maximum(m_prev, jnp.max(top, axis=-1, keepdims=True))
    alpha = jnp.exp(m_prev - m_new)
    acc = alpha * acc_s[...]
    tot = None
    for g in range(npp):
        e = jnp.exp(scores[g] - m_new)
        tot = e if tot is None else tot + e
        acc = acc + _dot(e.astype(BF16), vp[g][...].astype(BF16))
    l_s[...] = alpha * l_s[...] + jnp.sum(tot, axis=-1, keepdims=True)
    acc_s[...] = acc
    m_s[...] = m_new

    @pl.when(p == nsteps - 1)
    def _():
        o_ref[...] = (acc / l_s[...]).astype(BF16)


def _fox_sample(page_table, q2, cum_row, kn2, vn2, kpool, vpool, wpool, tpool, layer, hf):
    nb, npg = page_table.shape
    nr = q2.shape[1]
    nphys, pcols = kpool.shape[1], kpool.shape[2]
    rpp = pcols // LANES
    npp = _pick(npg, (8, 4, 2, 1))
    nsteps = npg // npp

    def page(b, p, pt, g):
        return pt[b * npg + (npg - 1 - (p * npp + g))]

    per_b3 = lambda b, p, pt: (b, 0, 0)
    kv_specs = [pl.BlockSpec((None, None, pcols, HEAD_DIM),
                             lambda b, p, pt, g=g: (layer, page(b, p, pt, g), 0, 0)) for g in range(npp)]
    wt_specs = [pl.BlockSpec((None, rpp, LANES),
                             lambda b, p, pt, g=g: (layer * nphys + page(b, p, pt, g), 0, 0)) for g in range(npp)]
    return pl.pallas_call(
        functools.partial(_fox_sample_kernel, hf=hf, nsteps=nsteps, rpp=rpp, npp=npp),
        grid_spec=pltpu.PrefetchScalarGridSpec(
            num_scalar_prefetch=1,
            grid=(nb, nsteps),
            in_specs=[pl.BlockSpec((None, nr, HEAD_DIM), per_b3),
                      pl.BlockSpec((None, 1, nr), per_b3),
                      pl.BlockSpec((None, nr, HEAD_DIM), per_b3),
                      pl.BlockSpec((None, nr, HEAD_DIM), per_b3)] + kv_specs + kv_specs + wt_specs + wt_specs,
            out_specs=pl.BlockSpec((None, nr, HEAD_DIM), per_b3),
            scratch_shapes=[pltpu.VMEM((nr, 1), F32), pltpu.VMEM((nr, 1), F32),
                            pltpu.VMEM((nr, HEAD_DIM), F32), pltpu.VMEM((1, LANES), F32)]),
        out_shape=jax.ShapeDtypeStruct((nb, nr, HEAD_DIM), BF16),
        name="fox_decode",
        compiler_params=_params("arbitrary", "arbitrary"),
    )(page_table.reshape(-1), q2, cum_row, kn2, vn2, *([kpool] * npp), *([vpool] * npp),
      *([wpool] * npp), *([tpool] * npp))


def _regroup_w_in(w_in, b_in, dims):
    hm, hf, hg = dims
    dm, df, dg = hm * HEAD_DIM, hf * HEAD_DIM, hg * HEAD_DIM
    sizes = (dm, dm, dm, dm, hm, hm, df, df, df, hf, dg, dg, dg, dg, hg, hg)
    offs = [0]
    for s in sizes:
        offs.append(offs[-1] + s)
    order = (0, 1, 2, 3, 6, 7, 8, 10, 11, 12, 13, 4, 5, 9, 14, 15)
    n_main = 4 * dm + 3 * df + 4 * dg
    n_gate = 2 * hm + hf + 2 * hg
    n_tot = -(-(n_main + LANES) // 512) * 512
    pad = n_tot - n_main - n_gate

    def regroup(a):
        parts = [a[..., offs[i]:offs[i + 1]] for i in order]
        parts.append(jnp.zeros(a.shape[:-1] + (pad,), a.dtype))
        return jnp.concatenate(parts, axis=-1)

    return regroup(w_in).astype(BF16), regroup(b_in)[:, None, :], n_main // LANES


def _gate_param_cols(a_log, dt_bias, dims):
    hm, hf, hg = dims
    depth = a_log.shape[0]
    lead = 2 * hm + hf
    z0 = jnp.zeros((depth, lead), F32)
    z1 = jnp.zeros((depth, GATE_ROWS - lead - hg), F32)
    a_col = jnp.concatenate([z0, a_log, z1], axis=1)[:, :, None]
    dt_col = jnp.concatenate([z0, dt_bias, z1], axis=1)[:, :, None]
    return a_col, dt_col


def kernel(x_prompt, x_sample, cache_fox_k, cache_fox_v, cache_fox_logf, state_mlstm_C, state_mlstm_n, state_mlstm_m, state_gdn_S, state_gdn_conv, page_table, c_prompt, c_sample, w_ada, b_ada, ln_g, ln_b, ffn_w_gate, ffn_w_up, ffn_w_down, w_in, b_in, mlstm_norm_g, gdn_conv_w, gdn_A_log, gdn_dt_bias, gdn_norm_g, w_out):
    bp, tp, d = x_prompt.shape
    bs, ts, _ = x_sample.shape
    depth = w_ada.shape[0]
    hm, hf, hg = state_mlstm_C.shape[2], cache_fox_k.shape[3], state_gdn_S.shape[2]
    dims = (hm, hf, hg)
    dm, df, dg = hm * HEAD_DIM, hf * HEAD_DIM, hg * HEAD_DIM
    nphys, page = cache_fox_k.shape[1], cache_fox_k.shape[2]
    alpha = (2 * depth) ** 0.25
    assert 2 * hm + hf + 2 * hg <= GATE_ROWS and (page * hf) % LANES == 0 and LANES % hf == 0

    nrow = -(-(bp + bs) // 16) * 16
    c_all = jnp.concatenate([c_prompt, c_sample, jnp.zeros((nrow - bp - bs, d), F32)], axis=0)
    mod = _ada(c_all, w_ada, b_ada).reshape(depth, nrow, 3 * N_SUB, d)
    mod = jnp.transpose(mod, (0, 2, 1, 3))
    mod_p = mod[:, :, :bp, None, :]
    mod_s = jnp.repeat(mod[:, :, bp:bp + bs], ts, axis=2)[:, :, None, :, :]

    w_in_r, b_in_r, gate_blk = _regroup_w_in(w_in, b_in, dims)
    a_col, dt_col = _gate_param_cols(gdn_A_log, gdn_dt_bias, dims)
    ln_g4 = ln_g[:, :, None, :]
    ln_b4 = ln_b[:, :, None, :]
    norm_m = mlstm_norm_g[:, :, None, :]
    norm_g = gdn_norm_g[:, None, :]

    rpp = (page * hf) // LANES
    wpool, tpool = _fox_pool(cache_fox_logf.reshape(depth * nphys * rpp, LANES), hf, rpp)
    wpool = wpool.reshape(depth * nphys, rpp, LANES)
    tpool = tpool.reshape(depth * nphys, rpp, LANES)
    kpool = cache_fox_k.reshape(depth, nphys, page * hf, HEAD_DIM)
    vpool = cache_fox_v.reshape(depth, nphys, page * hf, HEAD_DIM)

    o_f = 4 * dm
    o_g = o_f + 3 * df

    def layer_step(l, x, h, mod_x, nb, t, mlstm_state, gdn_state, sample):
        m_rows = nb * t
        grp = m_rows if sample else t
        a = _gateup(h, ffn_w_gate, ffn_w_up, l, 0)
        x, h = _proj_ln(a, ffn_w_down, (l, 0), x, mod_x, l, 0, (l, 1), ln_g4, ln_b4, grp, alpha, MACARON_W)
        proj = _win(h, w_in_r, b_in_r, l)
        lc = math.gcd(t, CHUNK)
        if sample:
            gc, grc, grf = _gates(proj, gate_blk, a_col, dt_col, l, 1, m_rows, lc, t, dims)
            grc = grc.reshape(nb, t // lc, 2 * GATE_ROWS, lc)
        else:
            gc, grc, grf = _gates(proj, gate_blk, a_col, dt_col, l, nb, t, lc, t, dims)
        c0, n0, m0 = mlstm_state
        proj3 = proj.reshape(nb, t, proj.shape[-1])
        gc3 = gc.reshape(nb, t, LANES)
        bpb = 1 if sample else nb
        hmx, c1, n1, m1 = _mlstm(proj3, gc3, grc, norm_m, c0, n0[:, :, None, :], m0[:, :, None, None],
                                 l, nb, t, bpb, dims)
        s0, b0 = gdn_state
        hgx, s1, b1 = _gdn(proj3, gc3, grc, gdn_conv_w, norm_g, s0, b0, l, nb, t, bpb, dims)
        hmx = hmx.reshape(m_rows, dm)
        hgx = hgx.reshape(m_rows, dg)
        fk = proj[:, o_f + df:o_f + 2 * df]
        fv = proj[:, o_f + 2 * df:o_f + 3 * df]
        logf = gc[:, GATE_ROWS + 2 * hm:GATE_ROWS + 2 * hm + hf]
        if sample:
            nr = t * hf
            q2 = proj[:, o_f:o_f + df].reshape(nb, nr, HEAD_DIM)
            cum = gc[:, 2 * hm:2 * hm + hf].reshape(nb, nr)
            hfx = _fox_sample(page_table, q2, cum[:, None, :], fk.reshape(nb, nr, HEAD_DIM),
                              fv.reshape(nb, nr, HEAD_DIM), kpool, vpool, wpool, tpool, l, hf)
            hfx = hfx.reshape(m_rows, df)
        else:
            hfx = _fox_prompt(proj, grf, nb, t, dims)
        mixed = jnp.concatenate([hmx.astype(BF16), hfx, hgx.astype(BF16)], axis=-1)
        x, h = _proj_ln(mixed, w_out, (l,), x, mod_x, l, 1, (l, 2), ln_g4, ln_b4, grp, alpha, 1.0)
        a = _gateup(h, ffn_w_gate, ffn_w_up, l, 1)
        nxt = (l + 1, 0) if l + 1 < depth else None
        x, h = _proj_ln(a, ffn_w_down, (l, 1), x, mod_x, l, 2, nxt, ln_g4, ln_b4, grp, alpha, MACARON_W)
        new_state = (fk.reshape(nb, t, hf, HEAD_DIM), fv.reshape(nb, t, hf, HEAD_DIM), logf.reshape(nb, t, hf),
                     c1, n1[:, :, 0, :], m1[:, :, 0, 0], s1, b1)
        return x, h, new_state

    xp = x_prompt.reshape(bp * tp, d)
    xs = x_sample.reshape(bs * ts, d)
    hp = _modulate(xp, mod_p, 0, 0, tp)
    hs = _modulate(xs, mod_s, 0, 0, bs * ts)
    zero_mlstm = (jnp.zeros((bp, hm, HEAD_DIM, HEAD_DIM), F32), jnp.zeros((bp, hm, HEAD_DIM), F32),
                  jnp.zeros((bp, hm), F32))
    zero_gdn = (jnp.zeros((bp, hg, HEAD_DIM, HEAD_DIM), F32), jnp.zeros((bp, 3, 3 * dg), F32))
    st_p, st_s = [], []
    for l in range(depth):
        xp, hp, sp = layer_step(l, xp, hp, mod_p, bp, tp, zero_mlstm, zero_gdn, False)
        xs, hs, ss = layer_step(l, xs, hs, mod_s, bs, ts,
                                (state_mlstm_C[l], state_mlstm_n[l], state_mlstm_m[l]),
                                (state_gdn_S[l], state_gdn_conv[l]), True)
        st_p.append(sp)
        st_s.append(ss)
    pn = [jnp.stack(a) for a in zip(*st_p)]
    sn = [jnp.stack(a) for a in zip(*st_s)]
    return (xp.reshape(bp, tp, d), xs.reshape(bs, ts, d), pn[0], pn[1], pn[2], pn[3], pn[4], pn[5], pn[6], pn[7],
            sn[0], sn[1], sn[2], sn[3], sn[4], sn[5], sn[6], sn[7])
```

```python
import functools
import math

import jax
import jax.numpy as jnp
from jax import lax
from jax.experimental import pallas as pl
from jax.experimental.pallas import tpu as pltpu

F32 = jnp.float32
BF16 = jnp.bfloat16

HEAD_DIM = 128
LANES = 128
SUBLANES = 8
CHUNK = 64
N_SUB = 3
LN_EPS = 1e-5
NORM_EPS = 1e-6
MACARON_W = 0.5
GATE_ROWS = 32
VMEM_LIMIT_BYTES = 56 * 1024 * 1024
HI = lax.Precision.HIGHEST
NEG_INF = float("-inf")


def _params(*sem):
    return pltpu.CompilerParams(dimension_semantics=sem, vmem_limit_bytes=VMEM_LIMIT_BYTES)


def _dot(a, b, precision=None):
    return jnp.dot(a, b, preferred_element_type=F32, precision=precision)


def _dot_nt(a, b, precision=None):
    return lax.dot_general(a, b, (((1,), (1,)), ((), ())), preferred_element_type=F32, precision=precision)


def _dot_tn(a, b, precision=None):
    return lax.dot_general(a, b, (((0,), (0,)), ((), ())), preferred_element_type=F32, precision=precision)


def _sigmoid(x):
    return 1.0 / (1.0 + jnp.exp(-x))


def _silu(x):
    return x * _sigmoid(x)


def _softplus(x):
    return jnp.maximum(x, 0.0) + jnp.log1p(jnp.exp(-jnp.abs(x)))


def _log_sigmoid(x):
    return -_softplus(-x)


def _pick(m, candidates):
    for c in candidates:
        if m % c == 0:
            return c
    return m


def _ada_kernel(c_ref, w_ref, b_ref, o_ref):
    c = c_ref[...]
    sc = _silu(c).astype(BF16)
    o_ref[...] = _dot(sc, w_ref[...].astype(BF16)) + b_ref[...]


def _ada(c_all, w_ada, b_ada):
    depth, d, n = w_ada.shape
    rows = c_all.shape[0]
    bn = _pick(n, (1024, 512, 256, 128))
    return pl.pallas_call(
        _ada_kernel,
        grid=(depth, n // bn),
        in_specs=[pl.BlockSpec((rows, d), lambda l, j: (0, 0)),
                  pl.BlockSpec((None, d, bn), lambda l, j: (l, 0, j)),
                  pl.BlockSpec((None, 1, bn), lambda l, j: (l, 0, j))],
        out_specs=pl.BlockSpec((None, rows, bn), lambda l, j: (l, 0, j)),
        out_shape=jax.ShapeDtypeStruct((depth, rows, n), F32),
        name="ada",
        compiler_params=_params("arbitrary", "arbitrary"),
    )(c_all, w_ada, b_ada.reshape(depth, 1, n))


def _modulate_kernel(x_ref, sc_ref, sh_ref, o_ref):
    o_ref[...] = (x_ref[...] * (1.0 + sc_ref[...]) + sh_ref[...]).astype(BF16)


def _mod_spec(mod, layer, slot, rows_per_group, bm):
    r, d = mod.shape[3], mod.shape[4]
    if r == 1:
        return pl.BlockSpec((None, None, None, 1, d),
                            lambda m, *_: (layer, slot, (m * bm) // rows_per_group, 0, 0))
    return pl.BlockSpec((None, None, None, bm, d), lambda m, *_: (layer, slot, 0, m, 0))


def _modulate(x, mod, layer, j, rows_per_group):
    m, d = x.shape
    bm = _pick(rows_per_group, (512, 256, 128, 64))
    return pl.pallas_call(
        _modulate_kernel,
        grid=(m // bm,),
        in_specs=[pl.BlockSpec((bm, d), lambda i: (i, 0)),
                  _mod_spec(mod, layer, 3 * j + 1, rows_per_group, bm),
                  _mod_spec(mod, layer, 3 * j + 0, rows_per_group, bm)],
        out_specs=pl.BlockSpec((bm, d), lambda i: (i, 0)),
        out_shape=jax.ShapeDtypeStruct((m, d), BF16),
        name="modulate",
        compiler_params=_params("arbitrary"),
    )(x, mod, mod)


def _gateup_kernel(h_ref, wg_ref, wu_ref, o_ref, wg_s, wu_s):
    @pl.when(pl.program_id(1) == 0)
    def _():
        wg_s[...] = wg_ref[...].astype(BF16)
        wu_s[...] = wu_ref[...].astype(BF16)

    h = h_ref[...]
    g = _dot(h, wg_s[...])
    u = _dot(h, wu_s[...])
    o_ref[...] = (_silu(g) * u).astype(BF16)


def _gateup(h, w_gate, w_up, layer, j):
    m, d = h.shape
    f = w_gate.shape[-1]
    bm = _pick(m, (1024, 512, 256, 128, 64))
    bn = _pick(f, (512, 256, 128))
    wspec = pl.BlockSpec((None, None, d, bn), lambda n, i: (layer, j, 0, n))
    return pl.pallas_call(
        _gateup_kernel,
        grid=(f // bn, m // bm),
        in_specs=[pl.BlockSpec((bm, d), lambda n, i: (i, 0)), wspec, wspec],
        out_specs=pl.BlockSpec((bm, bn), lambda n, i: (i, n)),
        out_shape=jax.ShapeDtypeStruct((m, f), BF16),
        scratch_shapes=[pltpu.VMEM((d, bn), BF16), pltpu.VMEM((d, bn), BF16)],
        name="gateup",
        compiler_params=_params("arbitrary", "arbitrary"),
    )(h, w_gate, w_up)


def _proj_ln_kernel(a_ref, w_ref, x_ref, gate_ref, lng_ref, lnb_ref, *rest, nk, bk, alpha, coef, emit_h):
    if emit_h:
        sc_ref, sh_ref, xo_ref, ho_ref, w_s = rest
    else:
        xo_ref, w_s = rest
    i = pl.program_id(0)

    @pl.when(i < nk)
    def _():
        w_s[pl.ds(pl.multiple_of(i * bk, bk), bk), :] = w_ref[...].astype(BF16)

    @pl.when(i >= nk)
    def _():
        acc = _dot(a_ref[...], w_s[...])
        y = alpha * x_ref[...] + (coef * (1.0 + gate_ref[...])) * acc
        mu = jnp.mean(y, axis=-1, keepdims=True)
        yc = y - mu
        var = jnp.mean(yc * yc, axis=-1, keepdims=True)
        xn = yc * lax.rsqrt(var + LN_EPS) * lng_ref[...] + lnb_ref[...]
        xo_ref[...] = xn
        if emit_h:
            ho_ref[...] = (xn * (1.0 + sc_ref[...]) + sh_ref[...]).astype(BF16)


def _proj_ln(a, w, w_prefix, x, mod, layer, j, nxt, ln_g, ln_b, rows_per_group, alpha, coef):
    m, k = a.shape
    d = x.shape[1]
    bm = _pick(rows_per_group, (256, 128, 64))
    bk = _pick(k, (512, 256, 128))
    nk = k // bk
    emit_h = nxt is not None
    npre = len(w_prefix)

    def row(i):
        return jnp.maximum(i - nk, 0)

    def shifted(spec_fn):
        return spec_fn

    in_specs = [
        pl.BlockSpec((bm, k), lambda i: (row(i), 0)),
        pl.BlockSpec((None,) * npre + (bk, d), lambda i: tuple(w_prefix) + (jnp.minimum(i, nk - 1), 0)),
        pl.BlockSpec((bm, d), lambda i: (row(i), 0)),
    ]

    def modspec(lyr, slot):
        r = mod.shape[3]
        if r == 1:
            return pl.BlockSpec((None, None, None, 1, d),
                                lambda i: (lyr, slot, (row(i) * bm) // rows_per_group, 0, 0))
        return pl.BlockSpec((None, None, None, bm, d), lambda i: (lyr, slot, 0, row(i), 0))

    in_specs.append(modspec(layer, 3 * j + 2))
    in_specs += [pl.BlockSpec((None, None, 1, d), lambda i: (layer, j, 0, 0)),
                 pl.BlockSpec((None, None, 1, d), lambda i: (layer, j, 0, 0))]
    args = [a, w, x, mod, ln_g, ln_b]
    out_specs = [pl.BlockSpec((bm, d), lambda i: (row(i), 0))]
    out_shape = [jax.ShapeDtypeStruct((m, d), F32)]
    if emit_h:
        in_specs += [modspec(nxt[0], 3 * nxt[1] + 1), modspec(nxt[0], 3 * nxt[1] + 0)]
        args += [mod, mod]
        out_specs.append(pl.BlockSpec((bm, d), lambda i: (row(i), 0)))
        out_shape.append(jax.ShapeDtypeStruct((m, d), BF16))
    out = pl.pallas_call(
        functools.partial(_proj_ln_kernel, nk=nk, bk=bk, alpha=alpha, coef=coef, emit_h=emit_h),
        grid=(nk + m // bm,),
        in_specs=in_specs,
        out_specs=out_specs,
        out_shape=out_shape,
        scratch_shapes=[pltpu.VMEM((k, d), BF16)],
        name="proj_ln",
        compiler_params=_params("arbitrary"),
    )(*args)
    return (out[0], out[1]) if emit_h else (out[0], None)


def _win_kernel(h_ref, w_ref, b_ref, o_ref):
    o_ref[...] = _dot_nt(h_ref[...], w_ref[...]) + b_ref[...]


def _win(h, w, b, layer):
    m, d = h.shape
    n = w.shape[1]
    bm = _pick(m, (1024, 512, 256, 128, 64))
    bn = _pick(n, (512, 256, 128))
    return pl.pallas_call(
        _win_kernel,
        grid=(m // bm, n // bn),
        in_specs=[pl.BlockSpec((bm, d), lambda i, c: (i, 0)),
                  pl.BlockSpec((None, bn, d), lambda i, c: (layer, c, 0)),
                  pl.BlockSpec((None, 1, bn), lambda i, c: (layer, 0, c))],
        out_specs=pl.BlockSpec((bm, bn), lambda i, c: (i, c)),
        out_shape=jax.ShapeDtypeStruct((m, n), F32),
        name="w_in",
        compiler_params=_params("arbitrary", "arbitrary"),
    )(h, w, b)


def _gates_kernel(x_ref, a_ref, dt_ref, gc_ref, grc_ref, grf_ref, carry_ref, *, hm, hf, hg, lc, seg, tb):
    t = pl.program_id(1)

    @pl.when(t == 0)
    def _():
        carry_ref[...] = jnp.zeros_like(carry_ref)

    z = jnp.transpose(x_ref[...])[:GATE_ROWS]
    ch = lax.broadcasted_iota(jnp.int32, (GATE_ROWS, 1), 0)
    is_mi = ch < hm
    is_mf = (ch >= hm) & (ch < 2 * hm)
    is_ff = (ch >= 2 * hm) & (ch < 2 * hm + hf)
    is_ga = (ch >= 2 * hm + hf) & (ch < 2 * hm + hf + hg)
    is_gb = (ch >= 2 * hm + hf + hg) & (ch < 2 * hm + hf + 2 * hg)
    ls = _log_sigmoid(z)
    gg = -jnp.exp(a_ref[...]) * _softplus(z + dt_ref[...])
    sg = _sigmoid(z)
    s_i = lax.broadcasted_iota(jnp.int32, (tb, tb), 0)
    t_i = lax.broadcasted_iota(jnp.int32, (tb, tb), 1)
    m_loc = jnp.where((s_i <= t_i) & ((s_i // lc) == (t_i // lc)), 1.0, 0.0).astype(F32)
    m_seg = jnp.where((s_i <= t_i) & ((s_i // seg) == (t_i // seg)), 1.0, 0.0).astype(F32)
    v_loc = jnp.where(is_mf, ls, jnp.where(is_ga, gg, 0.0))
    v_seg = jnp.where(is_ff, ls, 0.0)
    cum_loc = _dot(v_loc, m_loc, HI)
    cum_seg = _dot(v_seg, m_seg, HI) + carry_ref[...]
    if seg > tb:
        carry_ref[...] = cum_seg[:, tb - 1:tb]
    bank0 = jnp.where(is_mi, z, jnp.where(is_mf | is_ga, cum_loc,
                                          jnp.where(is_ff, cum_seg, jnp.where(is_gb, sg, 0.0))))
    bank1 = jnp.where(is_ff, ls, 0.0)
    rows = jnp.concatenate([bank0, bank1], axis=0)
    full = jnp.concatenate([rows, jnp.zeros((LANES - 2 * GATE_ROWS, tb), F32)], axis=0)
    gc_ref[...] = jnp.transpose(full)
    for c in range(tb // lc):
        grc_ref[c] = rows[:, c * lc:(c + 1) * lc]
    grf_ref[...] = bank0[2 * hm:2 * hm + hf]


def _gates(proj, gate_blk, a_col, dt_col, layer, nb, t, lc, seg, dims):
    hm, hf, hg = dims
    tb = _pick(t, (512, 256, 128, 64))
    nt = t // tb
    return pl.pallas_call(
        functools.partial(_gates_kernel, hm=hm, hf=hf, hg=hg, lc=lc, seg=seg, tb=tb),
        grid=(nb, nt),
        in_specs=[pl.BlockSpec((tb, LANES), lambda b, i: (b * nt + i, gate_blk)),
                  pl.BlockSpec((None, GATE_ROWS, 1), lambda b, i: (layer, 0, 0)),
                  pl.BlockSpec((None, GATE_ROWS, 1), lambda b, i: (layer, 0, 0))],
        out_specs=[pl.BlockSpec((tb, LANES), lambda b, i: (b * nt + i, 0)),
                   pl.BlockSpec((None, tb // lc, 2 * GATE_ROWS, lc), lambda b, i: (b, i, 0, 0)),
                   pl.BlockSpec((None, hf, tb), lambda b, i: (b, 0, i))],
        out_shape=[jax.ShapeDtypeStruct((nb * t, LANES), F32),
                   jax.ShapeDtypeStruct((nb, t // lc, 2 * GATE_ROWS, lc), F32),
                   jax.ShapeDtypeStruct((nb, hf, t), F32)],
        scratch_shapes=[pltpu.VMEM((GATE_ROWS, 1), F32)],
        name="gates",
        compiler_params=_params("arbitrary", "arbitrary"),
    )(proj, a_col, dt_col)


def _mlstm_kernel(q_ref, k_ref, v_ref, o_ref, gc_ref, gr_ref, ng_ref, c0_ref, n0_ref, m0_ref,
                  h_ref, c_ref, n_ref, m_ref, *, hm, lc, bpb, mxu_dtype):
    @pl.when(pl.program_id(1) == 0)
    def _():
        c_ref[...] = c0_ref[...]
        n_ref[...] = n0_ref[...]
        m_ref[...] = m0_ref[...]

    r_i = lax.broadcasted_iota(jnp.int32, (lc, lc), 0)
    c_i = lax.broadcasted_iota(jnp.int32, (lc, lc), 1)
    causal = c_i <= r_i
    scale = HEAD_DIM ** -0.5
    chains = [(b, h) for b in range(bpb) for h in range(hm)]
    sls = [slice(h * HEAD_DIM, (h + 1) * HEAD_DIM) for _, h in chains]
    gate = []
    for b, h in chains:
        gc = gc_ref[b]
        gr = gr_ref[b]
        li_col = gc[:, h:h + 1]
        b_col = gc[:, hm + h:hm + h + 1]
        li_row = gr[h:h + 1, :]
        b_row = gr[hm + h:hm + h + 1, :]
        mst = m_ref[b, h]
        dmat = jnp.where(causal, b_col - b_row + li_row, NEG_INF)
        inter = b_col + mst
        b_last = b_col[lc - 1:lc, :]
        d_last = b_last - b_col + li_col
        gate.append(dict(dmat=dmat, inter=inter, d_last=d_last, carry=b_last + mst,
                         dmax=jnp.max(dmat, axis=-1, keepdims=True), lmax=jnp.max(d_last, axis=0, keepdims=True)))
    for g in gate:
        m_row = jnp.maximum(g["inter"], g["dmax"])
        m_new = jnp.maximum(g["carry"], g["lmax"])
        g.update(m_row=m_row, w_intra=jnp.exp(g["dmat"] - m_row), w_inter=jnp.exp(g["inter"] - m_row),
                 m_new=m_new, w_k=jnp.exp(g["d_last"] - m_new), decay=jnp.exp(g["carry"] - m_new))
    first = []
    for (b, h), sl in zip(chains, sls):
        q = q_ref[b, :, sl]
        k = k_ref[b, :, sl] * scale
        qx = q.astype(mxu_dtype)
        vx = v_ref[b, :, sl].astype(mxu_dtype)
        first.append(dict(q=q, k=k, vx=vx, qk=_dot_nt(qx, k.astype(mxu_dtype)),
                          qc=_dot(qx, c_ref[b, h].astype(mxu_dtype))))
    outs = []
    for (b, h), g, f in zip(chains, gate, first):
        cst = c_ref[b, h]
        nst = n_ref[b, h]
        s = f["qk"] * g["w_intra"]
        num = _dot(s.astype(mxu_dtype), f["vx"]) + g["w_inter"] * f["qc"]
        den = jnp.sum(s, axis=-1, keepdims=True) + g["w_inter"] * jnp.sum(f["q"] * nst, axis=-1, keepdims=True)
        outs.append(num / jnp.maximum(jnp.abs(den), jnp.exp(-g["m_row"])))
        kw = f["k"] * g["w_k"]
        c_ref[b, h] = g["decay"] * cst + _dot_tn(kw.astype(mxu_dtype), f["vx"])
        n_ref[b, h] = g["decay"] * nst + jnp.sum(kw, axis=0, keepdims=True)
        m_ref[b, h] = g["m_new"]
    for (b, h), sl, hh in zip(chains, sls, outs):
        mu = jnp.mean(hh, axis=-1, keepdims=True)
        hc = hh - mu
        var = jnp.mean(hc * hc, axis=-1, keepdims=True)
        hn = hc * lax.rsqrt(var + LN_EPS) * ng_ref[h]
        h_ref[b, :, sl] = (hn * _sigmoid(o_ref[b, :, sl])).astype(h_ref.dtype)


def _mlstm(proj, gc, grc, norm_g, c0, n0, m0, layer, nb, t, bpb, dims):
    hm = dims[0]
    dm = hm * HEAD_DIM
    lc = math.gcd(t, CHUNK)
    nc = t // lc
    mxu_dtype = BF16 if lc >= 16 else F32

    def col(j):
        return pl.BlockSpec((bpb, lc, dm), lambda b, c: (b, c, j))

    st4 = pl.BlockSpec((bpb, hm, HEAD_DIM, HEAD_DIM), lambda b, c: (b, 0, 0, 0))
    stn = pl.BlockSpec((bpb, hm, 1, HEAD_DIM), lambda b, c: (b, 0, 0, 0))
    stm = pl.BlockSpec((bpb, hm, 1, 1), lambda b, c: (b, 0, 0, 0))
    return pl.pallas_call(
        functools.partial(_mlstm_kernel, hm=hm, lc=lc, bpb=bpb, mxu_dtype=mxu_dtype),
        grid=(nb // bpb, nc),
        in_specs=[col(0), col(1), col(2), col(3),
                  pl.BlockSpec((bpb, lc, LANES), lambda b, c: (b, c, 0)),
                  pl.BlockSpec((bpb, None, 2 * GATE_ROWS, lc), lambda b, c: (b, c, 0, 0)),
                  pl.BlockSpec((None, hm, 1, HEAD_DIM), lambda b, c: (layer, 0, 0, 0)),
                  st4, stn, stm],
        out_specs=[pl.BlockSpec((bpb, lc, dm), lambda b, c: (b, c, 0)), st4, stn, stm],
        out_shape=[jax.ShapeDtypeStruct((nb, t, dm), mxu_dtype),
                   jax.ShapeDtypeStruct((nb, hm, HEAD_DIM, HEAD_DIM), F32),
                   jax.ShapeDtypeStruct((nb, hm, 1, HEAD_DIM), F32),
                   jax.ShapeDtypeStruct((nb, hm, 1, 1), F32)],
        name="mlstm",
        compiler_params=_params("arbitrary", "arbitrary"),
    )(proj, proj, proj, proj, gc, grc, norm_g, c0, n0, m0)


def _split(a):
    hi = a.astype(BF16)
    return hi, (a - hi.astype(F32)).astype(BF16)


def _mm3(a, b):
    return _dot(a[0], b[0]) + (_dot(a[0], b[1]) + _dot(a[1], b[0]))


def _mm3_nt(a, b):
    return _dot_nt(a[0], b[0]) + (_dot_nt(a[0], b[1]) + _dot_nt(a[1], b[0]))


def _unit_lower_solve(a, rhs, lc, split_bf16):
    r_i = lax.broadcasted_iota(jnp.int32, (lc, lc), 0)
    c_i = lax.broadcasted_iota(jnp.int32, (lc, lc), 1)
    eye = jnp.where(r_i == c_i, 1.0, 0.0).astype(F32)
    if split_bf16:
        prep, mm = _split, _mm3
    else:
        prep, mm = (lambda x: x), (lambda x, y: _dot(x, y, HI))
    npows = [-x for x in a]
    invs = [eye + n for n in npows]
    width = 2
    while width < lc:
        ns = [prep(n) for n in npows]
        npows = [mm(n, n) for n in ns]
        invs = [inv + mm(prep(inv), prep(n)) for inv, n in zip(invs, npows)]
        width *= 2
    return [mm(prep(inv), prep(r)) for inv, r in zip(invs, rhs)]


def _gdn_prep_kernel(q_ref, k_ref, v_ref, pq_ref, pk_ref, pv_ref, b0_ref, gc_ref, gr_ref, cw_ref,
                     w_ref, u_ref, qg_ref, kd_ref, qk_ref, buf_ref, xp_ref, *, hm, hf, hg, lc, cb):
    dg = hg * HEAD_DIM
    keep = SUBLANES
    tb = cb * lc
    split_bf16 = lc >= 16
    prev = jnp.concatenate([pq_ref[keep - 3:keep, :], pk_ref[keep - 3:keep, :], pv_ref[keep - 3:keep, :]], axis=1)
    xp_ref[keep - 3:keep, :] = jnp.where(pl.program_id(1) == 0, b0_ref[...], prev)
    xp_ref[keep:keep + tb, 0:dg] = q_ref[...]
    xp_ref[keep:keep + tb, dg:2 * dg] = k_ref[...]
    xp_ref[keep:keep + tb, 2 * dg:3 * dg] = v_ref[...]
    conv = xp_ref[keep:keep + tb, :] * cw_ref[3:4, :]
    for j in range(3):
        conv = conv + xp_ref[keep - 3 + j:keep - 3 + j + tb, :] * cw_ref[j:j + 1, :]
    buf_ref[...] = xp_ref[keep + tb - 3:keep + tb, :]
    act = _silu(conv)

    r_i = lax.broadcasted_iota(jnp.int32, (lc, lc), 0)
    c_i = lax.broadcasted_iota(jnp.int32, (lc, lc), 1)
    incl = c_i <= r_i
    strict = c_i < r_i
    ch_g = 2 * hm + hf
    ch_b = ch_g + hg
    scale = HEAD_DIM ** -0.5
    chains = [(c, h) for c in range(cb) for h in range(hg)]
    amats, rhss, qks = [], [], []
    for c, h in chains:
        rows = slice(c * lc, (c + 1) * lc)
        sl = slice(h * HEAD_DIM, (h + 1) * HEAD_DIM)
        gc = gc_ref[rows, :]
        gr = gr_ref[c]
        cq = act[rows, h * HEAD_DIM:(h + 1) * HEAD_DIM]
        ck = act[rows, dg + h * HEAD_DIM:dg + (h + 1) * HEAD_DIM]
        cv = act[rows, 2 * dg + h * HEAD_DIM:2 * dg + (h + 1) * HEAD_DIM]
        qn = cq * lax.rsqrt(jnp.sum(cq * cq, axis=-1, keepdims=True) + NORM_EPS) * scale
        kn = ck * lax.rsqrt(jnp.sum(ck * ck, axis=-1, keepdims=True) + NORM_EPS)
        g_col = gc[:, ch_g + h:ch_g + h + 1]
        g_row = gr[ch_g + h:ch_g + h + 1, :]
        beta = gc[:, ch_b + h:ch_b + h + 1]
        decay = jnp.exp(jnp.where(incl, g_col - g_row, NEG_INF))
        if split_bf16:
            ks = _split(kn)
            kk = _mm3_nt(ks, ks)
            qk = _dot_nt(qn.astype(BF16), ks[0])
        else:
            kk = _dot_nt(kn, kn, HI)
            qk = _dot_nt(qn, kn)
        amats.append(jnp.where(strict, beta * kk * decay, 0.0))
        eg = jnp.exp(g_col)
        rhss.append(jnp.concatenate([kn * (beta * eg), cv * beta], axis=-1))
        g_last = g_col[lc - 1:lc, :]
        qg_ref[rows, sl] = (qn * eg).astype(qg_ref.dtype)
        kd_ref[rows, sl] = (kn * jnp.exp(g_last - g_col)).astype(kd_ref.dtype)
        qks.append((qk * decay).astype(qk_ref.dtype))
        if h == hg - 1:
            qk_ref[rows, :] = jnp.concatenate(qks[-hg:], axis=1)
    sols = _unit_lower_solve(amats, rhss, lc, split_bf16)
    for (c, h), sol in zip(chains, sols):
        rows = slice(c * lc, (c + 1) * lc)
        sl = slice(h * HEAD_DIM, (h + 1) * HEAD_DIM)
        w_ref[rows, sl] = sol[:, :HEAD_DIM].astype(w_ref.dtype)
        u_ref[rows, sl] = sol[:, HEAD_DIM:]


def _gdn_scan_kernel(w_ref, u_ref, qg_ref, kd_ref, qk_ref, z_ref, gc_ref, ng_ref, s0_ref,
                     h_ref, s_ref, *, hm, hf, hg, lc, bpb):
    @pl.when(pl.program_id(1) == 0)
    def _():
        s_ref[...] = s0_ref[...]

    ch_g = 2 * hm + hf
    xd = w_ref.dtype
    chains = [(b, h) for b in range(bpb) for h in range(hg)]
    sls = [slice(h * HEAD_DIM, (h + 1) * HEAD_DIM) for _, h in chains]
    first = []
    for (b, h), sl in zip(chains, sls):
        sx = s_ref[b, h].astype(xd)
        first.append((_dot(w_ref[b, :, sl], sx), _dot(qg_ref[b, :, sl], sx)))
    outs = []
    for (b, h), sl, (ws, qs) in zip(chains, sls, first):
        ux = (u_ref[b, :, sl] - ws).astype(xd)
        outs.append(qs + _dot(qk_ref[b][:, h * lc:(h + 1) * lc], ux))
        g_last = gc_ref[b, lc - SUBLANES:lc, :][SUBLANES - 1:SUBLANES, ch_g + h:ch_g + h + 1]
        s_ref[b, h] = jnp.exp(g_last) * s_ref[b, h] + _dot_tn(kd_ref[b, :, sl], ux)
    for (b, h), sl, o in zip(chains, sls, outs):
        on = o * lax.rsqrt(jnp.mean(o * o, axis=-1, keepdims=True) + NORM_EPS) * ng_ref[...]
        h_ref[b, :, sl] = (on * _silu(z_ref[b, :, sl])).astype(h_ref.dtype)


def _gdn(proj, gc, grc, conv_w, norm_g, s0, b0, layer, nb, t, bpb, dims):
    hm, hf, hg = dims
    dg = hg * HEAD_DIM
    lc = math.gcd(t, CHUNK)
    nc = t // lc
    xd = BF16 if lc >= 16 else F32
    base = (4 * hm * HEAD_DIM + 3 * hf * HEAD_DIM) // dg
    cb = _pick(nc, (2, 1))
    tb = cb * lc
    nblk = t // tb
    m = nb * t
    n_all = proj.shape[-1]
    proj2 = proj.reshape(m, n_all)

    def col(j):
        return pl.BlockSpec((tb, dg), lambda b, i: (b * nblk + i, base + j))

    def prev(j):
        return pl.BlockSpec((SUBLANES, dg),
                            lambda b, i: (jnp.maximum((b * t + i * tb) // SUBLANES - 1, 0), base + j))

    def rowblk(width):
        return pl.BlockSpec((tb, width), lambda b, i: (b * nblk + i, 0))

    w, u0, qg, kd, qk, b1 = pl.pallas_call(
        functools.partial(_gdn_prep_kernel, hm=hm, hf=hf, hg=hg, lc=lc, cb=cb),
        grid=(nb, nblk),
        in_specs=[col(0), col(1), col(2), prev(0), prev(1), prev(2),
                  pl.BlockSpec((None, 3, 3 * dg), lambda b, i: (b, 0, 0)),
                  rowblk(LANES),
                  pl.BlockSpec((None, cb, 2 * GATE_ROWS, lc), lambda b, i: (b, i, 0, 0)),
                  pl.BlockSpec((None, 4, 3 * dg), lambda b, i: (layer, 0, 0))],
        out_specs=[rowblk(dg), rowblk(dg), rowblk(dg), rowblk(dg), rowblk(hg * lc),
                   pl.BlockSpec((None, 3, 3 * dg), lambda b, i: (b, 0, 0))],
        out_shape=[jax.ShapeDtypeStruct((m, dg), xd), jax.ShapeDtypeStruct((m, dg), F32),
                   jax.ShapeDtypeStruct((m, dg), xd), jax.ShapeDtypeStruct((m, dg), xd),
                   jax.ShapeDtypeStruct((m, hg * lc), xd), jax.ShapeDtypeStruct((nb, 3, 3 * dg), F32)],
        scratch_shapes=[pltpu.VMEM((tb + SUBLANES, 3 * dg), F32)],
        name="gdn_prep",
        compiler_params=_params("arbitrary", "arbitrary"),
    )(proj2, proj2, proj2, proj2, proj2, proj2, b0, gc.reshape(m, LANES), grc, conv_w)

    def seq(width):
        return pl.BlockSpec((bpb, lc, width), lambda b, c: (b, c, 0))

    st4 = pl.BlockSpec((bpb, hg, HEAD_DIM, HEAD_DIM), lambda b, c: (b, 0, 0, 0))
    hgx, s1 = pl.pallas_call(
        functools.partial(_gdn_scan_kernel, hm=hm, hf=hf, hg=hg, lc=lc, bpb=bpb),
        grid=(nb // bpb, nc),
        in_specs=[seq(dg), seq(dg), seq(dg), seq(dg), seq(hg * lc),
                  pl.BlockSpec((bpb, lc, dg), lambda b, c: (b, c, base + 3)),
                  seq(LANES),
                  pl.BlockSpec((None, 1, HEAD_DIM), lambda b, c: (layer, 0, 0)),
                  st4],
        out_specs=[seq(dg), st4],
        out_shape=[jax.ShapeDtypeStruct((nb, t, dg), xd),
                   jax.ShapeDtypeStruct((nb, hg, HEAD_DIM, HEAD_DIM), F32)],
        name="gdn_scan",
        compiler_params=_params("arbitrary", "arbitrary"),
    )(w.reshape(nb, t, dg), u0.reshape(nb, t, dg), qg.reshape(nb, t, dg), kd.reshape(nb, t, dg),
      qk.reshape(nb, t, hg * lc), proj, gc, norm_g, s0)
    return hgx, s1, b1


def _fox_prompt_kernel(q_ref, k_ref, v_ref, gr_ref, o_ref, q_s, m_s, l_s, acc_s, *, hf, tb):
    qi = pl.program_id(1)
    ki = pl.program_id(2)
    rep = tb // LANES

    @pl.when(ki == 0)
    def _():
        q_s[...] = (q_ref[...] * HEAD_DIM ** -0.5).astype(BF16)
        m_s[...] = jnp.full(m_s.shape, NEG_INF, F32)
        l_s[...] = jnp.zeros(l_s.shape, F32)
        acc_s[...] = jnp.zeros(acc_s.shape, F32)

    def block(masked):
        if masked:
            visible = (lax.broadcasted_iota(jnp.int32, (tb, tb), 1) <= lax.broadcasted_iota(jnp.int32, (tb, tb), 0))
        for h in range(hf):
            sl = slice(h * HEAD_DIM, (h + 1) * HEAD_DIM)
            k = k_ref[:, sl].astype(BF16)
            v = v_ref[:, sl].astype(BF16)
            s = _dot_nt(q_s[:, sl], k) - gr_ref[h:h + 1, :]
            if masked:
                s = jnp.where(visible, s, NEG_INF)
            m_prev = m_s[h]
            m_new = jnp.maximum(m_prev, jnp.max(s, axis=-1, keepdims=True))
            alpha = jnp.exp(m_prev - m_new)
            p = jnp.exp(s - jnp.concatenate([m_new] * rep, axis=1))
            l_s[h] = alpha * l_s[h] + jnp.sum(p, axis=-1, keepdims=True)
            acc_s[:, sl] = alpha * acc_s[:, sl] + _dot(p.astype(BF16), v)
            m_s[h] = m_new

    @pl.when(ki < qi)
    def _():
        block(False)

    @pl.when(ki == qi)
    def _():
        block(True)
        for h in range(hf):
            sl = slice(h * HEAD_DIM, (h + 1) * HEAD_DIM)
            o_ref[:, sl] = (acc_s[:, sl] / l_s[h]).astype(BF16)


def _fox_prompt(proj, grf, nb, t, dims):
    hm, hf, hg = dims
    df = hf * HEAD_DIM
    base = (4 * hm * HEAD_DIM) // df
    tb = _pick(t, (512, 256, 128))
    nq = t // tb
    return pl.pallas_call(
        functools.partial(_fox_prompt_kernel, hf=hf, tb=tb),
        grid=(nb, nq, nq),
        in_specs=[pl.BlockSpec((tb, df), lambda b, i, j: (b * nq + i, base)),
                  pl.BlockSpec((tb, df), lambda b, i, j: (b * nq + jnp.minimum(j, i), base + 1)),
                  pl.BlockSpec((tb, df), lambda b, i, j: (b * nq + jnp.minimum(j, i), base + 2)),
                  pl.BlockSpec((None, hf, tb), lambda b, i, j: (b, 0, jnp.minimum(j, i)))],
        out_specs=pl.BlockSpec((tb, df), lambda b, i, j: (b * nq + i, 0)),
        out_shape=jax.ShapeDtypeStruct((nb * t, df), BF16),
        scratch_shapes=[pltpu.VMEM((tb, df), BF16), pltpu.VMEM((hf, tb, LANES), F32),
                        pltpu.VMEM((hf, tb, LANES), F32), pltpu.VMEM((tb, df), F32)],
        name="fox_prompt",
        compiler_params=_params("arbitrary", "arbitrary", "arbitrary"),
    )(proj, proj, proj, grf)


def _fox_pool_kernel(x_ref, w_ref, t_ref, *, hf, page):
    n = hf * page
    x = x_ref[...]
    x1 = x.astype(BF16)
    r1 = x - x1.astype(F32)
    x2 = r1.astype(BF16)
    x3 = (r1 - x2.astype(F32)).astype(BF16)
    ri = lax.broadcasted_iota(jnp.int32, (n, n), 0)
    ci = lax.broadcasted_iota(jnp.int32, (n, n), 1)
    later = ((ri // page) == (ci % hf)) & ((ri % page) > (ci // hf))
    m_suf = jnp.where(later, 1.0, 0.0).astype(BF16)
    rj = lax.broadcasted_iota(jnp.int32, (n, LANES), 0)
    lj = lax.broadcasted_iota(jnp.int32, (n, LANES), 1)
    m_tot = jnp.where((rj // page) == (lj % hf), 1.0, 0.0).astype(BF16)
    w_ref[...] = _dot(x1, m_suf) + (_dot(x2, m_suf) + _dot(x3, m_suf))
    t_ref[...] = _dot(x1, m_tot) + (_dot(x2, m_tot) + _dot(x3, m_tot))


def _fox_pool(logf_hs, hf, page):
    r, n = logf_hs.shape
    rb = _pick(r, (512, 256, 128, 64, 32, 16, 8))
    return pl.pallas_call(
        functools.partial(_fox_pool_kernel, hf=hf, page=page),
        grid=(r // rb,),
        in_specs=[pl.BlockSpec((rb, n), lambda i: (i, 0))],
        out_specs=[pl.BlockSpec((rb, n), lambda i: (i, 0)), pl.BlockSpec((rb, LANES), lambda i: (i, 0))],
        out_shape=[jax.ShapeDtypeStruct((r, n), F32), jax.ShapeDtypeStruct((r, LANES), F32)],
        name="fox_pool",
        compiler_params=_params("arbitrary"),
    )(logf_hs)


def _fox_sample_kernel(pt_ref, q_ref, cr_ref, kn_ref, vn_ref, *rest, hf, nsteps, rpp, npp):
    kp = rest[0:npp]
    vp = rest[npp:2 * npp]
    wp = rest[2 * npp:3 * npp]
    tp = rest[3 * npp:4 * npp]
    o_ref, m_s, l_s, acc_s, carry_s = rest[4 * npp:]
    p = pl.program_id(1)
    q = (q_ref[...] * HEAD_DIM ** -0.5).astype(BF16)
    nr = q.shape[0]

    @pl.when(p == 0)
    def _():
        ri = lax.broadcasted_iota(jnp.int32, (nr, nr), 0)
        ci = lax.broadcasted_iota(jnp.int32, (nr, nr), 1)
        ok = ((ri % hf) == (ci % hf)) & ((ci // hf) <= (ri // hf))
        s = jnp.where(ok, _dot_nt(q, kn_ref[...].astype(BF16)) - cr_ref[...], NEG_INF)
        m0 = jnp.max(s, axis=-1, keepdims=True)
        e = jnp.exp(s - m0)
        m_s[...] = m0
        l_s[...] = jnp.sum(e, axis=-1, keepdims=True)
        acc_s[...] = _dot(e.astype(BF16), vn_ref[...].astype(BF16))
        carry_s[...] = jnp.zeros_like(carry_s)

    ncol = rpp * LANES
    ri = lax.broadcasted_iota(jnp.int32, (nr, ncol), 0)
    ci = lax.broadcasted_iota(jnp.int32, (nr, ncol), 1)
    head_mask = jnp.where((ri % hf) == (ci % hf), 0.0, NEG_INF).astype(F32)
    carry = carry_s[...]
    scores = []
    for g in range(npp):
        bias_row = jnp.concatenate([wp[g][r:r + 1, :] + carry for r in range(rpp)], axis=1)
        scores.append(_dot_nt(q, kp[g][...].astype(BF16)) + (head_mask + bias_row))
        carry = carry + tp[g][...]
    carry_s[...] = carry
    top = scores[0]
    for g in range(1, npp):
        top = jnp.maximum(top, scores[g])
    m_prev = m_s[...]
    m_new = jnp.maximum(m_prev, jnp.max(top, axis=-1, keepdims=True))
    alpha = jnp.exp(m_prev - m_new)
    acc = alpha * acc_s[...]
    tot = None
    for g in range(npp):
        e = jnp.exp(scores[g] - m_new)
        tot = e if tot is None else tot + e
        acc = acc + _dot(e.astype(BF16), vp[g][...].astype(BF16))
    l_s[...] = alpha * l_s[...] + jnp.sum(tot, axis=-1, keepdims=True)
    acc_s[...] = acc
    m_s[...] = m_new

    @pl.when(p == nsteps - 1)
    def _():
        o_ref[...] = (acc / l_s[...]).astype(BF16)


def _fox_sample(page_table, q2, cum_row, kn2, vn2, kpool, vpool, wpool, tpool, layer, hf):
    nb, npg = page_table.shape
    nr = q2.shape[1]
    nphys, pcols = kpool.shape[1], kpool.shape[2]
    rpp = pcols // LANES
    npp = _pick(npg, (8, 4, 2, 1))
    nsteps = npg // npp

    def page(b, p, pt, g):
        return pt[b * npg + (npg - 1 - (p * npp + g))]

    per_b3 = lambda b, p, pt: (b, 0, 0)
    kv_specs = [pl.BlockSpec((None, None, pcols, HEAD_DIM),
                             lambda b, p, pt, g=g: (layer, page(b, p, pt, g), 0, 0)) for g in range(npp)]
    w_specs = [pl.BlockSpec((None, rpp, LANES),
                            lambda b, p, pt, g=g: (layer * nphys + page(b, p, pt, g), 0, 0)) for g in range(npp)]
    t_specs = [pl.BlockSpec((None, 1, LANES),
                            lambda b, p, pt, g=g: (layer * nphys + page(b, p, pt, g), 0, 0)) for g in range(npp)]
    return pl.pallas_call(
        functools.partial(_fox_sample_kernel, hf=hf, nsteps=nsteps, rpp=rpp, npp=npp),
        grid_spec=pltpu.PrefetchScalarGridSpec(
            num_scalar_prefetch=1,
            grid=(nb, nsteps),
            in_specs=[pl.BlockSpec((None, nr, HEAD_DIM), per_b3),
                      pl.BlockSpec((None, 1, nr), per_b3),
                      pl.BlockSpec((None, nr, HEAD_DIM), per_b3),
                      pl.BlockSpec((None, nr, HEAD_DIM), per_b3)] + kv_specs + kv_specs + w_specs + t_specs,
            out_specs=pl.BlockSpec((None, nr, HEAD_DIM), per_b3),
            scratch_shapes=[pltpu.VMEM((nr, 1), F32), pltpu.VMEM((nr, 1), F32),
                            pltpu.VMEM((nr, HEAD_DIM), F32), pltpu.VMEM((1, LANES), F32)]),
        out_shape=jax.ShapeDtypeStruct((nb, nr, HEAD_DIM), BF16),
        name="fox_decode",
        compiler_params=_params("arbitrary", "arbitrary"),
    )(page_table.reshape(-1), q2, cum_row, kn2, vn2, *([kpool] * npp), *([vpool] * npp),
      *([wpool] * npp), *([tpool] * npp))


def _regroup_w_in(w_in, b_in, dims):
    hm, hf, hg = dims
    dm, df, dg = hm * HEAD_DIM, hf * HEAD_DIM, hg * HEAD_DIM
    sizes = (dm, dm, dm, dm, hm, hm, df, df, df, hf, dg, dg, dg, dg, hg, hg)
    offs = [0]
    for s in sizes:
        offs.append(offs[-1] + s)
    order = (0, 1, 2, 3, 6, 7, 8, 10, 11, 12, 13, 4, 5, 9, 14, 15)
    n_main = 4 * dm + 3 * df + 4 * dg
    n_gate = 2 * hm + hf + 2 * hg
    n_tot = -(-(n_main + LANES) // 512) * 512
    pad = n_tot - n_main - n_gate

    def regroup(a, axis):
        parts = [lax.slice_in_dim(a, offs[i], offs[i + 1], axis=axis) for i in order]
        zshape = list(a.shape)
        zshape[axis] = pad
        parts.append(jnp.zeros(zshape, a.dtype))
        return jnp.concatenate(parts, axis=axis)

    w_t = regroup(jnp.swapaxes(w_in, 1, 2).astype(BF16), 1)
    return w_t, regroup(b_in, 1)[:, None, :], n_main // LANES


def _gate_param_cols(a_log, dt_bias, dims):
    hm, hf, hg = dims
    depth = a_log.shape[0]
    lead = 2 * hm + hf
    z0 = jnp.zeros((depth, lead), F32)
    z1 = jnp.zeros((depth, GATE_ROWS - lead - hg), F32)
    a_col = jnp.concatenate([z0, a_log, z1], axis=1)[:, :, None]
    dt_col = jnp.concatenate([z0, dt_bias, z1], axis=1)[:, :, None]
    return a_col, dt_col


def kernel(x_prompt, x_sample, cache_fox_k, cache_fox_v, cache_fox_logf, state_mlstm_C, state_mlstm_n, state_mlstm_m, state_gdn_S, state_gdn_conv, page_table, c_prompt, c_sample, w_ada, b_ada, ln_g, ln_b, ffn_w_gate, ffn_w_up, ffn_w_down, w_in, b_in, mlstm_norm_g, gdn_conv_w, gdn_A_log, gdn_dt_bias, gdn_norm_g, w_out):
    bp, tp, d = x_prompt.shape
    bs, ts, _ = x_sample.shape
    depth = w_ada.shape[0]
    hm, hf, hg = state_mlstm_C.shape[2], cache_fox_k.shape[3], state_gdn_S.shape[2]
    dims = (hm, hf, hg)
    dm, df, dg = hm * HEAD_DIM, hf * HEAD_DIM, hg * HEAD_DIM
    nphys, page = cache_fox_k.shape[1], cache_fox_k.shape[2]
    alpha = (2 * depth) ** 0.25
    assert 2 * hm + hf + 2 * hg <= GATE_ROWS and (page * hf) % LANES == 0 and LANES % hf == 0

    nrow = -(-(bp + bs) // 16) * 16
    c_all = jnp.concatenate([c_prompt, c_sample, jnp.zeros((nrow - bp - bs, d), F32)], axis=0)
    mod = _ada(c_all, w_ada, b_ada).reshape(depth, nrow, 3 * N_SUB, d)
    mod = jnp.transpose(mod, (0, 2, 1, 3))
    mod_p = mod[:, :, :bp, None, :]
    mod_s = jnp.repeat(mod[:, :, bp:bp + bs], ts, axis=2)[:, :, None, :, :]

    w_in_r, b_in_r, gate_blk = _regroup_w_in(w_in, b_in, dims)
    a_col, dt_col = _gate_param_cols(gdn_A_log, gdn_dt_bias, dims)
    ln_g4 = ln_g[:, :, None, :]
    ln_b4 = ln_b[:, :, None, :]
    norm_m = mlstm_norm_g[:, :, None, :]
    norm_g = gdn_norm_g[:, None, :]

    rpp = (page * hf) // LANES
    logf_hs = jnp.transpose(cache_fox_logf, (0, 1, 3, 2)).reshape(depth * nphys, hf * page)
    wpool, tpool = _fox_pool(logf_hs, hf, page)
    wpool = wpool.reshape(depth * nphys, rpp, LANES)
    tpool = tpool.reshape(depth * nphys, 1, LANES)
    kpool = cache_fox_k.reshape(depth, nphys, page * hf, HEAD_DIM)
    vpool = cache_fox_v.reshape(depth, nphys, page * hf, HEAD_DIM)

    o_f = 4 * dm
    o_g = o_f + 3 * df

    def layer_step(l, x, h, mod_x, nb, t, mlstm_state, gdn_state, sample):
        m_rows = nb * t
        grp = m_rows if sample else t
        a = _gateup(h, ffn_w_gate, ffn_w_up, l, 0)
        x, h = _proj_ln(a, ffn_w_down, (l, 0), x, mod_x, l, 0, (l, 1), ln_g4, ln_b4, grp, alpha, MACARON_W)
        proj = _win(h, w_in_r, b_in_r, l)
        lc = math.gcd(t, CHUNK)
        if sample:
            gc, grc, grf = _gates(proj, gate_blk, a_col, dt_col, l, 1, m_rows, lc, t, dims)
            grc = grc.reshape(nb, t // lc, 2 * GATE_ROWS, lc)
        else:
            gc, grc, grf = _gates(proj, gate_blk, a_col, dt_col, l, nb, t, lc, t, dims)
        c0, n0, m0 = mlstm_state
        proj3 = proj.reshape(nb, t, proj.shape[-1])
        gc3 = gc.reshape(nb, t, LANES)
        bpb = 1 if sample else nb
        hmx, c1, n1, m1 = _mlstm(proj3, gc3, grc, norm_m, c0, n0[:, :, None, :], m0[:, :, None, None],
                                 l, nb, t, bpb, dims)
        s0, b0 = gdn_state
        hgx, s1, b1 = _gdn(proj3, gc3, grc, gdn_conv_w, norm_g, s0, b0, l, nb, t, bpb, dims)
        hmx = hmx.reshape(m_rows, dm)
        hgx = hgx.reshape(m_rows, dg)
        fk = proj[:, o_f + df:o_f + 2 * df]
        fv = proj[:, o_f + 2 * df:o_f + 3 * df]
        logf = gc[:, GATE_ROWS + 2 * hm:GATE_ROWS + 2 * hm + hf]
        if sample:
            nr = t * hf
            q2 = proj[:, o_f:o_f + df].reshape(nb, nr, HEAD_DIM)
            cum = gc[:, 2 * hm:2 * hm + hf].reshape(nb, nr)
            hfx = _fox_sample(page_table, q2, cum[:, None, :], fk.reshape(nb, nr, HEAD_DIM),
                              fv.reshape(nb, nr, HEAD_DIM), kpool, vpool, wpool, tpool, l, hf)
            hfx = hfx.reshape(m_rows, df)
        else:
            hfx = _fox_prompt(proj, grf, nb, t, dims)
        mixed = jnp.concatenate([hmx.astype(BF16), hfx, hgx.astype(BF16)], axis=-1)
        x, h = _proj_ln(mixed, w_out, (l,), x, mod_x, l, 1, (l, 2), ln_g4, ln_b4, grp, alpha, 1.0)
        a = _gateup(h, ffn_w_gate, ffn_w_up, l, 1)
        nxt = (l + 1, 0) if l + 1 < depth else None
        x, h = _proj_ln(a, ffn_w_down, (l, 1), x, mod_x, l, 2, nxt, ln_g4, ln_b4, grp, alpha, MACARON_W)
        new_state = (fk.reshape(nb, t, hf, HEAD_DIM), fv.reshape(nb, t, hf, HEAD_DIM), logf.reshape(nb, t, hf),
                     c1, n1[:, :, 0, :], m1[:, :, 0, 0], s1, b1)
        return x, h, new_state

    xp = x_prompt.reshape(bp * tp, d)
    xs = x_sample.reshape(bs * ts, d)
    hp = _modulate(xp, mod_p, 0, 0, tp)
    hs = _modulate(xs, mod_s, 0, 0, bs * ts)
    zero_mlstm = (jnp.zeros((bp, hm, HEAD_DIM, HEAD_DIM), F32), jnp.zeros((bp, hm, HEAD_DIM), F32),
                  jnp.zeros((bp, hm), F32))
    zero_gdn = (jnp.zeros((bp, hg, HEAD_DIM, HEAD_DIM), F32), jnp.zeros((bp, 3, 3 * dg), F32))
    st_p, st_s = [], []
    for l in range(depth):
        xp, hp, sp = layer_step(l, xp, hp, mod_p, bp, tp, zero_mlstm, zero_gdn, False)
        xs, hs, ss = layer_step(l, xs, hs, mod_s, bs, ts,
                                (state_mlstm_C[l], state_mlstm_n[l], state_mlstm_m[l]),
                                (state_gdn_S[l], state_gdn_conv[l]), True)
        st_p.append(sp)
        st_s.append(ss)
    pn = [jnp.stack(a) for a in zip(*st_p)]
    sn = [jnp.stack(a) for a in zip(*st_s)]
    return (xp.reshape(bp, tp, d), xs.reshape(bs, ts, d), pn[0], pn[1], pn[2], pn[3], pn[4], pn[5], pn[6], pn[7],
            sn[0], sn[1], sn[2], sn[3], sn[4], sn[5], sn[6], sn[7])
```

```python
import functools
import math

import jax
import jax.numpy as jnp
from jax import lax
from jax.experimental import pallas as pl
from jax.experimental.pallas import tpu as pltpu

F32 = jnp.float32
BF16 = jnp.bfloat16

HEAD_DIM = 128
LANES = 128
SUBLANES = 8
CHUNK = 64
N_SUB = 3
LN_EPS = 1e-5
NORM_EPS = 1e-6
MACARON_W = 0.5
GATE_ROWS = 32
VMEM_LIMIT_BYTES = 56 * 1024 * 1024
HI = lax.Precision.HIGHEST
NEG_INF = float("-inf")


def _params(*sem):
    return pltpu.CompilerParams(dimension_semantics=sem, vmem_limit_bytes=VMEM_LIMIT_BYTES)


def _dot(a, b, precision=None):
    return jnp.dot(a, b, preferred_element_type=F32, precision=precision)


def _dot_nt(a, b, precision=None):
    return lax.dot_general(a, b, (((1,), (1,)), ((), ())), preferred_element_type=F32, precision=precision)


def _dot_tn(a, b, precision=None):
    return lax.dot_general(a, b, (((0,), (0,)), ((), ())), preferred_element_type=F32, precision=precision)


def _sigmoid(x):
    return 1.0 / (1.0 + jnp.exp(-x))


def _silu(x):
    return x * _sigmoid(x)


def _softplus(x):
    return jnp.maximum(x, 0.0) + jnp.log1p(jnp.exp(-jnp.abs(x)))


def _log_sigmoid(x):
    return -_softplus(-x)


def _pick(m, candidates):
    for c in candidates:
        if m % c == 0:
            return c
    return m


def _ada_kernel(c_ref, w_ref, b_ref, o_ref):
    c = c_ref[...]
    sc = _silu(c).astype(BF16)
    o_ref[...] = _dot(sc, w_ref[...].astype(BF16)) + b_ref[...]


def _ada(c_all, w_ada, b_ada):
    depth, d, n = w_ada.shape
    rows = c_all.shape[0]
    bn = _pick(n, (1024, 512, 256, 128))
    return pl.pallas_call(
        _ada_kernel,
        grid=(depth, n // bn),
        in_specs=[pl.BlockSpec((rows, d), lambda l, j: (0, 0)),
                  pl.BlockSpec((None, d, bn), lambda l, j: (l, 0, j)),
                  pl.BlockSpec((None, 1, bn), lambda l, j: (l, 0, j))],
        out_specs=pl.BlockSpec((None, rows, bn), lambda l, j: (l, 0, j)),
        out_shape=jax.ShapeDtypeStruct((depth, rows, n), F32),
        name="ada",
        compiler_params=_params("arbitrary", "arbitrary"),
    )(c_all, w_ada, b_ada.reshape(depth, 1, n))


def _modulate_kernel(x_ref, sc_ref, sh_ref, o_ref):
    o_ref[...] = (x_ref[...] * (1.0 + sc_ref[...]) + sh_ref[...]).astype(BF16)


def _mod_spec(mod, layer, slot, rows_per_group, bm):
    r, d = mod.shape[3], mod.shape[4]
    if r == 1:
        return pl.BlockSpec((None, None, None, 1, d),
                            lambda m, *_: (layer, slot, (m * bm) // rows_per_group, 0, 0))
    return pl.BlockSpec((None, None, None, bm, d), lambda m, *_: (layer, slot, 0, m, 0))


def _modulate(x, mod, layer, j, rows_per_group):
    m, d = x.shape
    bm = _pick(rows_per_group, (512, 256, 128, 64))
    return pl.pallas_call(
        _modulate_kernel,
        grid=(m // bm,),
        in_specs=[pl.BlockSpec((bm, d), lambda i: (i, 0)),
                  _mod_spec(mod, layer, 3 * j + 1, rows_per_group, bm),
                  _mod_spec(mod, layer, 3 * j + 0, rows_per_group, bm)],
        out_specs=pl.BlockSpec((bm, d), lambda i: (i, 0)),
        out_shape=jax.ShapeDtypeStruct((m, d), BF16),
        name="modulate",
        compiler_params=_params("arbitrary"),
    )(x, mod, mod)


def _gateup_kernel(h_ref, hs_ref, wg_ref, wu_ref, o_ref, os_ref, wg_s, wu_s, *, last):
    i = pl.program_id(1)

    @pl.when(i == 0)
    def _():
        wg_s[...] = wg_ref[...].astype(BF16)
        wu_s[...] = wu_ref[...].astype(BF16)

    def swiglu_half(h):
        g = _dot(h, wg_s[...])
        u = _dot(h, wu_s[...])
        return (_silu(g) * u).astype(BF16)

    o_ref[...] = swiglu_half(h_ref[...])

    @pl.when(i == last)
    def _():
        os_ref[...] = swiglu_half(hs_ref[...])


def _gateup(h, hs, w_gate, w_up, layer, j):
    m, d = h.shape
    ms = hs.shape[0]
    f = w_gate.shape[-1]
    bm = _pick(m, (1024, 512, 256, 128, 64))
    bn = _pick(f, (512, 256, 128))
    wspec = pl.BlockSpec((None, None, d, bn), lambda n, i: (layer, j, 0, n))
    return pl.pallas_call(
        functools.partial(_gateup_kernel, last=m // bm - 1),
        grid=(f // bn, m // bm),
        in_specs=[pl.BlockSpec((bm, d), lambda n, i: (i, 0)), pl.BlockSpec((ms, d), lambda n, i: (0, 0)),
                  wspec, wspec],
        out_specs=[pl.BlockSpec((bm, bn), lambda n, i: (i, n)), pl.BlockSpec((ms, bn), lambda n, i: (0, n))],
        out_shape=[jax.ShapeDtypeStruct((m, f), BF16), jax.ShapeDtypeStruct((ms, f), BF16)],
        scratch_shapes=[pltpu.VMEM((d, bn), BF16), pltpu.VMEM((d, bn), BF16)],
        name="gateup",
        compiler_params=_params("arbitrary", "arbitrary"),
    )(h, hs, w_gate, w_up)


def _proj_ln_kernel(*refs, widths, nk, nm, bk, alpha, coef, emit_h):
    npart = len(widths)
    nin = npart + 2 + (2 if emit_h else 0)
    nout = 2 if emit_h else 1
    w_ref, lng_ref, lnb_ref = refs[0:3]
    ins = [refs[3 + s * nin:3 + (s + 1) * nin] for s in range(2)]
    outs = [refs[3 + 2 * nin + s * nout:3 + 2 * nin + (s + 1) * nout] for s in range(2)]
    w_s = refs[3 + 2 * nin + 2 * nout]
    i = pl.program_id(0)

    @pl.when(i < nk)
    def _():
        w_s[pl.ds(pl.multiple_of(i * bk, bk), bk), :] = w_ref[...].astype(BF16)

    def stream(s):
        parts = ins[s][:npart]
        x_ref, gate_ref = ins[s][npart], ins[s][npart + 1]
        acc = None
        off = 0
        for a_ref, wd in zip(parts, widths):
            term = _dot(a_ref[...], w_s[off:off + wd, :])
            acc = term if acc is None else acc + term
            off += wd
        y = alpha * x_ref[...] + (coef * (1.0 + gate_ref[...])) * acc
        mu = jnp.mean(y, axis=-1, keepdims=True)
        yc = y - mu
        var = jnp.mean(yc * yc, axis=-1, keepdims=True)
        xn = yc * lax.rsqrt(var + LN_EPS) * lng_ref[...] + lnb_ref[...]
        outs[s][0][...] = xn
        if emit_h:
            sc_ref, sh_ref = ins[s][npart + 2], ins[s][npart + 3]
            outs[s][1][...] = (xn * (1.0 + sc_ref[...]) + sh_ref[...]).astype(BF16)

    @pl.when((i >= nk) & (i < nk + nm))
    def _():
        stream(0)

    @pl.when(i == nk + nm)
    def _():
        stream(1)


def _proj_ln(parts_p, parts_s, w, w_prefix, x_p, x_s, mod_p, mod_s, layer, j, nxt, ln_g, ln_b, rows_per_group,
             alpha, coef):
    widths = tuple(a.shape[1] for a in parts_p)
    k = sum(widths)
    m, d = x_p.shape
    ms = x_s.shape[0]
    bm = _pick(rows_per_group, (256, 128, 64))
    bk = _pick(k, (512, 256, 128))
    nk = k // bk
    nm = m // bm
    emit_h = nxt is not None
    npre = len(w_prefix)

    def row(i):
        return jnp.clip(i - nk, 0, nm - 1)

    def modspec_p(lyr, slot):
        return pl.BlockSpec((None, None, None, 1, d),
                            lambda i: (lyr, slot, (row(i) * bm) // rows_per_group, 0, 0))

    def modspec_s(lyr, slot):
        return pl.BlockSpec((None, None, None, ms, d), lambda i: (lyr, slot, 0, 0, 0))

    in_specs = [pl.BlockSpec((None,) * npre + (bk, d), lambda i: tuple(w_prefix) + (jnp.minimum(i, nk - 1), 0)),
                pl.BlockSpec((None, None, 1, d), lambda i: (layer, j, 0, 0)),
                pl.BlockSpec((None, None, 1, d), lambda i: (layer, j, 0, 0))]
    args = [w, ln_g, ln_b]
    in_specs += [pl.BlockSpec((bm, wd), lambda i: (row(i), 0)) for wd in widths]
    in_specs += [pl.BlockSpec((bm, d), lambda i: (row(i), 0)), modspec_p(layer, 3 * j + 2)]
    args += list(parts_p) + [x_p, mod_p]
    if emit_h:
        in_specs += [modspec_p(nxt[0], 3 * nxt[1] + 1), modspec_p(nxt[0], 3 * nxt[1] + 0)]
        args += [mod_p, mod_p]
    in_specs += [pl.BlockSpec((ms, wd), lambda i: (0, 0)) for wd in widths]
    in_specs += [pl.BlockSpec((ms, d), lambda i: (0, 0)), modspec_s(layer, 3 * j + 2)]
    args += list(parts_s) + [x_s, mod_s]
    if emit_h:
        in_specs += [modspec_s(nxt[0], 3 * nxt[1] + 1), modspec_s(nxt[0], 3 * nxt[1] + 0)]
        args += [mod_s, mod_s]
    out_specs = [pl.BlockSpec((bm, d), lambda i: (row(i), 0))]
    out_shape = [jax.ShapeDtypeStruct((m, d), F32)]
    if emit_h:
        out_specs.append(pl.BlockSpec((bm, d), lambda i: (row(i), 0)))
        out_shape.append(jax.ShapeDtypeStruct((m, d), BF16))
    out_specs.append(pl.BlockSpec((ms, d), lambda i: (0, 0)))
    out_shape.append(jax.ShapeDtypeStruct((ms, d), F32))
    if emit_h:
        out_specs.append(pl.BlockSpec((ms, d), lambda i: (0, 0)))
        out_shape.append(jax.ShapeDtypeStruct((ms, d), BF16))
    out = pl.pallas_call(
        functools.partial(_proj_ln_kernel, widths=widths, nk=nk, nm=nm, bk=bk, alpha=alpha, coef=coef,
                          emit_h=emit_h),
        grid=(nk + nm + 1,),
        in_specs=in_specs,
        out_specs=out_specs,
        out_shape=out_shape,
        scratch_shapes=[pltpu.VMEM((k, d), BF16)],
        name="proj_ln",
        compiler_params=_params("arbitrary"),
    )(*args)
    if emit_h:
        return (out[0], out[1]), (out[2], out[3])
    return (out[0], None), (out[1], None)


def _win_kernel(h_ref, w_ref, b_ref, o_ref):
    o_ref[...] = _dot_nt(h_ref[...], w_ref[...]) + b_ref[...]


def _win(h, w, b, layer):
    m, d = h.shape
    n = w.shape[1]
    bm = _pick(m, (1024, 512, 256, 128, 64))
    bn = _pick(n, (512, 256, 128))
    return pl.pallas_call(
        _win_kernel,
        grid=(m // bm, n // bn),
        in_specs=[pl.BlockSpec((bm, d), lambda i, c: (i, 0)),
                  pl.BlockSpec((None, bn, d), lambda i, c: (layer, c, 0)),
                  pl.BlockSpec((None, 1, bn), lambda i, c: (layer, 0, c))],
        out_specs=pl.BlockSpec((bm, bn), lambda i, c: (i, c)),
        out_shape=jax.ShapeDtypeStruct((m, n), F32),
        name="w_in",
        compiler_params=_params("arbitrary", "arbitrary"),
    )(h, w, b)


def _gates_kernel(x_ref, a_ref, dt_ref, gc_ref, grc_ref, grf_ref, carry_ref, *, hm, hf, hg, lc, seg, tb):
    t = pl.program_id(1)

    @pl.when(t == 0)
    def _():
        carry_ref[...] = jnp.zeros_like(carry_ref)

    z = jnp.transpose(x_ref[...])[:GATE_ROWS]
    ch = lax.broadcasted_iota(jnp.int32, (GATE_ROWS, 1), 0)
    is_mi = ch < hm
    is_mf = (ch >= hm) & (ch < 2 * hm)
    is_ff = (ch >= 2 * hm) & (ch < 2 * hm + hf)
    is_ga = (ch >= 2 * hm + hf) & (ch < 2 * hm + hf + hg)
    is_gb = (ch >= 2 * hm + hf + hg) & (ch < 2 * hm + hf + 2 * hg)
    ls = _log_sigmoid(z)
    gg = -jnp.exp(a_ref[...]) * _softplus(z + dt_ref[...])
    sg = _sigmoid(z)
    s_i = lax.broadcasted_iota(jnp.int32, (tb, tb), 0)
    t_i = lax.broadcasted_iota(jnp.int32, (tb, tb), 1)
    m_loc = jnp.where((s_i <= t_i) & ((s_i // lc) == (t_i // lc)), 1.0, 0.0).astype(F32)
    m_seg = jnp.where((s_i <= t_i) & ((s_i // seg) == (t_i // seg)), 1.0, 0.0).astype(F32)
    v_loc = jnp.where(is_mf, ls, jnp.where(is_ga, gg, 0.0))
    v_seg = jnp.where(is_ff, ls, 0.0)
    cum_loc = _dot(v_loc, m_loc, HI)
    cum_seg = _dot(v_seg, m_seg, HI) + carry_ref[...]
    if seg > tb:
        carry_ref[...] = cum_seg[:, tb - 1:tb]
    bank0 = jnp.where(is_mi, z, jnp.where(is_mf | is_ga, cum_loc,
                                          jnp.where(is_ff, cum_seg, jnp.where(is_gb, sg, 0.0))))
    bank1 = jnp.where(is_ff, ls, 0.0)
    rows = jnp.concatenate([bank0, bank1], axis=0)
    full = jnp.concatenate([rows, jnp.zeros((LANES - 2 * GATE_ROWS, tb), F32)], axis=0)
    gc_ref[...] = jnp.transpose(full)
    for c in range(tb // lc):
        grc_ref[c] = rows[:, c * lc:(c + 1) * lc]
    grf_ref[...] = bank0[2 * hm:2 * hm + hf]


def _gates(proj, gate_blk, a_col, dt_col, layer, nb, t, lc, seg, dims):
    hm, hf, hg = dims
    tb = _pick(t, (512, 256, 128, 64))
    nt = t // tb
    return pl.pallas_call(
        functools.partial(_gates_kernel, hm=hm, hf=hf, hg=hg, lc=lc, seg=seg, tb=tb),
        grid=(nb, nt),
        in_specs=[pl.BlockSpec((tb, LANES), lambda b, i: (b * nt + i, gate_blk)),
                  pl.BlockSpec((None, GATE_ROWS, 1), lambda b, i: (layer, 0, 0)),
                  pl.BlockSpec((None, GATE_ROWS, 1), lambda b, i: (layer, 0, 0))],
        out_specs=[pl.BlockSpec((tb, LANES), lambda b, i: (b * nt + i, 0)),
                   pl.BlockSpec((None, tb // lc, 2 * GATE_ROWS, lc), lambda b, i: (b, i, 0, 0)),
                   pl.BlockSpec((None, hf, tb), lambda b, i: (b, 0, i))],
        out_shape=[jax.ShapeDtypeStruct((nb * t, LANES), F32),
                   jax.ShapeDtypeStruct((nb, t // lc, 2 * GATE_ROWS, lc), F32),
                   jax.ShapeDtypeStruct((nb, hf, t), F32)],
        scratch_shapes=[pltpu.VMEM((GATE_ROWS, 1), F32)],
        name="gates",
        compiler_params=_params("arbitrary", "arbitrary"),
    )(proj, a_col, dt_col)


def _mlstm_kernel(q_ref, k_ref, v_ref, o_ref, gc_ref, gr_ref, ng_ref, c0_ref, n0_ref, m0_ref,
                  h_ref, c_ref, n_ref, m_ref, *, hm, lc, bpb, mxu_dtype):
    @pl.when(pl.program_id(1) == 0)
    def _():
        c_ref[...] = c0_ref[...]
        n_ref[...] = n0_ref[...]
        m_ref[...] = m0_ref[...]

    r_i = lax.broadcasted_iota(jnp.int32, (lc, lc), 0)
    c_i = lax.broadcasted_iota(jnp.int32, (lc, lc), 1)
    causal = c_i <= r_i
    scale = HEAD_DIM ** -0.5
    chains = [(b, h) for b in range(bpb) for h in range(hm)]
    sls = [slice(h * HEAD_DIM, (h + 1) * HEAD_DIM) for _, h in chains]
    first = []
    for (b, h), sl in zip(chains, sls):
        q = q_ref[b, :, sl]
        k = k_ref[b, :, sl] * scale
        qx = q.astype(mxu_dtype)
        vx = v_ref[b, :, sl].astype(mxu_dtype)
        first.append(dict(q=q, k=k, vx=vx, qk=_dot_nt(qx, k.astype(mxu_dtype)),
                          qc=_dot(qx, c_ref[b, h].astype(mxu_dtype))))
    gate = []
    for b, h in chains:
        gc = gc_ref[b]
        gr = gr_ref[b]
        li_col = gc[:, h:h + 1]
        b_col = gc[:, hm + h:hm + h + 1]
        li_row = gr[h:h + 1, :]
        b_row = gr[hm + h:hm + h + 1, :]
        mst = m_ref[b, h]
        dmat = jnp.where(causal, b_col - b_row + li_row, NEG_INF)
        inter = b_col + mst
        b_last = b_col[lc - 1:lc, :]
        d_last = b_last - b_col + li_col
        gate.append(dict(dmat=dmat, inter=inter, d_last=d_last, carry=b_last + mst,
                         dmax=jnp.max(dmat, axis=-1, keepdims=True), lmax=jnp.max(d_last, axis=0, keepdims=True)))
    for g in gate:
        m_row = jnp.maximum(g["inter"], g["dmax"])
        m_new = jnp.maximum(g["carry"], g["lmax"])
        g.update(m_row=m_row, w_intra=jnp.exp(g["dmat"] - m_row), w_inter=jnp.exp(g["inter"] - m_row),
                 m_new=m_new, w_k=jnp.exp(g["d_last"] - m_new), decay=jnp.exp(g["carry"] - m_new))
    outs = []
    for (b, h), g, f in zip(chains, gate, first):
        cst = c_ref[b, h]
        nst = n_ref[b, h]
        s = f["qk"] * g["w_intra"]
        num = _dot(s.astype(mxu_dtype), f["vx"]) + g["w_inter"] * f["qc"]
        den = jnp.sum(s, axis=-1, keepdims=True) + g["w_inter"] * jnp.sum(f["q"] * nst, axis=-1, keepdims=True)
        outs.append(num / jnp.maximum(jnp.abs(den), jnp.exp(-g["m_row"])))
        kw = f["k"] * g["w_k"]
        c_ref[b, h] = g["decay"] * cst + _dot_tn(kw.astype(mxu_dtype), f["vx"])
        n_ref[b, h] = g["decay"] * nst + jnp.sum(kw, axis=0, keepdims=True)
        m_ref[b, h] = g["m_new"]
    for (b, h), sl, hh in zip(chains, sls, outs):
        mu = jnp.mean(hh, axis=-1, keepdims=True)
        hc = hh - mu
        var = jnp.mean(hc * hc, axis=-1, keepdims=True)
        hn = hc * lax.rsqrt(var + LN_EPS) * ng_ref[h]
        h_ref[b, :, sl] = (hn * _sigmoid(o_ref[b, :, sl])).astype(h_ref.dtype)


def _mlstm(proj, gc, grc, norm_g, c0, n0, m0, layer, nb, t, bpb, dims):
    hm = dims[0]
    dm = hm * HEAD_DIM
    lc = math.gcd(t, CHUNK)
    nc = t // lc
    mxu_dtype = BF16 if lc >= 16 else F32

    def col(j):
        return pl.BlockSpec((bpb, lc, dm), lambda b, c: (b, c, j))

    st4 = pl.BlockSpec((bpb, hm, HEAD_DIM, HEAD_DIM), lambda b, c: (b, 0, 0, 0))
    stn = pl.BlockSpec((bpb, hm, 1, HEAD_DIM), lambda b, c: (b, 0, 0, 0))
    stm = pl.BlockSpec((bpb, hm, 1, 1), lambda b, c: (b, 0, 0, 0))
    return pl.pallas_call(
        functools.partial(_mlstm_kernel, hm=hm, lc=lc, bpb=bpb, mxu_dtype=mxu_dtype),
        grid=(nb // bpb, nc),
        in_specs=[col(0), col(1), col(2), col(3),
                  pl.BlockSpec((bpb, lc, LANES), lambda b, c: (b, c, 0)),
                  pl.BlockSpec((bpb, None, 2 * GATE_ROWS, lc), lambda b, c: (b, c, 0, 0)),
                  pl.BlockSpec((None, hm, 1, HEAD_DIM), lambda b, c: (layer, 0, 0, 0)),
                  st4, stn, stm],
        out_specs=[pl.BlockSpec((bpb, lc, dm), lambda b, c: (b, c, 0)), st4, stn, stm],
        out_shape=[jax.ShapeDtypeStruct((nb, t, dm), mxu_dtype),
                   jax.ShapeDtypeStruct((nb, hm, HEAD_DIM, HEAD_DIM), F32),
                   jax.ShapeDtypeStruct((nb, hm, 1, HEAD_DIM), F32),
                   jax.ShapeDtypeStruct((nb, hm, 1, 1), F32)],
        name="mlstm",
        compiler_params=_params("arbitrary", "arbitrary"),
    )(proj, proj, proj, proj, gc, grc, norm_g, c0, n0, m0)


def _split(a):
    hi = a.astype(BF16)
    return hi, (a - hi.astype(F32)).astype(BF16)


def _mm3(a, b):
    return _dot(a[0], b[0]) + (_dot(a[0], b[1]) + _dot(a[1], b[0]))


def _mm3_nt(a, b):
    return _dot_nt(a[0], b[0]) + (_dot_nt(a[0], b[1]) + _dot_nt(a[1], b[0]))


def _unit_lower_solve(a, rhs, lc, split_bf16):
    r_i = lax.broadcasted_iota(jnp.int32, (lc, lc), 0)
    c_i = lax.broadcasted_iota(jnp.int32, (lc, lc), 1)
    eye = jnp.where(r_i == c_i, 1.0, 0.0).astype(F32)
    if split_bf16:
        prep, mm = _split, _mm3
    else:
        prep, mm = (lambda x: x), (lambda x, y: _dot(x, y, HI))
    npows = [-x for x in a]
    invs = [eye + n for n in npows]
    width = 2
    while width < lc:
        ns = [prep(n) for n in npows]
        npows = [mm(n, n) for n in ns]
        invs = [inv + mm(prep(inv), prep(n)) for inv, n in zip(invs, npows)]
        width *= 2
    return [mm(prep(inv), prep(r)) for inv, r in zip(invs, rhs)]


def _gdn_prep_kernel(q_ref, k_ref, v_ref, pq_ref, pk_ref, pv_ref, b0_ref, gc_ref, gr_ref, cw_ref,
                     w_ref, u_ref, qg_ref, kd_ref, qk_ref, buf_ref, xp_ref, *, hm, hf, hg, lc, cb):
    dg = hg * HEAD_DIM
    keep = SUBLANES
    tb = cb * lc
    split_bf16 = lc >= 16
    prev = jnp.concatenate([pq_ref[keep - 3:keep, :], pk_ref[keep - 3:keep, :], pv_ref[keep - 3:keep, :]], axis=1)
    xp_ref[keep - 3:keep, :] = jnp.where(pl.program_id(1) == 0, b0_ref[...], prev)
    xp_ref[keep:keep + tb, 0:dg] = q_ref[...]
    xp_ref[keep:keep + tb, dg:2 * dg] = k_ref[...]
    xp_ref[keep:keep + tb, 2 * dg:3 * dg] = v_ref[...]
    conv = xp_ref[keep:keep + tb, :] * cw_ref[3:4, :]
    for j in range(3):
        conv = conv + xp_ref[keep - 3 + j:keep - 3 + j + tb, :] * cw_ref[j:j + 1, :]
    buf_ref[...] = xp_ref[keep + tb - 3:keep + tb, :]
    act = _silu(conv)

    r_i = lax.broadcasted_iota(jnp.int32, (lc, lc), 0)
    c_i = lax.broadcasted_iota(jnp.int32, (lc, lc), 1)
    incl = c_i <= r_i
    strict = c_i < r_i
    ch_g = 2 * hm + hf
    ch_b = ch_g + hg
    scale = HEAD_DIM ** -0.5
    chains = [(c, h) for c in range(cb) for h in range(hg)]
    amats, rhss, qks = [], [], []
    for c, h in chains:
        rows = slice(c * lc, (c + 1) * lc)
        sl = slice(h * HEAD_DIM, (h + 1) * HEAD_DIM)
        gc = gc_ref[rows, :]
        gr = gr_ref[c]
        cq = act[rows, h * HEAD_DIM:(h + 1) * HEAD_DIM]
        ck = act[rows, dg + h * HEAD_DIM:dg + (h + 1) * HEAD_DIM]
        cv = act[rows, 2 * dg + h * HEAD_DIM:2 * dg + (h + 1) * HEAD_DIM]
        qn = cq * lax.rsqrt(jnp.sum(cq * cq, axis=-1, keepdims=True) + NORM_EPS) * scale
        kn = ck * lax.rsqrt(jnp.sum(ck * ck, axis=-1, keepdims=True) + NORM_EPS)
        g_col = gc[:, ch_g + h:ch_g + h + 1]
        g_row = gr[ch_g + h:ch_g + h + 1, :]
        beta = gc[:, ch_b + h:ch_b + h + 1]
        decay = jnp.exp(jnp.where(incl, g_col - g_row, NEG_INF))
        if split_bf16:
            ks = _split(kn)
            kk = _mm3_nt(ks, ks)
            qk = _dot_nt(qn.astype(BF16), ks[0])
        else:
            kk = _dot_nt(kn, kn, HI)
            qk = _dot_nt(qn, kn)
        amats.append(jnp.where(strict, beta * kk * decay, 0.0))
        eg = jnp.exp(g_col)
        rhss.append(jnp.concatenate([kn * (beta * eg), cv * beta], axis=-1))
        g_last = g_col[lc - 1:lc, :]
        qg_ref[rows, sl] = (qn * eg).astype(qg_ref.dtype)
        kd_ref[rows, sl] = (kn * jnp.exp(g_last - g_col)).astype(kd_ref.dtype)
        qks.append((qk * decay).astype(qk_ref.dtype))
        if h == hg - 1:
            qk_ref[rows, :] = jnp.concatenate(qks[-hg:], axis=1)
    sols = _unit_lower_solve(amats, rhss, lc, split_bf16)
    for (c, h), sol in zip(chains, sols):
        rows = slice(c * lc, (c + 1) * lc)
        sl = slice(h * HEAD_DIM, (h + 1) * HEAD_DIM)
        w_ref[rows, sl] = sol[:, :HEAD_DIM].astype(w_ref.dtype)
        u_ref[rows, sl] = sol[:, HEAD_DIM:]


def _gdn_scan_kernel(w_ref, u_ref, qg_ref, kd_ref, qk_ref, z_ref, gc_ref, ng_ref, s0_ref,
                     h_ref, s_ref, *, hm, hf, hg, lc, bpb):
    @pl.when(pl.program_id(1) == 0)
    def _():
        s_ref[...] = s0_ref[...]

    ch_g = 2 * hm + hf
    xd = w_ref.dtype
    chains = [(b, h) for b in range(bpb) for h in range(hg)]
    sls = [slice(h * HEAD_DIM, (h + 1) * HEAD_DIM) for _, h in chains]
    first = []
    for (b, h), sl in zip(chains, sls):
        sx = s_ref[b, h].astype(xd)
        first.append((_dot(w_ref[b, :, sl], sx), _dot(qg_ref[b, :, sl], sx)))
    outs = []
    for (b, h), sl, (ws, qs) in zip(chains, sls, first):
        ux = (u_ref[b, :, sl] - ws).astype(xd)
        outs.append(qs + _dot(qk_ref[b][:, h * lc:(h + 1) * lc], ux))
        g_last = gc_ref[b, lc - SUBLANES:lc, :][SUBLANES - 1:SUBLANES, ch_g + h:ch_g + h + 1]
        s_ref[b, h] = jnp.exp(g_last) * s_ref[b, h] + _dot_tn(kd_ref[b, :, sl], ux)
    for (b, h), sl, o in zip(chains, sls, outs):
        on = o * lax.rsqrt(jnp.mean(o * o, axis=-1, keepdims=True) + NORM_EPS) * ng_ref[...]
        h_ref[b, :, sl] = (on * _silu(z_ref[b, :, sl])).astype(h_ref.dtype)


def _gdn(proj, gc, grc, conv_w, norm_g, s0, b0, layer, nb, t, bpb, dims):
    hm, hf, hg = dims
    dg = hg * HEAD_DIM
    lc = math.gcd(t, CHUNK)
    nc = t // lc
    xd = BF16 if lc >= 16 else F32
    base = (4 * hm * HEAD_DIM + 3 * hf * HEAD_DIM) // dg
    cb = _pick(nc, (2, 1))
    tb = cb * lc
    nblk = t // tb
    m = nb * t
    n_all = proj.shape[-1]
    proj2 = proj.reshape(m, n_all)

    def col(j):
        return pl.BlockSpec((tb, dg), lambda b, i: (b * nblk + i, base + j))

    def prev(j):
        return pl.BlockSpec((SUBLANES, dg),
                            lambda b, i: (jnp.maximum((b * t + i * tb) // SUBLANES - 1, 0), base + j))

    def rowblk(width):
        return pl.BlockSpec((tb, width), lambda b, i: (b * nblk + i, 0))

    w, u0, qg, kd, qk, b1 = pl.pallas_call(
        functools.partial(_gdn_prep_kernel, hm=hm, hf=hf, hg=hg, lc=lc, cb=cb),
        grid=(nb, nblk),
        in_specs=[col(0), col(1), col(2), prev(0), prev(1), prev(2),
                  pl.BlockSpec((None, 3, 3 * dg), lambda b, i: (b, 0, 0)),
                  rowblk(LANES),
                  pl.BlockSpec((None, cb, 2 * GATE_ROWS, lc), lambda b, i: (b, i, 0, 0)),
                  pl.BlockSpec((None, 4, 3 * dg), lambda b, i: (layer, 0, 0))],
        out_specs=[rowblk(dg), rowblk(dg), rowblk(dg), rowblk(dg), rowblk(hg * lc),
                   pl.BlockSpec((None, 3, 3 * dg), lambda b, i: (b, 0, 0))],
        out_shape=[jax.ShapeDtypeStruct((m, dg), xd), jax.ShapeDtypeStruct((m, dg), F32),
                   jax.ShapeDtypeStruct((m, dg), xd), jax.ShapeDtypeStruct((m, dg), xd),
                   jax.ShapeDtypeStruct((m, hg * lc), xd), jax.ShapeDtypeStruct((nb, 3, 3 * dg), F32)],
        scratch_shapes=[pltpu.VMEM((tb + SUBLANES, 3 * dg), F32)],
        name="gdn_prep",
        compiler_params=_params("arbitrary", "arbitrary"),
    )(proj2, proj2, proj2, proj2, proj2, proj2, b0, gc.reshape(m, LANES), grc, conv_w)

    def seq(width):
        return pl.BlockSpec((bpb, lc, width), lambda b, c: (b, c, 0))

    st4 = pl.BlockSpec((bpb, hg, HEAD_DIM, HEAD_DIM), lambda b, c: (b, 0, 0, 0))
    hgx, s1 = pl.pallas_call(
        functools.partial(_gdn_scan_kernel, hm=hm, hf=hf, hg=hg, lc=lc, bpb=bpb),
        grid=(nb // bpb, nc),
        in_specs=[seq(dg), seq(dg), seq(dg), seq(dg), seq(hg * lc),
                  pl.BlockSpec((bpb, lc, dg), lambda b, c: (b, c, base + 3)),
                  seq(LANES),
                  pl.BlockSpec((None, 1, HEAD_DIM), lambda b, c: (layer, 0, 0)),
                  st4],
        out_specs=[seq(dg), st4],
        out_shape=[jax.ShapeDtypeStruct((nb, t, dg), xd),
                   jax.ShapeDtypeStruct((nb, hg, HEAD_DIM, HEAD_DIM), F32)],
        name="gdn_scan",
        compiler_params=_params("arbitrary", "arbitrary"),
    )(w.reshape(nb, t, dg), u0.reshape(nb, t, dg), qg.reshape(nb, t, dg), kd.reshape(nb, t, dg),
      qk.reshape(nb, t, hg * lc), proj, gc, norm_g, s0)
    return hgx, s1, b1


def _fox_prompt_kernel(q_ref, k_ref, v_ref, gr_ref, o_ref, q_s, m_s, l_s, acc_s, *, hf, tb):
    qi = pl.program_id(1)
    ki = pl.program_id(2)
    rep = tb // LANES

    @pl.when(ki == 0)
    def _():
        q_s[...] = (q_ref[...] * HEAD_DIM ** -0.5).astype(BF16)
        m_s[...] = jnp.full(m_s.shape, NEG_INF, F32)
        l_s[...] = jnp.zeros(l_s.shape, F32)
        acc_s[...] = jnp.zeros(acc_s.shape, F32)

    def block(masked):
        if masked:
            visible = (lax.broadcasted_iota(jnp.int32, (tb, tb), 1) <= lax.broadcasted_iota(jnp.int32, (tb, tb), 0))
        for h in range(hf):
            sl = slice(h * HEAD_DIM, (h + 1) * HEAD_DIM)
            k = k_ref[:, sl].astype(BF16)
            v = v_ref[:, sl].astype(BF16)
            s = _dot_nt(q_s[:, sl], k) - gr_ref[h:h + 1, :]
            if masked:
                s = jnp.where(visible, s, NEG_INF)
            m_prev = m_s[h]
            m_new = jnp.maximum(m_prev, jnp.max(s, axis=-1, keepdims=True))
            alpha = jnp.exp(m_prev - m_new)
            p = jnp.exp(s - jnp.concatenate([m_new] * rep, axis=1))
            l_s[h] = alpha * l_s[h] + jnp.sum(p, axis=-1, keepdims=True)
            acc_s[:, sl] = alpha * acc_s[:, sl] + _dot(p.astype(BF16), v)
            m_s[h] = m_new

    @pl.when(ki < qi)
    def _():
        block(False)

    @pl.when(ki == qi)
    def _():
        block(True)
        for h in range(hf):
            sl = slice(h * HEAD_DIM, (h + 1) * HEAD_DIM)
            o_ref[:, sl] = (acc_s[:, sl] / l_s[h]).astype(BF16)


def _fox_prompt(proj, grf, nb, t, dims):
    hm, hf, hg = dims
    df = hf * HEAD_DIM
    base = (4 * hm * HEAD_DIM) // df
    tb = _pick(t, (512, 256, 128))
    nq = t // tb
    return pl.pallas_call(
        functools.partial(_fox_prompt_kernel, hf=hf, tb=tb),
        grid=(nb, nq, nq),
        in_specs=[pl.BlockSpec((tb, df), lambda b, i, j: (b * nq + i, base)),
                  pl.BlockSpec((tb, df), lambda b, i, j: (b * nq + jnp.minimum(j, i), base + 1)),
                  pl.BlockSpec((tb, df), lambda b, i, j: (b * nq + jnp.minimum(j, i), base + 2)),
                  pl.BlockSpec((None, hf, tb), lambda b, i, j: (b, 0, jnp.minimum(j, i)))],
        out_specs=pl.BlockSpec((tb, df), lambda b, i, j: (b * nq + i, 0)),
        out_shape=jax.ShapeDtypeStruct((nb * t, df), BF16),
        scratch_shapes=[pltpu.VMEM((tb, df), BF16), pltpu.VMEM((hf, tb, LANES), F32),
                        pltpu.VMEM((hf, tb, LANES), F32), pltpu.VMEM((tb, df), F32)],
        name="fox_prompt",
        compiler_params=_params("arbitrary", "arbitrary", "arbitrary"),
    )(proj, proj, proj, grf)


def _fox_pool_kernel(x_ref, w_ref, t_ref, *, hf, page):
    n = hf * page
    x = x_ref[...]
    x1 = x.astype(BF16)
    r1 = x - x1.astype(F32)
    x2 = r1.astype(BF16)
    x3 = (r1 - x2.astype(F32)).astype(BF16)
    ri = lax.broadcasted_iota(jnp.int32, (n, n), 0)
    ci = lax.broadcasted_iota(jnp.int32, (n, n), 1)
    later = ((ri // page) == (ci % hf)) & ((ri % page) > (ci // hf))
    m_suf = jnp.where(later, 1.0, 0.0).astype(BF16)
    rj = lax.broadcasted_iota(jnp.int32, (n, LANES), 0)
    lj = lax.broadcasted_iota(jnp.int32, (n, LANES), 1)
    m_tot = jnp.where((rj // page) == (lj % hf), 1.0, 0.0).astype(BF16)
    w_ref[...] = _dot(x1, m_suf) + (_dot(x2, m_suf) + _dot(x3, m_suf))
    t_ref[...] = _dot(x1, m_tot) + (_dot(x2, m_tot) + _dot(x3, m_tot))


def _fox_pool(logf_hs, hf, page):
    r, n = logf_hs.shape
    rb = _pick(r, (512, 256, 128, 64, 32, 16, 8))
    return pl.pallas_call(
        functools.partial(_fox_pool_kernel, hf=hf, page=page),
        grid=(r // rb,),
        in_specs=[pl.BlockSpec((rb, n), lambda i: (i, 0))],
        out_specs=[pl.BlockSpec((rb, n), lambda i: (i, 0)), pl.BlockSpec((rb, LANES), lambda i: (i, 0))],
        out_shape=[jax.ShapeDtypeStruct((r, n), F32), jax.ShapeDtypeStruct((r, LANES), F32)],
        name="fox_pool",
        compiler_params=_params("arbitrary"),
    )(logf_hs)


def _fox_sample_kernel(pt_ref, q_ref, cr_ref, kn_ref, vn_ref, *rest, hf, nsteps, rpp, npp):
    kp = rest[0:npp]
    vp = rest[npp:2 * npp]
    wp = rest[2 * npp:3 * npp]
    tp = rest[3 * npp:4 * npp]
    o_ref, m_s, l_s, acc_s, carry_s = rest[4 * npp:]
    p = pl.program_id(1)
    q = (q_ref[...] * HEAD_DIM ** -0.5).astype(BF16)
    nr = q.shape[0]

    @pl.when(p == 0)
    def _():
        ri = lax.broadcasted_iota(jnp.int32, (nr, nr), 0)
        ci = lax.broadcasted_iota(jnp.int32, (nr, nr), 1)
        ok = ((ri % hf) == (ci % hf)) & ((ci // hf) <= (ri // hf))
        s = jnp.where(ok, _dot_nt(q, kn_ref[...].astype(BF16)) - cr_ref[...], NEG_INF)
        m0 = jnp.max(s, axis=-1, keepdims=True)
        e = jnp.exp(s - m0)
        m_s[...] = m0
        l_s[...] = jnp.sum(e, axis=-1, keepdims=True)
        acc_s[...] = _dot(e.astype(BF16), vn_ref[...].astype(BF16))
        carry_s[...] = jnp.zeros_like(carry_s)

    ncol = rpp * LANES
    ri = lax.broadcasted_iota(jnp.int32, (nr, ncol), 0)
    ci = lax.broadcasted_iota(jnp.int32, (nr, ncol), 1)
    head_mask = jnp.where((ri % hf) == (ci % hf), 0.0, NEG_INF).astype(F32)
    carry = carry_s[...]
    scores = []
    for g in range(npp):
        bias_row = jnp.concatenate([wp[g][r:r + 1, :] + carry for r in range(rpp)], axis=1)
        scores.append(_dot_nt(q, kp[g][...].astype(BF16)) + (head_mask + bias_row))
        carry = carry + tp[g][...]
    carry_s[...] = carry
    top = scores[0]
    for g in range(1, npp):
        top = jnp.maximum(top, scores[g])
    m_prev = m_s[...]
    m_new = jnp.maximum(m_prev, jnp.max(top, axis=-1, keepdims=True))
    alpha = jnp.exp(m_prev - m_new)
    acc = alpha * acc_s[...]
    tot = None
    for g in range(npp):
        e = jnp.exp(scores[g] - m_new)
        tot = e if tot is None else tot + e
        acc = acc + _dot(e.astype(BF16), vp[g][...].astype(BF16))
    l_s[...] = alpha * l_s[...] + jnp.sum(tot, axis=-1, keepdims=True)
    acc_s[...] = acc
    m_s[...] = m_new

    @pl.when(p == nsteps - 1)
    def _():
        o_ref[...] = (acc / l_s[...]).astype(BF16)


def _fox_sample(page_table, q2, cum_row, kn2, vn2, kpool, vpool, wpool, tpool, layer, hf):
    nb, npg = page_table.shape
    nr = q2.shape[1]
    nphys, pcols = kpool.shape[1], kpool.shape[2]
    rpp = pcols // LANES
    npp = _pick(npg, (8, 4, 2, 1))
    nsteps = npg // npp

    def page(b, p, pt, g):
        return pt[b * npg + (npg - 1 - (p * npp + g))]

    per_b3 = lambda b, p, pt: (b, 0, 0)
    kv_specs = [pl.BlockSpec((None, None, pcols, HEAD_DIM),
                             lambda b, p, pt, g=g: (layer, page(b, p, pt, g), 0, 0)) for g in range(npp)]
    w_specs = [pl.BlockSpec((None, rpp, LANES),
                            lambda b, p, pt, g=g: (layer * nphys + page(b, p, pt, g), 0, 0)) for g in range(npp)]
    t_specs = [pl.BlockSpec((None, 1, LANES),
                            lambda b, p, pt, g=g: (layer * nphys + page(b, p, pt, g), 0, 0)) for g in range(npp)]
    return pl.pallas_call(
        functools.partial(_fox_sample_kernel, hf=hf, nsteps=nsteps, rpp=rpp, npp=npp),
        grid_spec=pltpu.PrefetchScalarGridSpec(
            num_scalar_prefetch=1,
            grid=(nb, nsteps),
            in_specs=[pl.BlockSpec((None, nr, HEAD_DIM), per_b3),
                      pl.BlockSpec((None, 1, nr), per_b3),
                      pl.BlockSpec((None, nr, HEAD_DIM), per_b3),
                      pl.BlockSpec((None, nr, HEAD_DIM), per_b3)] + kv_specs + kv_specs + w_specs + t_specs,
            out_specs=pl.BlockSpec((None, nr, HEAD_DIM), per_b3),
            scratch_shapes=[pltpu.VMEM((nr, 1), F32), pltpu.VMEM((nr, 1), F32),
                            pltpu.VMEM((nr, HEAD_DIM), F32), pltpu.VMEM((1, LANES), F32)]),
        out_shape=jax.ShapeDtypeStruct((nb, nr, HEAD_DIM), BF16),
        name="fox_decode",
        compiler_params=_params("arbitrary", "arbitrary"),
    )(page_table.reshape(-1), q2, cum_row, kn2, vn2, *([kpool] * npp), *([vpool] * npp),
      *([wpool] * npp), *([tpool] * npp))


def _regroup_w_in(w_in, b_in, dims):
    hm, hf, hg = dims
    dm, df, dg = hm * HEAD_DIM, hf * HEAD_DIM, hg * HEAD_DIM
    sizes = (dm, dm, dm, dm, hm, hm, df, df, df, hf, dg, dg, dg, dg, hg, hg)
    offs = [0]
    for s in sizes:
        offs.append(offs[-1] + s)
    order = (0, 1, 2, 3, 6, 7, 8, 10, 11, 12, 13, 4, 5, 9, 14, 15)
    n_main = 4 * dm + 3 * df + 4 * dg
    n_gate = 2 * hm + hf + 2 * hg
    n_tot = -(-(n_main + LANES) // 512) * 512
    pad = n_tot - n_main - n_gate

    def regroup(a, axis):
        parts = [lax.slice_in_dim(a, offs[i], offs[i + 1], axis=axis) for i in order]
        zshape = list(a.shape)
        zshape[axis] = pad
        parts.append(jnp.zeros(zshape, a.dtype))
        return jnp.concatenate(parts, axis=axis)

    w_t = regroup(jnp.swapaxes(w_in, 1, 2).astype(BF16), 1)
    return w_t, regroup(b_in, 1)[:, None, :], n_main // LANES


def _gate_param_cols(a_log, dt_bias, dims):
    hm, hf, hg = dims
    depth = a_log.shape[0]
    lead = 2 * hm + hf
    z0 = jnp.zeros((depth, lead), F32)
    z1 = jnp.zeros((depth, GATE_ROWS - lead - hg), F32)
    a_col = jnp.concatenate([z0, a_log, z1], axis=1)[:, :, None]
    dt_col = jnp.concatenate([z0, dt_bias, z1], axis=1)[:, :, None]
    return a_col, dt_col


def kernel(x_prompt, x_sample, cache_fox_k, cache_fox_v, cache_fox_logf, state_mlstm_C, state_mlstm_n, state_mlstm_m, state_gdn_S, state_gdn_conv, page_table, c_prompt, c_sample, w_ada, b_ada, ln_g, ln_b, ffn_w_gate, ffn_w_up, ffn_w_down, w_in, b_in, mlstm_norm_g, gdn_conv_w, gdn_A_log, gdn_dt_bias, gdn_norm_g, w_out):
    bp, tp, d = x_prompt.shape
    bs, ts, _ = x_sample.shape
    depth = w_ada.shape[0]
    hm, hf, hg = state_mlstm_C.shape[2], cache_fox_k.shape[3], state_gdn_S.shape[2]
    dims = (hm, hf, hg)
    dm, df, dg = hm * HEAD_DIM, hf * HEAD_DIM, hg * HEAD_DIM
    nphys, page = cache_fox_k.shape[1], cache_fox_k.shape[2]
    alpha = (2 * depth) ** 0.25
    assert 2 * hm + hf + 2 * hg <= GATE_ROWS and (page * hf) % LANES == 0 and LANES % hf == 0

    nrow = -(-(bp + bs) // 16) * 16
    c_all = jnp.concatenate([c_prompt, c_sample, jnp.zeros((nrow - bp - bs, d), F32)], axis=0)
    mod = _ada(c_all, w_ada, b_ada).reshape(depth, nrow, 3 * N_SUB, d)
    mod = jnp.transpose(mod, (0, 2, 1, 3))
    mod_p = mod[:, :, :bp, None, :]
    mod_s = jnp.repeat(mod[:, :, bp:bp + bs], ts, axis=2)[:, :, None, :, :]

    w_in_r, b_in_r, gate_blk = _regroup_w_in(w_in, b_in, dims)
    a_col, dt_col = _gate_param_cols(gdn_A_log, gdn_dt_bias, dims)
    ln_g4 = ln_g[:, :, None, :]
    ln_b4 = ln_b[:, :, None, :]
    norm_m = mlstm_norm_g[:, :, None, :]
    norm_g = gdn_norm_g[:, None, :]

    rpp = (page * hf) // LANES
    logf_hs = jnp.transpose(cache_fox_logf, (0, 1, 3, 2)).reshape(depth * nphys, hf * page)
    wpool, tpool = _fox_pool(logf_hs, hf, page)
    wpool = wpool.reshape(depth * nphys, rpp, LANES)
    tpool = tpool.reshape(depth * nphys, 1, LANES)
    kpool = cache_fox_k.reshape(depth, nphys, page * hf, HEAD_DIM)
    vpool = cache_fox_v.reshape(depth, nphys, page * hf, HEAD_DIM)

    o_f = 4 * dm
    o_g = o_f + 3 * df

    def mixers(l, h, nb, t, mlstm_state, gdn_state, sample):
        m_rows = nb * t
        proj = _win(h, w_in_r, b_in_r, l)
        lc = math.gcd(t, CHUNK)
        if sample:
            gc, grc, grf = _gates(proj, gate_blk, a_col, dt_col, l, 1, m_rows, lc, t, dims)
            grc = grc.reshape(nb, t // lc, 2 * GATE_ROWS, lc)
        else:
            gc, grc, grf = _gates(proj, gate_blk, a_col, dt_col, l, nb, t, lc, t, dims)
        c0, n0, m0 = mlstm_state
        proj3 = proj.reshape(nb, t, proj.shape[-1])
        gc3 = gc.reshape(nb, t, LANES)
        bpb = 1 if sample else nb
        hmx, c1, n1, m1 = _mlstm(proj3, gc3, grc, norm_m, c0, n0[:, :, None, :], m0[:, :, None, None],
                                 l, nb, t, bpb, dims)
        s0, b0 = gdn_state
        hgx, s1, b1 = _gdn(proj3, gc3, grc, gdn_conv_w, norm_g, s0, b0, l, nb, t, bpb, dims)
        hmx = hmx.reshape(m_rows, dm)
        hgx = hgx.reshape(m_rows, dg)
        fk = proj[:, o_f + df:o_f + 2 * df]
        fv = proj[:, o_f + 2 * df:o_f + 3 * df]
        logf = gc[:, GATE_ROWS + 2 * hm:GATE_ROWS + 2 * hm + hf]
        if sample:
            nr = t * hf
            q2 = proj[:, o_f:o_f + df].reshape(nb, nr, HEAD_DIM)
            cum = gc[:, 2 * hm:2 * hm + hf].reshape(nb, nr)
            hfx = _fox_sample(page_table, q2, cum[:, None, :], fk.reshape(nb, nr, HEAD_DIM),
                              fv.reshape(nb, nr, HEAD_DIM), kpool, vpool, wpool, tpool, l, hf)
            hfx = hfx.reshape(m_rows, df)
        else:
            hfx = _fox_prompt(proj, grf, nb, t, dims)
        new_state = (fk.reshape(nb, t, hf, HEAD_DIM), fv.reshape(nb, t, hf, HEAD_DIM), logf.reshape(nb, t, hf),
                     c1, n1[:, :, 0, :], m1[:, :, 0, 0], s1, b1)
        return (hmx.astype(BF16), hfx, hgx.astype(BF16)), new_state

    xp = x_prompt.reshape(bp * tp, d)
    xs = x_sample.reshape(bs * ts, d)
    hp = _modulate(xp, mod_p, 0, 0, tp)
    hs = _modulate(xs, mod_s, 0, 0, bs * ts)
    zero_mlstm = (jnp.zeros((bp, hm, HEAD_DIM, HEAD_DIM), F32), jnp.zeros((bp, hm, HEAD_DIM), F32),
                  jnp.zeros((bp, hm), F32))
    zero_gdn = (jnp.zeros((bp, hg, HEAD_DIM, HEAD_DIM), F32), jnp.zeros((bp, 3, 3 * dg), F32))

    def proj_ln(parts_p, parts_s, w, w_prefix, l, j, nxt, coef):
        return _proj_ln(parts_p, parts_s, w, w_prefix, xp, xs, mod_p, mod_s, l, j, nxt, ln_g4, ln_b4, tp, alpha, coef)

    st_p, st_s = [], []
    for l in range(depth):
        ap, as_ = _gateup(hp, hs, ffn_w_gate, ffn_w_up, l, 0)
        (xp, hp), (xs, hs) = proj_ln([ap], [as_], ffn_w_down, (l, 0), l, 0, (l, 1), MACARON_W)
        mixed_p, sp = mixers(l, hp, bp, tp, zero_mlstm, zero_gdn, False)
        mixed_s, ss = mixers(l, hs, bs, ts, (state_mlstm_C[l], state_mlstm_n[l], state_mlstm_m[l]),
                             (state_gdn_S[l], state_gdn_conv[l]), True)
        (xp, hp), (xs, hs) = proj_ln(mixed_p, mixed_s, w_out, (l,), l, 1, (l, 2), 1.0)
        ap, as_ = _gateup(hp, hs, ffn_w_gate, ffn_w_up, l, 1)
        nxt = (l + 1, 0) if l + 1 < depth else None
        (xp, hp), (xs, hs) = proj_ln([ap], [as_], ffn_w_down, (l, 1), l, 2, nxt, MACARON_W)
        st_p.append(sp)
        st_s.append(ss)
    pn = [jnp.stack(a) for a in zip(*st_p)]
    sn = [jnp.stack(a) for a in zip(*st_s)]
    return (xp.reshape(bp, tp, d), xs.reshape(bs, ts, d), pn[0], pn[1], pn[2], pn[3], pn[4], pn[5], pn[6], pn[7],
            sn[0], sn[1], sn[2], sn[3], sn[4], sn[5], sn[6], sn[7])
```

```python
import functools
import math

import jax
import jax.numpy as jnp
from jax import lax
from jax.experimental import pallas as pl
from jax.experimental.pallas import tpu as pltpu

F32 = jnp.float32
BF16 = jnp.bfloat16

HEAD_DIM = 128
LANES = 128
SUBLANES = 8
CHUNK = 64
N_SUB = 3
LN_EPS = 1e-5
NORM_EPS = 1e-6
MACARON_W = 0.5
GATE_ROWS = 32
VMEM_LIMIT_BYTES = 56 * 1024 * 1024
HI = lax.Precision.HIGHEST
NEG_INF = float("-inf")


def _params(*sem):
    return pltpu.CompilerParams(dimension_semantics=sem, vmem_limit_bytes=VMEM_LIMIT_BYTES)


def _dot(a, b, precision=None):
    return jnp.dot(a, b, preferred_element_type=F32, precision=precision)


def _dot_nt(a, b, precision=None):
    return lax.dot_general(a, b, (((1,), (1,)), ((), ())), preferred_element_type=F32, precision=precision)


def _dot_tn(a, b, precision=None):
    return lax.dot_general(a, b, (((0,), (0,)), ((), ())), preferred_element_type=F32, precision=precision)


def _sigmoid(x):
    return 1.0 / (1.0 + jnp.exp(-x))


def _silu(x):
    return x * _sigmoid(x)


def _softplus(x):
    return jnp.maximum(x, 0.0) + jnp.log1p(jnp.exp(-jnp.abs(x)))


def _log_sigmoid(x):
    return -_softplus(-x)


def _pick(m, candidates):
    for c in candidates:
        if m % c == 0:
            return c
    return m


def _ada_kernel(c_ref, w_ref, b_ref, o_ref):
    c = c_ref[...]
    sc = _silu(c).astype(BF16)
    o_ref[...] = _dot(sc, w_ref[...].astype(BF16)) + b_ref[...]


def _ada(c_all, w_ada, b_ada):
    depth, d, n = w_ada.shape
    rows = c_all.shape[0]
    bn = _pick(n, (1024, 512, 256, 128))
    return pl.pallas_call(
        _ada_kernel,
        grid=(depth, n // bn),
        in_specs=[pl.BlockSpec((rows, d), lambda l, j: (0, 0)),
                  pl.BlockSpec((None, d, bn), lambda l, j: (l, 0, j)),
                  pl.BlockSpec((None, 1, bn), lambda l, j: (l, 0, j))],
        out_specs=pl.BlockSpec((None, rows, bn), lambda l, j: (l, 0, j)),
        out_shape=jax.ShapeDtypeStruct((depth, rows, n), F32),
        name="ada",
        compiler_params=_params("arbitrary", "arbitrary"),
    )(c_all, w_ada, b_ada.reshape(depth, 1, n))


def _modulate_kernel(x_ref, sc_ref, sh_ref, o_ref):
    o_ref[...] = (x_ref[...] * (1.0 + sc_ref[...]) + sh_ref[...]).astype(BF16)


def _mod_spec(mod, layer, slot, rows_per_group, bm):
    r, d = mod.shape[3], mod.shape[4]
    if r == 1:
        return pl.BlockSpec((None, None, None, 1, d),
                            lambda m, *_: (layer, slot, (m * bm) // rows_per_group, 0, 0))
    return pl.BlockSpec((None, None, None, bm, d), lambda m, *_: (layer, slot, 0, m, 0))


def _modulate(x, mod, layer, j, rows_per_group):
    m, d = x.shape
    bm = _pick(rows_per_group, (512, 256, 128, 64))
    return pl.pallas_call(
        _modulate_kernel,
        grid=(m // bm,),
        in_specs=[pl.BlockSpec((bm, d), lambda i: (i, 0)),
                  _mod_spec(mod, layer, 3 * j + 1, rows_per_group, bm),
                  _mod_spec(mod, layer, 3 * j + 0, rows_per_group, bm)],
        out_specs=pl.BlockSpec((bm, d), lambda i: (i, 0)),
        out_shape=jax.ShapeDtypeStruct((m, d), BF16),
        name="modulate",
        compiler_params=_params("arbitrary"),
    )(x, mod, mod)


def _gateup_kernel(h_ref, hs_ref, wg_ref, wu_ref, o_ref, os_ref, wg_s, wu_s, *, last):
    i = pl.program_id(1)

    @pl.when(i == 0)
    def _():
        wg_s[...] = wg_ref[...].astype(BF16)
        wu_s[...] = wu_ref[...].astype(BF16)

    def swiglu_half(h):
        g = _dot(h, wg_s[...])
        u = _dot(h, wu_s[...])
        return (_silu(g) * u).astype(BF16)

    o_ref[...] = swiglu_half(h_ref[...])

    @pl.when(i == last)
    def _():
        os_ref[...] = swiglu_half(hs_ref[...])


def _gateup(h, hs, w_gate, w_up, layer, j):
    m, d = h.shape
    ms = hs.shape[0]
    f = w_gate.shape[-1]
    bm = _pick(m, (2048, 1024, 512, 256, 128, 64))
    bn = _pick(f, (512, 256, 128))
    wspec = pl.BlockSpec((None, None, d, bn), lambda n, i: (layer, j, 0, n))
    return pl.pallas_call(
        functools.partial(_gateup_kernel, last=m // bm - 1),
        grid=(f // bn, m // bm),
        in_specs=[pl.BlockSpec((bm, d), lambda n, i: (i, 0)), pl.BlockSpec((ms, d), lambda n, i: (0, 0)),
                  wspec, wspec],
        out_specs=[pl.BlockSpec((bm, bn), lambda n, i: (i, n)), pl.BlockSpec((ms, bn), lambda n, i: (0, n))],
        out_shape=[jax.ShapeDtypeStruct((m, f), BF16), jax.ShapeDtypeStruct((ms, f), BF16)],
        scratch_shapes=[pltpu.VMEM((d, bn), BF16), pltpu.VMEM((d, bn), BF16)],
        name="gateup",
        compiler_params=_params("arbitrary", "arbitrary"),
    )(h, hs, w_gate, w_up)


def _proj_ln_kernel(*refs, widths, nk, nm, bk, alpha, coef, emit_h):
    npart = len(widths)
    nin = npart + 2 + (2 if emit_h else 0)
    nout = 2 if emit_h else 1
    w_ref, lng_ref, lnb_ref = refs[0:3]
    ins = [refs[3 + s * nin:3 + (s + 1) * nin] for s in range(2)]
    outs = [refs[3 + 2 * nin + s * nout:3 + 2 * nin + (s + 1) * nout] for s in range(2)]
    w_s, acc_s = refs[3 + 2 * nin + 2 * nout:]
    i = pl.program_id(0)

    @pl.when(i < nk)
    def _():
        w_s[pl.ds(pl.multiple_of(i * bk, bk), bk), :] = w_ref[...].astype(BF16)

    def project(s):
        acc = None
        off = 0
        for a_ref, wd in zip(ins[s][:npart], widths):
            term = _dot(a_ref[...], w_s[off:off + wd, :])
            acc = term if acc is None else acc + term
            off += wd
        return acc

    def finish(s, acc):
        x_ref, gate_ref = ins[s][npart], ins[s][npart + 1]
        y = alpha * x_ref[...] + (coef * (1.0 + gate_ref[...])) * acc
        mu = jnp.mean(y, axis=-1, keepdims=True)
        yc = y - mu
        var = jnp.mean(yc * yc, axis=-1, keepdims=True)
        xn = yc * lax.rsqrt(var + LN_EPS) * lng_ref[...] + lnb_ref[...]
        outs[s][0][...] = xn
        if emit_h:
            sc_ref, sh_ref = ins[s][npart + 2], ins[s][npart + 3]
            outs[s][1][...] = (xn * (1.0 + sc_ref[...]) + sh_ref[...]).astype(BF16)

    j = i - nk
    cur = lax.rem(j, 2)

    @pl.when(j == 0)
    def _():
        acc_s[0] = project(0)

    @pl.when((j > 0) & (j < nm))
    def _():
        finish(0, acc_s[1 - cur])
        acc_s[cur] = project(0)

    @pl.when(j == nm)
    def _():
        finish(0, acc_s[1 - cur])

    @pl.when(j == nm + 1)
    def _():
        finish(1, project(1))


def _proj_ln(parts_p, parts_s, w, w_prefix, x_p, x_s, mod_p, mod_s, layer, j, nxt, ln_g, ln_b, rows_per_group,
             alpha, coef):
    widths = tuple(a.shape[1] for a in parts_p)
    k = sum(widths)
    m, d = x_p.shape
    ms = x_s.shape[0]
    bm = _pick(rows_per_group, (256, 128, 64))
    bk = _pick(k, (256, 128))
    nk = k // bk
    nm = m // bm
    emit_h = nxt is not None
    npre = len(w_prefix)

    def row_a(i):
        return jnp.clip(i - nk, 0, nm - 1)

    def row(i):
        return jnp.clip(i - nk - 1, 0, nm - 1)

    def modspec_p(lyr, slot):
        return pl.BlockSpec((None, None, None, 1, d),
                            lambda i: (lyr, slot, (row(i) * bm) // rows_per_group, 0, 0))

    def modspec_s(lyr, slot):
        return pl.BlockSpec((None, None, None, ms, d), lambda i: (lyr, slot, 0, 0, 0))

    in_specs = [pl.BlockSpec((None,) * npre + (bk, d), lambda i: tuple(w_prefix) + (jnp.minimum(i, nk - 1), 0)),
                pl.BlockSpec((None, None, 1, d), lambda i: (layer, j, 0, 0)),
                pl.BlockSpec((None, None, 1, d), lambda i: (layer, j, 0, 0))]
    args = [w, ln_g, ln_b]
    in_specs += [pl.BlockSpec((bm, wd), lambda i: (row_a(i), 0)) for wd in widths]
    in_specs += [pl.BlockSpec((bm, d), lambda i: (row(i), 0)), modspec_p(layer, 3 * j + 2)]
    args += list(parts_p) + [x_p, mod_p]
    if emit_h:
        in_specs += [modspec_p(nxt[0], 3 * nxt[1] + 1), modspec_p(nxt[0], 3 * nxt[1] + 0)]
        args += [mod_p, mod_p]
    in_specs += [pl.BlockSpec((ms, wd), lambda i: (0, 0)) for wd in widths]
    in_specs += [pl.BlockSpec((ms, d), lambda i: (0, 0)), modspec_s(layer, 3 * j + 2)]
    args += list(parts_s) + [x_s, mod_s]
    if emit_h:
        in_specs += [modspec_s(nxt[0], 3 * nxt[1] + 1), modspec_s(nxt[0], 3 * nxt[1] + 0)]
        args += [mod_s, mod_s]
    out_specs = [pl.BlockSpec((bm, d), lambda i: (row(i), 0))]
    out_shape = [jax.ShapeDtypeStruct((m, d), F32)]
    if emit_h:
        out_specs.append(pl.BlockSpec((bm, d), lambda i: (row(i), 0)))
        out_shape.append(jax.ShapeDtypeStruct((m, d), BF16))
    out_specs.append(pl.BlockSpec((ms, d), lambda i: (0, 0)))
    out_shape.append(jax.ShapeDtypeStruct((ms, d), F32))
    if emit_h:
        out_specs.append(pl.BlockSpec((ms, d), lambda i: (0, 0)))
        out_shape.append(jax.ShapeDtypeStruct((ms, d), BF16))
    out = pl.pallas_call(
        functools.partial(_proj_ln_kernel, widths=widths, nk=nk, nm=nm, bk=bk, alpha=alpha, coef=coef,
                          emit_h=emit_h),
        grid=(nk + nm + 2,),
        in_specs=in_specs,
        out_specs=out_specs,
        out_shape=out_shape,
        scratch_shapes=[pltpu.VMEM((k, d), BF16), pltpu.VMEM((2, bm, d), F32)],
        name="proj_ln",
        compiler_params=_params("arbitrary"),
    )(*args)
    if emit_h:
        return (out[0], out[1]), (out[2], out[3])
    return (out[0], None), (out[1], None)


def _win_kernel(h_ref, w_ref, b_ref, o_ref):
    o_ref[...] = _dot_nt(h_ref[...], w_ref[...]) + b_ref[...]


def _win(h, w, b, layer):
    m, d = h.shape
    n = w.shape[1]
    bm = _pick(m, (2048, 1024, 512, 256, 128, 64))
    bn = _pick(n, (768, 512, 256, 128))
    return pl.pallas_call(
        _win_kernel,
        grid=(m // bm, n // bn),
        in_specs=[pl.BlockSpec((bm, d), lambda i, c: (i, 0)),
                  pl.BlockSpec((None, bn, d), lambda i, c: (layer, c, 0)),
                  pl.BlockSpec((None, 1, bn), lambda i, c: (layer, 0, c))],
        out_specs=pl.BlockSpec((bm, bn), lambda i, c: (i, c)),
        out_shape=jax.ShapeDtypeStruct((m, n), F32),
        name="w_in",
        compiler_params=_params("arbitrary", "arbitrary"),
    )(h, w, b)


def _gates_kernel(x_ref, a_ref, dt_ref, gc_ref, grc_ref, grf_ref, carry_ref, *, hm, hf, hg, lc, seg, tb):
    t = pl.program_id(1)

    @pl.when(t == 0)
    def _():
        carry_ref[...] = jnp.zeros_like(carry_ref)

    z = jnp.transpose(x_ref[...])[:GATE_ROWS]
    ch = lax.broadcasted_iota(jnp.int32, (GATE_ROWS, 1), 0)
    is_mi = ch < hm
    is_mf = (ch >= hm) & (ch < 2 * hm)
    is_ff = (ch >= 2 * hm) & (ch < 2 * hm + hf)
    is_ga = (ch >= 2 * hm + hf) & (ch < 2 * hm + hf + hg)
    is_gb = (ch >= 2 * hm + hf + hg) & (ch < 2 * hm + hf + 2 * hg)
    ls = _log_sigmoid(z)
    gg = -jnp.exp(a_ref[...]) * _softplus(z + dt_ref[...])
    sg = _sigmoid(z)
    s_i = lax.broadcasted_iota(jnp.int32, (tb, tb), 0)
    t_i = lax.broadcasted_iota(jnp.int32, (tb, tb), 1)
    m_loc = jnp.where((s_i <= t_i) & ((s_i // lc) == (t_i // lc)), 1.0, 0.0).astype(F32)
    m_seg = jnp.where((s_i <= t_i) & ((s_i // seg) == (t_i // seg)), 1.0, 0.0).astype(F32)
    v_loc = jnp.where(is_mf, ls, jnp.where(is_ga, gg, 0.0))
    v_seg = jnp.where(is_ff, ls, 0.0)
    cum_loc = _dot(v_loc, m_loc, HI)
    cum_seg = _dot(v_seg, m_seg, HI) + carry_ref[...]
    if seg > tb:
        carry_ref[...] = cum_seg[:, tb - 1:tb]
    bank0 = jnp.where(is_mi, z, jnp.where(is_mf | is_ga, cum_loc,
                                          jnp.where(is_ff, cum_seg, jnp.where(is_gb, sg, 0.0))))
    bank1 = jnp.where(is_ff, ls, 0.0)
    rows = jnp.concatenate([bank0, bank1], axis=0)
    full = jnp.concatenate([rows, jnp.zeros((LANES - 2 * GATE_ROWS, tb), F32)], axis=0)
    gc_ref[...] = jnp.transpose(full)
    for c in range(tb // lc):
        grc_ref[c] = rows[:, c * lc:(c + 1) * lc]
    grf_ref[...] = bank0[2 * hm:2 * hm + hf]


def _gates(proj, gate_blk, a_col, dt_col, layer, nb, t, lc, seg, dims):
    hm, hf, hg = dims
    tb = _pick(t, (512, 256, 128, 64))
    nt = t // tb
    return pl.pallas_call(
        functools.partial(_gates_kernel, hm=hm, hf=hf, hg=hg, lc=lc, seg=seg, tb=tb),
        grid=(nb, nt),
        in_specs=[pl.BlockSpec((tb, LANES), lambda b, i: (b * nt + i, gate_blk)),
                  pl.BlockSpec((None, GATE_ROWS, 1), lambda b, i: (layer, 0, 0)),
                  pl.BlockSpec((None, GATE_ROWS, 1), lambda b, i: (layer, 0, 0))],
        out_specs=[pl.BlockSpec((tb, LANES), lambda b, i: (b * nt + i, 0)),
                   pl.BlockSpec((None, tb // lc, 2 * GATE_ROWS, lc), lambda b, i: (b, i, 0, 0)),
                   pl.BlockSpec((None, hf, tb), lambda b, i: (b, 0, i))],
        out_shape=[jax.ShapeDtypeStruct((nb * t, LANES), F32),
                   jax.ShapeDtypeStruct((nb, t // lc, 2 * GATE_ROWS, lc), F32),
                   jax.ShapeDtypeStruct((nb, hf, t), F32)],
        scratch_shapes=[pltpu.VMEM((GATE_ROWS, 1), F32)],
        name="gates",
        compiler_params=_params("arbitrary", "arbitrary"),
    )(proj, a_col, dt_col)


def _mlstm_kernel(q_ref, k_ref, v_ref, o_ref, gc_ref, gr_ref, ng_ref, c0_ref, n0_ref, m0_ref,
                  h_ref, c_ref, n_ref, m_ref, *, hm, lc, bpb, mxu_dtype):
    @pl.when(pl.program_id(1) == 0)
    def _():
        c_ref[...] = c0_ref[...]
        n_ref[...] = n0_ref[...]
        m_ref[...] = m0_ref[...]

    r_i = lax.broadcasted_iota(jnp.int32, (lc, lc), 0)
    c_i = lax.broadcasted_iota(jnp.int32, (lc, lc), 1)
    causal = c_i <= r_i
    scale = HEAD_DIM ** -0.5
    chains = [(b, h) for b in range(bpb) for h in range(hm)]
    sls = [slice(h * HEAD_DIM, (h + 1) * HEAD_DIM) for _, h in chains]
    first = []
    for (b, h), sl in zip(chains, sls):
        q = q_ref[b, :, sl]
        k = k_ref[b, :, sl] * scale
        qx = q.astype(mxu_dtype)
        vx = v_ref[b, :, sl].astype(mxu_dtype)
        first.append(dict(q=q, k=k, vx=vx, qk=_dot_nt(qx, k.astype(mxu_dtype)),
                          qc=_dot(qx, c_ref[b, h].astype(mxu_dtype))))
    gate = []
    for b, h in chains:
        gc = gc_ref[b]
        gr = gr_ref[b]
        li_col = gc[:, h:h + 1]
        b_col = gc[:, hm + h:hm + h + 1]
        li_row = gr[h:h + 1, :]
        b_row = gr[hm + h:hm + h + 1, :]
        mst = m_ref[b, h]
        dmat = jnp.where(causal, b_col - b_row + li_row, NEG_INF)
        inter = b_col + mst
        b_last = b_col[lc - 1:lc, :]
        d_last = b_last - b_col + li_col
        gate.append(dict(dmat=dmat, inter=inter, d_last=d_last, carry=b_last + mst,
                         dmax=jnp.max(dmat, axis=-1, keepdims=True), lmax=jnp.max(d_last, axis=0, keepdims=True)))
    for g in gate:
        m_row = jnp.maximum(g["inter"], g["dmax"])
        m_new = jnp.maximum(g["carry"], g["lmax"])
        g.update(m_row=m_row, w_intra=jnp.exp(g["dmat"] - m_row), w_inter=jnp.exp(g["inter"] - m_row),
                 m_new=m_new, w_k=jnp.exp(g["d_last"] - m_new), decay=jnp.exp(g["carry"] - m_new))
    outs = []
    for (b, h), g, f in zip(chains, gate, first):
        cst = c_ref[b, h]
        nst = n_ref[b, h]
        s = f["qk"] * g["w_intra"]
        num = _dot(s.astype(mxu_dtype), f["vx"]) + g["w_inter"] * f["qc"]
        den = jnp.sum(s, axis=-1, keepdims=True) + g["w_inter"] * jnp.sum(f["q"] * nst, axis=-1, keepdims=True)
        outs.append(num / jnp.maximum(jnp.abs(den), jnp.exp(-g["m_row"])))
        kw = f["k"] * g["w_k"]
        c_ref[b, h] = g["decay"] * cst + _dot_tn(kw.astype(mxu_dtype), f["vx"])
        n_ref[b, h] = g["decay"] * nst + jnp.sum(kw, axis=0, keepdims=True)
        m_ref[b, h] = g["m_new"]
    for (b, h), sl, hh in zip(chains, sls, outs):
        mu = jnp.mean(hh, axis=-1, keepdims=True)
        hc = hh - mu
        var = jnp.mean(hc * hc, axis=-1, keepdims=True)
        hn = hc * lax.rsqrt(var + LN_EPS) * ng_ref[h]
        h_ref[b, :, sl] = (hn * _sigmoid(o_ref[b, :, sl])).astype(h_ref.dtype)


def _mlstm(proj, gc, grc, norm_g, c0, n0, m0, layer, nb, t, bpb, dims):
    hm = dims[0]
    dm = hm * HEAD_DIM
    lc = math.gcd(t, CHUNK)
    nc = t // lc
    mxu_dtype = BF16 if lc >= 16 else F32

    def col(j):
        return pl.BlockSpec((bpb, lc, dm), lambda b, c: (b, c, j))

    st4 = pl.BlockSpec((bpb, hm, HEAD_DIM, HEAD_DIM), lambda b, c: (b, 0, 0, 0))
    stn = pl.BlockSpec((bpb, hm, 1, HEAD_DIM), lambda b, c: (b, 0, 0, 0))
    stm = pl.BlockSpec((bpb, hm, 1, 1), lambda b, c: (b, 0, 0, 0))
    return pl.pallas_call(
        functools.partial(_mlstm_kernel, hm=hm, lc=lc, bpb=bpb, mxu_dtype=mxu_dtype),
        grid=(nb // bpb, nc),
        in_specs=[col(0), col(1), col(2), col(3),
                  pl.BlockSpec((bpb, lc, LANES), lambda b, c: (b, c, 0)),
                  pl.BlockSpec((bpb, None, 2 * GATE_ROWS, lc), lambda b, c: (b, c, 0, 0)),
                  pl.BlockSpec((None, hm, 1, HEAD_DIM), lambda b, c: (layer, 0, 0, 0)),
                  st4, stn, stm],
        out_specs=[pl.BlockSpec((bpb, lc, dm), lambda b, c: (b, c, 0)), st4, stn, stm],
        out_shape=[jax.ShapeDtypeStruct((nb, t, dm), mxu_dtype),
                   jax.ShapeDtypeStruct((nb, hm, HEAD_DIM, HEAD_DIM), F32),
                   jax.ShapeDtypeStruct((nb, hm, 1, HEAD_DIM), F32),
                   jax.ShapeDtypeStruct((nb, hm, 1, 1), F32)],
        name="mlstm",
        compiler_params=_params("arbitrary", "arbitrary"),
    )(proj, proj, proj, proj, gc, grc, norm_g, c0, n0, m0)


def _split(a):
    hi = a.astype(BF16)
    return hi, (a - hi.astype(F32)).astype(BF16)


def _mm3(a, b):
    return _dot(a[0], b[0]) + (_dot(a[0], b[1]) + _dot(a[1], b[0]))


def _mm3_nt(a, b):
    return _dot_nt(a[0], b[0]) + (_dot_nt(a[0], b[1]) + _dot_nt(a[1], b[0]))


def _unit_lower_solve(a, rhs, lc, split_bf16):
    r_i = lax.broadcasted_iota(jnp.int32, (lc, lc), 0)
    c_i = lax.broadcasted_iota(jnp.int32, (lc, lc), 1)
    eye = jnp.where(r_i == c_i, 1.0, 0.0).astype(F32)
    if split_bf16:
        prep, mm = _split, _mm3
    else:
        prep, mm = (lambda x: x), (lambda x, y: _dot(x, y, HI))
    npows = [-x for x in a]
    invs = [eye + n for n in npows]
    width = 2
    while width < lc:
        ns = [prep(n) for n in npows]
        npows = [mm(n, n) for n in ns]
        invs = [inv + mm(prep(inv), prep(n)) for inv, n in zip(invs, npows)]
        width *= 2
    return [mm(prep(inv), prep(r)) for inv, r in zip(invs, rhs)]


def _gdn_prep_kernel(q_ref, k_ref, v_ref, pq_ref, pk_ref, pv_ref, b0_ref, gc_ref, gr_ref, cw_ref,
                     w_ref, u_ref, qg_ref, kd_ref, qk_ref, buf_ref, xp_ref, *, hm, hf, hg, lc, cb):
    dg = hg * HEAD_DIM
    keep = SUBLANES
    tb = cb * lc
    split_bf16 = lc >= 16
    prev = jnp.concatenate([pq_ref[keep - 3:keep, :], pk_ref[keep - 3:keep, :], pv_ref[keep - 3:keep, :]], axis=1)
    xp_ref[keep - 3:keep, :] = jnp.where(pl.program_id(1) == 0, b0_ref[...], prev)
    xp_ref[keep:keep + tb, 0:dg] = q_ref[...]
    xp_ref[keep:keep + tb, dg:2 * dg] = k_ref[...]
    xp_ref[keep:keep + tb, 2 * dg:3 * dg] = v_ref[...]
    conv = xp_ref[keep:keep + tb, :] * cw_ref[3:4, :]
    for j in range(3):
        conv = conv + xp_ref[keep - 3 + j:keep - 3 + j + tb, :] * cw_ref[j:j + 1, :]
    buf_ref[...] = xp_ref[keep + tb - 3:keep + tb, :]
    act = _silu(conv)

    r_i = lax.broadcasted_iota(jnp.int32, (lc, lc), 0)
    c_i = lax.broadcasted_iota(jnp.int32, (lc, lc), 1)
    incl = c_i <= r_i
    strict = c_i < r_i
    ch_g = 2 * hm + hf
    ch_b = ch_g + hg
    scale = HEAD_DIM ** -0.5
    chains = [(c, h) for c in range(cb) for h in range(hg)]
    amats, rhss, qks = [], [], []
    for c, h in chains:
        rows = slice(c * lc, (c + 1) * lc)
        sl = slice(h * HEAD_DIM, (h + 1) * HEAD_DIM)
        gc = gc_ref[rows, :]
        gr = gr_ref[c]
        cq = act[rows, h * HEAD_DIM:(h + 1) * HEAD_DIM]
        ck = act[rows, dg + h * HEAD_DIM:dg + (h + 1) * HEAD_DIM]
        cv = act[rows, 2 * dg + h * HEAD_DIM:2 * dg + (h + 1) * HEAD_DIM]
        qn = cq * lax.rsqrt(jnp.sum(cq * cq, axis=-1, keepdims=True) + NORM_EPS) * scale
        kn = ck * lax.rsqrt(jnp.sum(ck * ck, axis=-1, keepdims=True) + NORM_EPS)
        g_col = gc[:, ch_g + h:ch_g + h + 1]
        g_row = gr[ch_g + h:ch_g + h + 1, :]
        beta = gc[:, ch_b + h:ch_b + h + 1]
        decay = jnp.exp(jnp.where(incl, g_col - g_row, NEG_INF))
        if split_bf16:
            ks = _split(kn)
            kk = _mm3_nt(ks, ks)
            qk = _dot_nt(qn.astype(BF16), ks[0])
        else:
            kk = _dot_nt(kn, kn, HI)
            qk = _dot_nt(qn, kn)
        amats.append(jnp.where(strict, beta * kk * decay, 0.0))
        eg = jnp.exp(g_col)
        rhss.append(jnp.concatenate([kn * (beta * eg), cv * beta], axis=-1))
        g_last = g_col[lc - 1:lc, :]
        qg_ref[rows, sl] = (qn * eg).astype(qg_ref.dtype)
        kd_ref[rows, sl] = (kn * jnp.exp(g_last - g_col)).astype(kd_ref.dtype)
        qks.append((qk * decay).astype(qk_ref.dtype))
        if h == hg - 1:
            qk_ref[rows, :] = jnp.concatenate(qks[-hg:], axis=1)
    sols = _unit_lower_solve(amats, rhss, lc, split_bf16)
    for (c, h), sol in zip(chains, sols):
        rows = slice(c * lc, (c + 1) * lc)
        sl = slice(h * HEAD_DIM, (h + 1) * HEAD_DIM)
        w_ref[rows, sl] = sol[:, :HEAD_DIM].astype(w_ref.dtype)
        u_ref[rows, sl] = sol[:, HEAD_DIM:]


def _gdn_scan_kernel(w_ref, u_ref, qg_ref, kd_ref, qk_ref, z_ref, gc_ref, ng_ref, s0_ref,
                     h_ref, s_ref, *, hm, hf, hg, lc, bpb):
    @pl.when(pl.program_id(1) == 0)
    def _():
        s_ref[...] = s0_ref[...]

    ch_g = 2 * hm + hf
    xd = w_ref.dtype
    chains = [(b, h) for b in range(bpb) for h in range(hg)]
    sls = [slice(h * HEAD_DIM, (h + 1) * HEAD_DIM) for _, h in chains]
    first = []
    for (b, h), sl in zip(chains, sls):
        sx = s_ref[b, h].astype(xd)
        first.append((_dot(w_ref[b, :, sl], sx), _dot(qg_ref[b, :, sl], sx)))
    outs = []
    for (b, h), sl, (ws, qs) in zip(chains, sls, first):
        ux = (u_ref[b, :, sl] - ws).astype(xd)
        outs.append(qs + _dot(qk_ref[b][:, h * lc:(h + 1) * lc], ux))
        g_last = gc_ref[b, lc - SUBLANES:lc, :][SUBLANES - 1:SUBLANES, ch_g + h:ch_g + h + 1]
        s_ref[b, h] = jnp.exp(g_last) * s_ref[b, h] + _dot_tn(kd_ref[b, :, sl], ux)
    for (b, h), sl, o in zip(chains, sls, outs):
        on = o * lax.rsqrt(jnp.mean(o * o, axis=-1, keepdims=True) + NORM_EPS) * ng_ref[...]
        h_ref[b, :, sl] = (on * _silu(z_ref[b, :, sl])).astype(h_ref.dtype)


def _gdn(proj, gc, grc, conv_w, norm_g, s0, b0, layer, nb, t, bpb, dims):
    hm, hf, hg = dims
    dg = hg * HEAD_DIM
    lc = math.gcd(t, CHUNK)
    nc = t // lc
    xd = BF16 if lc >= 16 else F32
    base = (4 * hm * HEAD_DIM + 3 * hf * HEAD_DIM) // dg
    cb = _pick(nc, (2, 1))
    tb = cb * lc
    nblk = t // tb
    m = nb * t
    n_all = proj.shape[-1]
    proj2 = proj.reshape(m, n_all)

    def col(j):
        return pl.BlockSpec((tb, dg), lambda b, i: (b * nblk + i, base + j))

    def prev(j):
        return pl.BlockSpec((SUBLANES, dg),
                            lambda b, i: (jnp.maximum((b * t + i * tb) // SUBLANES - 1, 0), base + j))

    def rowblk(width):
        return pl.BlockSpec((tb, width), lambda b, i: (b * nblk + i, 0))

    w, u0, qg, kd, qk, b1 = pl.pallas_call(
        functools.partial(_gdn_prep_kernel, hm=hm, hf=hf, hg=hg, lc=lc, cb=cb),
        grid=(nb, nblk),
        in_specs=[col(0), col(1), col(2), prev(0), prev(1), prev(2),
                  pl.BlockSpec((None, 3, 3 * dg), lambda b, i: (b, 0, 0)),
                  rowblk(LANES),
                  pl.BlockSpec((None, cb, 2 * GATE_ROWS, lc), lambda b, i: (b, i, 0, 0)),
                  pl.BlockSpec((None, 4, 3 * dg), lambda b, i: (layer, 0, 0))],
        out_specs=[rowblk(dg), rowblk(dg), rowblk(dg), rowblk(dg), rowblk(hg * lc),
                   pl.BlockSpec((None, 3, 3 * dg), lambda b, i: (b, 0, 0))],
        out_shape=[jax.ShapeDtypeStruct((m, dg), xd), jax.ShapeDtypeStruct((m, dg), F32),
                   jax.ShapeDtypeStruct((m, dg), xd), jax.ShapeDtypeStruct((m, dg), xd),
                   jax.ShapeDtypeStruct((m, hg * lc), xd), jax.ShapeDtypeStruct((nb, 3, 3 * dg), F32)],
        scratch_shapes=[pltpu.VMEM((tb + SUBLANES, 3 * dg), F32)],
        name="gdn_prep",
        compiler_params=_params("arbitrary", "arbitrary"),
    )(proj2, proj2, proj2, proj2, proj2, proj2, b0, gc.reshape(m, LANES), grc, conv_w)

    def seq(width):
        return pl.BlockSpec((bpb, lc, width), lambda b, c: (b, c, 0))

    st4 = pl.BlockSpec((bpb, hg, HEAD_DIM, HEAD_DIM), lambda b, c: (b, 0, 0, 0))
    hgx, s1 = pl.pallas_call(
        functools.partial(_gdn_scan_kernel, hm=hm, hf=hf, hg=hg, lc=lc, bpb=bpb),
        grid=(nb // bpb, nc),
        in_specs=[seq(dg), seq(dg), seq(dg), seq(dg), seq(hg * lc),
                  pl.BlockSpec((bpb, lc, dg), lambda b, c: (b, c, base + 3)),
                  seq(LANES),
                  pl.BlockSpec((None, 1, HEAD_DIM), lambda b, c: (layer, 0, 0)),
                  st4],
        out_specs=[seq(dg), st4],
        out_shape=[jax.ShapeDtypeStruct((nb, t, dg), xd),
                   jax.ShapeDtypeStruct((nb, hg, HEAD_DIM, HEAD_DIM), F32)],
        name="gdn_scan",
        compiler_params=_params("arbitrary", "arbitrary"),
    )(w.reshape(nb, t, dg), u0.reshape(nb, t, dg), qg.reshape(nb, t, dg), kd.reshape(nb, t, dg),
      qk.reshape(nb, t, hg * lc), proj, gc, norm_g, s0)
    return hgx, s1, b1


def _fox_prompt_kernel(q_ref, k_ref, v_ref, gr_ref, o_ref, q_s, m_s, l_s, acc_s, *, hf, tb):
    qi = pl.program_id(1)
    ki = pl.program_id(2)
    rep = tb // LANES

    @pl.when(ki == 0)
    def _():
        q_s[...] = (q_ref[...] * HEAD_DIM ** -0.5).astype(BF16)
        m_s[...] = jnp.full(m_s.shape, NEG_INF, F32)
        l_s[...] = jnp.zeros(l_s.shape, F32)
        acc_s[...] = jnp.zeros(acc_s.shape, F32)

    def block(masked):
        if masked:
            visible = (lax.broadcasted_iota(jnp.int32, (tb, tb), 1) <= lax.broadcasted_iota(jnp.int32, (tb, tb), 0))
        for h in range(hf):
            sl = slice(h * HEAD_DIM, (h + 1) * HEAD_DIM)
            k = k_ref[:, sl].astype(BF16)
            v = v_ref[:, sl].astype(BF16)
            s = _dot_nt(q_s[:, sl], k) - gr_ref[h:h + 1, :]
            if masked:
                s = jnp.where(visible, s, NEG_INF)
            m_prev = m_s[h]
            m_new = jnp.maximum(m_prev, jnp.max(s, axis=-1, keepdims=True))
            alpha = jnp.exp(m_prev - m_new)
            p = jnp.exp(s - jnp.concatenate([m_new] * rep, axis=1))
            l_s[h] = alpha * l_s[h] + jnp.sum(p, axis=-1, keepdims=True)
            acc_s[:, sl] = alpha * acc_s[:, sl] + _dot(p.astype(BF16), v)
            m_s[h] = m_new

    @pl.when(ki < qi)
    def _():
        block(False)

    @pl.when(ki == qi)
    def _():
        block(True)
        for h in range(hf):
            sl = slice(h * HEAD_DIM, (h + 1) * HEAD_DIM)
            o_ref[:, sl] = (acc_s[:, sl] / l_s[h]).astype(BF16)


def _fox_prompt(proj, grf, nb, t, dims):
    hm, hf, hg = dims
    df = hf * HEAD_DIM
    base = (4 * hm * HEAD_DIM) // df
    tb = _pick(t, (512, 256, 128))
    nq = t // tb
    return pl.pallas_call(
        functools.partial(_fox_prompt_kernel, hf=hf, tb=tb),
        grid=(nb, nq, nq),
        in_specs=[pl.BlockSpec((tb, df), lambda b, i, j: (b * nq + i, base)),
                  pl.BlockSpec((tb, df), lambda b, i, j: (b * nq + jnp.minimum(j, i), base + 1)),
                  pl.BlockSpec((tb, df), lambda b, i, j: (b * nq + jnp.minimum(j, i), base + 2)),
                  pl.BlockSpec((None, hf, tb), lambda b, i, j: (b, 0, jnp.minimum(j, i)))],
        out_specs=pl.BlockSpec((tb, df), lambda b, i, j: (b * nq + i, 0)),
        out_shape=jax.ShapeDtypeStruct((nb * t, df), BF16),
        scratch_shapes=[pltpu.VMEM((tb, df), BF16), pltpu.VMEM((hf, tb, LANES), F32),
                        pltpu.VMEM((hf, tb, LANES), F32), pltpu.VMEM((tb, df), F32)],
        name="fox_prompt",
        compiler_params=_params("arbitrary", "arbitrary", "arbitrary"),
    )(proj, proj, proj, grf)


def _fox_pool_kernel(x_ref, w_ref, t_ref, *, hf, page):
    n = hf * page
    x = x_ref[...]
    x1 = x.astype(BF16)
    r1 = x - x1.astype(F32)
    x2 = r1.astype(BF16)
    x3 = (r1 - x2.astype(F32)).astype(BF16)
    ri = lax.broadcasted_iota(jnp.int32, (n, n), 0)
    ci = lax.broadcasted_iota(jnp.int32, (n, n), 1)
    later = ((ri // page) == (ci % hf)) & ((ri % page) > (ci // hf))
    m_suf = jnp.where(later, 1.0, 0.0).astype(BF16)
    rj = lax.broadcasted_iota(jnp.int32, (n, LANES), 0)
    lj = lax.broadcasted_iota(jnp.int32, (n, LANES), 1)
    m_tot = jnp.where((rj // page) == (lj % hf), 1.0, 0.0).astype(BF16)
    w_ref[...] = _dot(x1, m_suf) + (_dot(x2, m_suf) + _dot(x3, m_suf))
    t_ref[...] = _dot(x1, m_tot) + (_dot(x2, m_tot) + _dot(x3, m_tot))


def _fox_pool(logf_hs, hf, page):
    r, n = logf_hs.shape
    rb = _pick(r, (512, 256, 128, 64, 32, 16, 8))
    return pl.pallas_call(
        functools.partial(_fox_pool_kernel, hf=hf, page=page),
        grid=(r // rb,),
        in_specs=[pl.BlockSpec((rb, n), lambda i: (i, 0))],
        out_specs=[pl.BlockSpec((rb, n), lambda i: (i, 0)), pl.BlockSpec((rb, LANES), lambda i: (i, 0))],
        out_shape=[jax.ShapeDtypeStruct((r, n), F32), jax.ShapeDtypeStruct((r, LANES), F32)],
        name="fox_pool",
        compiler_params=_params("arbitrary"),
    )(logf_hs)


def _fox_sample_kernel(pt_ref, q_ref, cr_ref, kn_ref, vn_ref, *rest, hf, nsteps, rpp, npp):
    kp = rest[0:npp]
    vp = rest[npp:2 * npp]
    wp = rest[2 * npp:3 * npp]
    tp = rest[3 * npp:4 * npp]
    o_ref, m_s, l_s, acc_s, carry_s = rest[4 * npp:]
    p = pl.program_id(1)
    q = (q_ref[...] * HEAD_DIM ** -0.5).astype(BF16)
    nr = q.shape[0]

    @pl.when(p == 0)
    def _():
        ri = lax.broadcasted_iota(jnp.int32, (nr, nr), 0)
        ci = lax.broadcasted_iota(jnp.int32, (nr, nr), 1)
        ok = ((ri % hf) == (ci % hf)) & ((ci // hf) <= (ri // hf))
        s = jnp.where(ok, _dot_nt(q, kn_ref[...].astype(BF16)) - cr_ref[...], NEG_INF)
        m0 = jnp.max(s, axis=-1, keepdims=True)
        e = jnp.exp(s - m0)
        m_s[...] = m0
        l_s[...] = jnp.sum(e, axis=-1, keepdims=True)
        acc_s[...] = _dot(e.astype(BF16), vn_ref[...].astype(BF16))
        carry_s[...] = jnp.zeros_like(carry_s)

    ncol = rpp * LANES
    ri = lax.broadcasted_iota(jnp.int32, (nr, ncol), 0)
    ci = lax.broadcasted_iota(jnp.int32, (nr, ncol), 1)
    head_mask = jnp.where((ri % hf) == (ci % hf), 0.0, NEG_INF).astype(F32)
    carry = carry_s[...]
    scores = []
    for g in range(npp):
        bias_row = jnp.concatenate([wp[g][r:r + 1, :] + carry for r in range(rpp)], axis=1)
        scores.append(_dot_nt(q, kp[g][...].astype(BF16)) + (head_mask + bias_row))
        carry = carry + tp[g][...]
    carry_s[...] = carry
    top = scores[0]
    for g in range(1, npp):
        top = jnp.maximum(top, scores[g])
    m_prev = m_s[...]
    m_new = jnp.maximum(m_prev, jnp.max(top, axis=-1, keepdims=True))
    alpha = jnp.exp(m_prev - m_new)
    acc = alpha * acc_s[...]
    tot = None
    for g in range(npp):
        e = jnp.exp(scores[g] - m_new)
        tot = e if tot is None else tot + e
        acc = acc + _dot(e.astype(BF16), vp[g][...].astype(BF16))
    l_s[...] = alpha * l_s[...] + jnp.sum(tot, axis=-1, keepdims=True)
    acc_s[...] = acc
    m_s[...] = m_new

    @pl.when(p == nsteps - 1)
    def _():
        o_ref[...] = (acc / l_s[...]).astype(BF16)


def _fox_sample(page_table, q2, cum_row, kn2, vn2, kpool, vpool, wpool, tpool, layer, hf):
    nb, npg = page_table.shape
    nr = q2.shape[1]
    nphys, pcols = kpool.shape[1], kpool.shape[2]
    rpp = pcols // LANES
    npp = _pick(npg, (16, 8, 4, 2, 1))
    nsteps = npg // npp

    def page(b, p, pt, g):
        return pt[b * npg + (npg - 1 - (p * npp + g))]

    per_b3 = lambda b, p, pt: (b, 0, 0)
    kv_specs = [pl.BlockSpec((None, None, pcols, HEAD_DIM),
                             lambda b, p, pt, g=g: (layer, page(b, p, pt, g), 0, 0)) for g in range(npp)]
    w_specs = [pl.BlockSpec((None, rpp, LANES),
                            lambda b, p, pt, g=g: (layer * nphys + page(b, p, pt, g), 0, 0)) for g in range(npp)]
    t_specs = [pl.BlockSpec((None, 1, LANES),
                            lambda b, p, pt, g=g: (layer * nphys + page(b, p, pt, g), 0, 0)) for g in range(npp)]
    return pl.pallas_call(
        functools.partial(_fox_sample_kernel, hf=hf, nsteps=nsteps, rpp=rpp, npp=npp),
        grid_spec=pltpu.PrefetchScalarGridSpec(
            num_scalar_prefetch=1,
            grid=(nb, nsteps),
            in_specs=[pl.BlockSpec((None, nr, HEAD_DIM), per_b3),
                      pl.BlockSpec((None, 1, nr), per_b3),
                      pl.BlockSpec((None, nr, HEAD_DIM), per_b3),
                      pl.BlockSpec((None, nr, HEAD_DIM), per_b3)] + kv_specs + kv_specs + w_specs + t_specs,
            out_specs=pl.BlockSpec((None, nr, HEAD_DIM), per_b3),
            scratch_shapes=[pltpu.VMEM((nr, 1), F32), pltpu.VMEM((nr, 1), F32),
                            pltpu.VMEM((nr, HEAD_DIM), F32), pltpu.VMEM((1, LANES), F32)]),
        out_shape=jax.ShapeDtypeStruct((nb, nr, HEAD_DIM), BF16),
        name="fox_decode",
        compiler_params=_params("arbitrary", "arbitrary"),
    )(page_table.reshape(-1), q2, cum_row, kn2, vn2, *([kpool] * npp), *([vpool] * npp),
      *([wpool] * npp), *([tpool] * npp))


def _regroup_w_in(w_in, b_in, dims):
    hm, hf, hg = dims
    dm, df, dg = hm * HEAD_DIM, hf * HEAD_DIM, hg * HEAD_DIM
    sizes = (dm, dm, dm, dm, hm, hm, df, df, df, hf, dg, dg, dg, dg, hg, hg)
    offs = [0]
    for s in sizes:
        offs.append(offs[-1] + s)
    order = (0, 1, 2, 3, 6, 7, 8, 10, 11, 12, 13, 4, 5, 9, 14, 15)
    n_main = 4 * dm + 3 * df + 4 * dg
    n_gate = 2 * hm + hf + 2 * hg
    n_tot = -(-(n_main + LANES) // 512) * 512
    pad = n_tot - n_main - n_gate

    def regroup(a, axis):
        parts = [lax.slice_in_dim(a, offs[i], offs[i + 1], axis=axis) for i in order]
        zshape = list(a.shape)
        zshape[axis] = pad
        parts.append(jnp.zeros(zshape, a.dtype))
        return jnp.concatenate(parts, axis=axis)

    w_t = regroup(jnp.swapaxes(w_in, 1, 2).astype(BF16), 1)
    return w_t, regroup(b_in, 1)[:, None, :], n_main // LANES


def _gate_param_cols(a_log, dt_bias, dims):
    hm, hf, hg = dims
    depth = a_log.shape[0]
    lead = 2 * hm + hf
    z0 = jnp.zeros((depth, lead), F32)
    z1 = jnp.zeros((depth, GATE_ROWS - lead - hg), F32)
    a_col = jnp.concatenate([z0, a_log, z1], axis=1)[:, :, None]
    dt_col = jnp.concatenate([z0, dt_bias, z1], axis=1)[:, :, None]
    return a_col, dt_col


def kernel(x_prompt, x_sample, cache_fox_k, cache_fox_v, cache_fox_logf, state_mlstm_C, state_mlstm_n, state_mlstm_m, state_gdn_S, state_gdn_conv, page_table, c_prompt, c_sample, w_ada, b_ada, ln_g, ln_b, ffn_w_gate, ffn_w_up, ffn_w_down, w_in, b_in, mlstm_norm_g, gdn_conv_w, gdn_A_log, gdn_dt_bias, gdn_norm_g, w_out):
    bp, tp, d = x_prompt.shape
    bs, ts, _ = x_sample.shape
    depth = w_ada.shape[0]
    hm, hf, hg = state_mlstm_C.shape[2], cache_fox_k.shape[3], state_gdn_S.shape[2]
    dims = (hm, hf, hg)
    dm, df, dg = hm * HEAD_DIM, hf * HEAD_DIM, hg * HEAD_DIM
    nphys, page = cache_fox_k.shape[1], cache_fox_k.shape[2]
    alpha = (2 * depth) ** 0.25
    assert 2 * hm + hf + 2 * hg <= GATE_ROWS and (page * hf) % LANES == 0 and LANES % hf == 0

    nrow = -(-(bp + bs) // 16) * 16
    c_all = jnp.concatenate([c_prompt, c_sample, jnp.zeros((nrow - bp - bs, d), F32)], axis=0)
    mod = _ada(c_all, w_ada, b_ada).reshape(depth, nrow, 3 * N_SUB, d)
    mod = jnp.transpose(mod, (0, 2, 1, 3))
    mod_p = mod[:, :, :bp, None, :]
    mod_s = jnp.repeat(mod[:, :, bp:bp + bs], ts, axis=2)[:, :, None, :, :]

    w_in_r, b_in_r, gate_blk = _regroup_w_in(w_in, b_in, dims)
    a_col, dt_col = _gate_param_cols(gdn_A_log, gdn_dt_bias, dims)
    ln_g4 = ln_g[:, :, None, :]
    ln_b4 = ln_b[:, :, None, :]
    norm_m = mlstm_norm_g[:, :, None, :]
    norm_g = gdn_norm_g[:, None, :]

    rpp = (page * hf) // LANES
    logf_hs = jnp.transpose(cache_fox_logf, (0, 1, 3, 2)).reshape(depth * nphys, hf * page)
    wpool, tpool = _fox_pool(logf_hs, hf, page)
    wpool = wpool.reshape(depth * nphys, rpp, LANES)
    tpool = tpool.reshape(depth * nphys, 1, LANES)
    kpool = cache_fox_k.reshape(depth, nphys, page * hf, HEAD_DIM)
    vpool = cache_fox_v.reshape(depth, nphys, page * hf, HEAD_DIM)

    o_f = 4 * dm
    o_g = o_f + 3 * df

    def mixers(l, h, nb, t, mlstm_state, gdn_state, sample):
        m_rows = nb * t
        proj = _win(h, w_in_r, b_in_r, l)
        lc = math.gcd(t, CHUNK)
        if sample:
            gc, grc, grf = _gates(proj, gate_blk, a_col, dt_col, l, 1, m_rows, lc, t, dims)
            grc = grc.reshape(nb, t // lc, 2 * GATE_ROWS, lc)
        else:
            gc, grc, grf = _gates(proj, gate_blk, a_col, dt_col, l, nb, t, lc, t, dims)
        c0, n0, m0 = mlstm_state
        proj3 = proj.reshape(nb, t, proj.shape[-1])
        gc3 = gc.reshape(nb, t, LANES)
        bpb = 1 if sample else nb
        hmx, c1, n1, m1 = _mlstm(proj3, gc3, grc, norm_m, c0, n0[:, :, None, :], m0[:, :, None, None],
                                 l, nb, t, bpb, dims)
        s0, b0 = gdn_state
        hgx, s1, b1 = _gdn(proj3, gc3, grc, gdn_conv_w, norm_g, s0, b0, l, nb, t, bpb, dims)
        hmx = hmx.reshape(m_rows, dm)
        hgx = hgx.reshape(m_rows, dg)
        fk = proj[:, o_f + df:o_f + 2 * df]
        fv = proj[:, o_f + 2 * df:o_f + 3 * df]
        logf = gc[:, GATE_ROWS + 2 * hm:GATE_ROWS + 2 * hm + hf]
        if sample:
            nr = t * hf
            q2 = proj[:, o_f:o_f + df].reshape(nb, nr, HEAD_DIM)
            cum = gc[:, 2 * hm:2 * hm + hf].reshape(nb, nr)
            hfx = _fox_sample(page_table, q2, cum[:, None, :], fk.reshape(nb, nr, HEAD_DIM),
                              fv.reshape(nb, nr, HEAD_DIM), kpool, vpool, wpool, tpool, l, hf)
            hfx = hfx.reshape(m_rows, df)
        else:
            hfx = _fox_prompt(proj, grf, nb, t, dims)
        new_state = (fk.reshape(nb, t, hf, HEAD_DIM), fv.reshape(nb, t, hf, HEAD_DIM), logf.reshape(nb, t, hf),
                     c1, n1[:, :, 0, :], m1[:, :, 0, 0], s1, b1)
        return (hmx.astype(BF16), hfx, hgx.astype(BF16)), new_state

    xp = x_prompt.reshape(bp * tp, d)
    xs = x_sample.reshape(bs * ts, d)
    hp = _modulate(xp, mod_p, 0, 0, tp)
    hs = _modulate(xs, mod_s, 0, 0, bs * ts)
    zero_mlstm = (jnp.zeros((bp, hm, HEAD_DIM, HEAD_DIM), F32), jnp.zeros((bp, hm, HEAD_DIM), F32),
                  jnp.zeros((bp, hm), F32))
    zero_gdn = (jnp.zeros((bp, hg, HEAD_DIM, HEAD_DIM), F32), jnp.zeros((bp, 3, 3 * dg), F32))

    def proj_ln(parts_p, parts_s, w, w_prefix, l, j, nxt, coef):
        return _proj_ln(parts_p, parts_s, w, w_prefix, xp, xs, mod_p, mod_s, l, j, nxt, ln_g4, ln_b4, tp, alpha, coef)

    st_p, st_s = [], []
    for l in range(depth):
        ap, as_ = _gateup(hp, hs, ffn_w_gate, ffn_w_up, l, 0)
        (xp, hp), (xs, hs) = proj_ln([ap], [as_], ffn_w_down, (l, 0), l, 0, (l, 1), MACARON_W)
        mixed_p, sp = mixers(l, hp, bp, tp, zero_mlstm, zero_gdn, False)
        mixed_s, ss = mixers(l, hs, bs, ts, (state_mlstm_C[l], state_mlstm_n[l], state_mlstm_m[l]),
                             (state_gdn_S[l], state_gdn_conv[l]), True)
        (xp, hp), (xs, hs) = proj_ln(mixed_p, mixed_s, w_out, (l,), l, 1, (l, 2), 1.0)
        ap, as_ = _gateup(hp, hs, ffn_w_gate, ffn_w_up, l, 1)
        nxt = (l + 1, 0) if l + 1 < depth else None
        (xp, hp), (xs, hs) = proj_ln([ap], [as_], ffn_w_down, (l, 1), l, 2, nxt, MACARON_W)
        st_p.append(sp)
        st_s.append(ss)
    pn = [jnp.stack(a) for a in zip(*st_p)]
    sn = [jnp.stack(a) for a in zip(*st_s)]
    return (xp.reshape(bp, tp, d), xs.reshape(bs, ts, d), pn[0], pn[1], pn[2], pn[3], pn[4], pn[5], pn[6], pn[7],
            sn[0], sn[1], sn[2], sn[3], sn[4], sn[5], sn[6], sn[7])
```

```python
import functools
import math

import jax
import jax.numpy as jnp
from jax import lax
from jax.experimental import pallas as pl
from jax.experimental.pallas import tpu as pltpu

F32 = jnp.float32
BF16 = jnp.bfloat16

HEAD_DIM = 128
LANES = 128
SUBLANES = 8
CHUNK = 64
N_SUB = 3
LN_EPS = 1e-5
NORM_EPS = 1e-6
MACARON_W = 0.5
GATE_ROWS = 32
VMEM_LIMIT_BYTES = 56 * 1024 * 1024
HI = lax.Precision.HIGHEST
NEG_INF = float("-inf")


def _params(*sem):
    return pltpu.CompilerParams(dimension_semantics=sem, vmem_limit_bytes=VMEM_LIMIT_BYTES)


def _dot(a, b, precision=None):
    return jnp.dot(a, b, preferred_element_type=F32, precision=precision)


def _dot_nt(a, b, precision=None):
    return lax.dot_general(a, b, (((1,), (1,)), ((), ())), preferred_element_type=F32, precision=precision)


def _dot_tn(a, b, precision=None):
    return lax.dot_general(a, b, (((0,), (0,)), ((), ())), preferred_element_type=F32, precision=precision)


def _sigmoid(x):
    return 1.0 / (1.0 + jnp.exp(-x))


def _silu(x):
    return x * _sigmoid(x)


def _softplus(x):
    return jnp.maximum(x, 0.0) + jnp.log1p(jnp.exp(-jnp.abs(x)))


def _log_sigmoid(x):
    return -_softplus(-x)


def _pick(m, candidates):
    for c in candidates:
        if m % c == 0:
            return c
    return m


def _ada_kernel(c_ref, w_ref, b_ref, o_ref):
    c = c_ref[...]
    sc = _silu(c).astype(BF16)
    o_ref[...] = _dot(sc, w_ref[...].astype(BF16)) + b_ref[...]


def _ada(c_all, w_ada, b_ada):
    depth, d, n = w_ada.shape
    rows = c_all.shape[0]
    bn = _pick(n, (1024, 512, 256, 128))
    return pl.pallas_call(
        _ada_kernel,
        grid=(depth, n // bn),
        in_specs=[pl.BlockSpec((rows, d), lambda l, j: (0, 0)),
                  pl.BlockSpec((None, d, bn), lambda l, j: (l, 0, j)),
                  pl.BlockSpec((None, 1, bn), lambda l, j: (l, 0, j))],
        out_specs=pl.BlockSpec((None, rows, bn), lambda l, j: (l, 0, j)),
        out_shape=jax.ShapeDtypeStruct((depth, rows, n), F32),
        name="ada",
        compiler_params=_params("arbitrary", "arbitrary"),
    )(c_all, w_ada, b_ada.reshape(depth, 1, n))


def _modulate_kernel(x_ref, sc_ref, sh_ref, o_ref):
    o_ref[...] = (x_ref[...] * (1.0 + sc_ref[...]) + sh_ref[...]).astype(BF16)


def _mod_spec(mod, layer, slot, rows_per_group, bm):
    r, d = mod.shape[3], mod.shape[4]
    if r == 1:
        return pl.BlockSpec((None, None, None, 1, d),
                            lambda m, *_: (layer, slot, (m * bm) // rows_per_group, 0, 0))
    return pl.BlockSpec((None, None, None, bm, d), lambda m, *_: (layer, slot, 0, m, 0))


def _modulate(x, mod, layer, j, rows_per_group):
    m, d = x.shape
    bm = _pick(rows_per_group, (512, 256, 128, 64))
    return pl.pallas_call(
        _modulate_kernel,
        grid=(m // bm,),
        in_specs=[pl.BlockSpec((bm, d), lambda i: (i, 0)),
                  _mod_spec(mod, layer, 3 * j + 1, rows_per_group, bm),
                  _mod_spec(mod, layer, 3 * j + 0, rows_per_group, bm)],
        out_specs=pl.BlockSpec((bm, d), lambda i: (i, 0)),
        out_shape=jax.ShapeDtypeStruct((m, d), BF16),
        name="modulate",
        compiler_params=_params("arbitrary"),
    )(x, mod, mod)


def _gateup_kernel(h_ref, hs_ref, wg_ref, wu_ref, o_ref, os_ref, wg_s, wu_s, *, last):
    i = pl.program_id(1)

    @pl.when(i == 0)
    def _():
        wg_s[...] = wg_ref[...].astype(BF16)
        wu_s[...] = wu_ref[...].astype(BF16)

    def swiglu_half(h):
        g = _dot(h, wg_s[...])
        u = _dot(h, wu_s[...])
        return (_silu(g) * u).astype(BF16)

    o_ref[...] = swiglu_half(h_ref[...])

    @pl.when(i == last)
    def _():
        os_ref[...] = swiglu_half(hs_ref[...])


def _gateup(h, hs, w_gate, w_up, layer, j):
    m, d = h.shape
    ms = hs.shape[0]
    f = w_gate.shape[-1]
    bm = _pick(m, (1024, 512, 256, 128, 64))
    bn = _pick(f, (512, 256, 128))
    wspec = pl.BlockSpec((None, None, d, bn), lambda n, i: (layer, j, 0, n))
    return pl.pallas_call(
        functools.partial(_gateup_kernel, last=m // bm - 1),
        grid=(f // bn, m // bm),
        in_specs=[pl.BlockSpec((bm, d), lambda n, i: (i, 0)), pl.BlockSpec((ms, d), lambda n, i: (0, 0)),
                  wspec, wspec],
        out_specs=[pl.BlockSpec((bm, bn), lambda n, i: (i, n)), pl.BlockSpec((ms, bn), lambda n, i: (0, n))],
        out_shape=[jax.ShapeDtypeStruct((m, f), BF16), jax.ShapeDtypeStruct((ms, f), BF16)],
        scratch_shapes=[pltpu.VMEM((d, bn), BF16), pltpu.VMEM((d, bn), BF16)],
        name="gateup",
        compiler_params=_params("arbitrary", "arbitrary"),
    )(h, hs, w_gate, w_up)


def _proj_ln_kernel(*refs, widths, nk, nm, bk, alpha, coef, emit_h):
    npart = len(widths)
    nin = npart + 2 + (2 if emit_h else 0)
    nout = 2 if emit_h else 1
    w_ref, lng_ref, lnb_ref = refs[0:3]
    ins = [refs[3 + s * nin:3 + (s + 1) * nin] for s in range(2)]
    outs = [refs[3 + 2 * nin + s * nout:3 + 2 * nin + (s + 1) * nout] for s in range(2)]
    w_s = refs[3 + 2 * nin + 2 * nout]
    i = pl.program_id(0)

    @pl.when(i < nk)
    def _():
        w_s[pl.ds(pl.multiple_of(i * bk, bk), bk), :] = w_ref[...].astype(BF16)

    def project(s):
        acc = None
        off = 0
        for a_ref, wd in zip(ins[s][:npart], widths):
            term = _dot(a_ref[...], w_s[off:off + wd, :])
            acc = term if acc is None else acc + term
            off += wd
        return acc

    def finish(s, acc):
        x_ref, gate_ref = ins[s][npart], ins[s][npart + 1]
        y = alpha * x_ref[...] + (coef * (1.0 + gate_ref[...])) * acc
        mu = jnp.mean(y, axis=-1, keepdims=True)
        yc = y - mu
        var = jnp.mean(yc * yc, axis=-1, keepdims=True)
        xn = yc * lax.rsqrt(var + LN_EPS) * lng_ref[...] + lnb_ref[...]
        outs[s][0][...] = xn
        if emit_h:
            sc_ref, sh_ref = ins[s][npart + 2], ins[s][npart + 3]
            outs[s][1][...] = (xn * (1.0 + sc_ref[...]) + sh_ref[...]).astype(BF16)

    @pl.when((i >= nk) & (i < nk + nm))
    def _():
        finish(0, project(0))

    @pl.when(i == nk + nm)
    def _():
        finish(1, project(1))


def _proj_ln(parts_p, parts_s, w, w_prefix, x_p, x_s, mod_p, mod_s, layer, j, nxt, ln_g, ln_b, rows_per_group,
             alpha, coef):
    widths = tuple(a.shape[1] for a in parts_p)
    k = sum(widths)
    m, d = x_p.shape
    ms = x_s.shape[0]
    bm = _pick(rows_per_group, (256, 128, 64))
    bk = _pick(k, (512, 256, 128))
    nk = k // bk
    nm = m // bm
    emit_h = nxt is not None
    npre = len(w_prefix)

    def row(i):
        return jnp.clip(i - nk, 0, nm - 1)

    def modspec_p(lyr, slot):
        return pl.BlockSpec((None, None, None, 1, d),
                            lambda i: (lyr, slot, (row(i) * bm) // rows_per_group, 0, 0))

    def modspec_s(lyr, slot):
        return pl.BlockSpec((None, None, None, ms, d), lambda i: (lyr, slot, 0, 0, 0))

    in_specs = [pl.BlockSpec((None,) * npre + (bk, d), lambda i: tuple(w_prefix) + (jnp.minimum(i, nk - 1), 0)),
                pl.BlockSpec((None, None, 1, d), lambda i: (layer, j, 0, 0)),
                pl.BlockSpec((None, None, 1, d), lambda i: (layer, j, 0, 0))]
    args = [w, ln_g, ln_b]
    in_specs += [pl.BlockSpec((bm, wd), lambda i: (row(i), 0)) for wd in widths]
    in_specs += [pl.BlockSpec((bm, d), lambda i: (row(i), 0)), modspec_p(layer, 3 * j + 2)]
    args += list(parts_p) + [x_p, mod_p]
    if emit_h:
        in_specs += [modspec_p(nxt[0], 3 * nxt[1] + 1), modspec_p(nxt[0], 3 * nxt[1] + 0)]
        args += [mod_p, mod_p]
    in_specs += [pl.BlockSpec((ms, wd), lambda i: (0, 0)) for wd in widths]
    in_specs += [pl.BlockSpec((ms, d), lambda i: (0, 0)), modspec_s(layer, 3 * j + 2)]
    args += list(parts_s) + [x_s, mod_s]
    if emit_h:
        in_specs += [modspec_s(nxt[0], 3 * nxt[1] + 1), modspec_s(nxt[0], 3 * nxt[1] + 0)]
        args += [mod_s, mod_s]
    out_specs = [pl.BlockSpec((bm, d), lambda i: (row(i), 0))]
    out_shape = [jax.ShapeDtypeStruct((m, d), F32)]
    if emit_h:
        out_specs.append(pl.BlockSpec((bm, d), lambda i: (row(i), 0)))
        out_shape.append(jax.ShapeDtypeStruct((m, d), BF16))
    out_specs.append(pl.BlockSpec((ms, d), lambda i: (0, 0)))
    out_shape.append(jax.ShapeDtypeStruct((ms, d), F32))
    if emit_h:
        out_specs.append(pl.BlockSpec((ms, d), lambda i: (0, 0)))
        out_shape.append(jax.ShapeDtypeStruct((ms, d), BF16))
    out = pl.pallas_call(
        functools.partial(_proj_ln_kernel, widths=widths, nk=nk, nm=nm, bk=bk, alpha=alpha, coef=coef,
                          emit_h=emit_h),
        grid=(nk + nm + 1,),
        in_specs=in_specs,
        out_specs=out_specs,
        out_shape=out_shape,
        scratch_shapes=[pltpu.VMEM((k, d), BF16)],
        name="proj_ln",
        compiler_params=_params("arbitrary"),
    )(*args)
    if emit_h:
        return (out[0], out[1]), (out[2], out[3])
    return (out[0], None), (out[1], None)


def _win_kernel(h_ref, w_ref, b_ref, o_ref):
    o_ref[...] = _dot_nt(h_ref[...], w_ref[...]) + b_ref[...]


def _win(h, w, b, layer):
    m, d = h.shape
    n = w.shape[1]
    bm = _pick(m, (2048, 1024, 512, 256, 128, 64))
    bn = _pick(n, (768, 512, 256, 128))
    return pl.pallas_call(
        _win_kernel,
        grid=(m // bm, n // bn),
        in_specs=[pl.BlockSpec((bm, d), lambda i, c: (i, 0)),
                  pl.BlockSpec((None, bn, d), lambda i, c: (layer, c, 0)),
                  pl.BlockSpec((None, 1, bn), lambda i, c: (layer, 0, c))],
        out_specs=pl.BlockSpec((bm, bn), lambda i, c: (i, c)),
        out_shape=jax.ShapeDtypeStruct((m, n), F32),
        name="w_in",
        compiler_params=_params("arbitrary", "arbitrary"),
    )(h, w, b)


def _gates_kernel(x_ref, a_ref, dt_ref, gc_ref, grc_ref, grf_ref, carry_ref, *, hm, hf, hg, lc, seg, tb):
    t = pl.program_id(1)

    @pl.when(t == 0)
    def _():
        carry_ref[...] = jnp.zeros_like(carry_ref)

    z = jnp.transpose(x_ref[...])[:GATE_ROWS]
    ch = lax.broadcasted_iota(jnp.int32, (GATE_ROWS, 1), 0)
    is_mi = ch < hm
    is_mf = (ch >= hm) & (ch < 2 * hm)
    is_ff = (ch >= 2 * hm) & (ch < 2 * hm + hf)
    is_ga = (ch >= 2 * hm + hf) & (ch < 2 * hm + hf + hg)
    is_gb = (ch >= 2 * hm + hf + hg) & (ch < 2 * hm + hf + 2 * hg)
    ls = _log_sigmoid(z)
    gg = -jnp.exp(a_ref[...]) * _softplus(z + dt_ref[...])
    sg = _sigmoid(z)
    s_i = lax.broadcasted_iota(jnp.int32, (tb, tb), 0)
    t_i = lax.broadcasted_iota(jnp.int32, (tb, tb), 1)
    m_loc = jnp.where((s_i <= t_i) & ((s_i // lc) == (t_i // lc)), 1.0, 0.0).astype(F32)
    m_seg = jnp.where((s_i <= t_i) & ((s_i // seg) == (t_i // seg)), 1.0, 0.0).astype(F32)
    v_loc = jnp.where(is_mf, ls, jnp.where(is_ga, gg, 0.0))
    v_seg = jnp.where(is_ff, ls, 0.0)
    cum_loc = _dot(v_loc, m_loc, HI)
    cum_seg = _dot(v_seg, m_seg, HI) + carry_ref[...]
    if seg > tb:
        carry_ref[...] = cum_seg[:, tb - 1:tb]
    bank0 = jnp.where(is_mi, z, jnp.where(is_mf | is_ga, cum_loc,
                                          jnp.where(is_ff, cum_seg, jnp.where(is_gb, sg, 0.0))))
    bank1 = jnp.where(is_ff, ls, 0.0)
    rows = jnp.concatenate([bank0, bank1], axis=0)
    full = jnp.concatenate([rows, jnp.zeros((LANES - 2 * GATE_ROWS, tb), F32)], axis=0)
    gc_ref[...] = jnp.transpose(full)
    for c in range(tb // lc):
        grc_ref[c] = rows[:, c * lc:(c + 1) * lc]
    grf_ref[...] = bank0[2 * hm:2 * hm + hf]


def _gates(proj, gate_blk, a_col, dt_col, layer, nb, t, lc, seg, dims):
    hm, hf, hg = dims
    tb = _pick(t, (512, 256, 128, 64))
    nt = t // tb
    return pl.pallas_call(
        functools.partial(_gates_kernel, hm=hm, hf=hf, hg=hg, lc=lc, seg=seg, tb=tb),
        grid=(nb, nt),
        in_specs=[pl.BlockSpec((tb, LANES), lambda b, i: (b * nt + i, gate_blk)),
                  pl.BlockSpec((None, GATE_ROWS, 1), lambda b, i: (layer, 0, 0)),
                  pl.BlockSpec((None, GATE_ROWS, 1), lambda b, i: (layer, 0, 0))],
        out_specs=[pl.BlockSpec((tb, LANES), lambda b, i: (b * nt + i, 0)),
                   pl.BlockSpec((None, tb // lc, 2 * GATE_ROWS, lc), lambda b, i: (b, i, 0, 0)),
                   pl.BlockSpec((None, hf, tb), lambda b, i: (b, 0, i))],
        out_shape=[jax.ShapeDtypeStruct((nb * t, LANES), F32),
                   jax.ShapeDtypeStruct((nb, t // lc, 2 * GATE_ROWS, lc), F32),
                   jax.ShapeDtypeStruct((nb, hf, t), F32)],
        scratch_shapes=[pltpu.VMEM((GATE_ROWS, 1), F32)],
        name="gates",
        compiler_params=_params("arbitrary", "arbitrary"),
    )(proj, a_col, dt_col)


def _mlstm_kernel(q_ref, k_ref, v_ref, o_ref, gc_ref, gr_ref, ng_ref, c0_ref, n0_ref, m0_ref,
                  h_ref, c_ref, n_ref, m_ref, *, hm, lc, bpb, mxu_dtype):
    @pl.when(pl.program_id(1) == 0)
    def _():
        c_ref[...] = c0_ref[...]
        n_ref[...] = n0_ref[...]
        m_ref[...] = m0_ref[...]

    r_i = lax.broadcasted_iota(jnp.int32, (lc, lc), 0)
    c_i = lax.broadcasted_iota(jnp.int32, (lc, lc), 1)
    causal = c_i <= r_i
    scale = HEAD_DIM ** -0.5
    chains = [(b, h) for b in range(bpb) for h in range(hm)]
    sls = [slice(h * HEAD_DIM, (h + 1) * HEAD_DIM) for _, h in chains]
    first = []
    for (b, h), sl in zip(chains, sls):
        q = q_ref[b, :, sl]
        k = k_ref[b, :, sl] * scale
        qx = q.astype(mxu_dtype)
        vx = v_ref[b, :, sl].astype(mxu_dtype)
        first.append(dict(q=q, k=k, vx=vx, qk=_dot_nt(qx, k.astype(mxu_dtype)),
                          qc=_dot(qx, c_ref[b, h].astype(mxu_dtype))))
    gate = []
    for b, h in chains:
        gc = gc_ref[b]
        gr = gr_ref[b]
        li_col = gc[:, h:h + 1]
        b_col = gc[:, hm + h:hm + h + 1]
        li_row = gr[h:h + 1, :]
        b_row = gr[hm + h:hm + h + 1, :]
        mst = m_ref[b, h]
        dmat = jnp.where(causal, b_col - b_row + li_row, NEG_INF)
        inter = b_col + mst
        b_last = b_col[lc - 1:lc, :]
        d_last = b_last - b_col + li_col
        gate.append(dict(dmat=dmat, inter=inter, d_last=d_last, carry=b_last + mst,
                         dmax=jnp.max(dmat, axis=-1, keepdims=True), lmax=jnp.max(d_last, axis=0, keepdims=True)))
    for g in gate:
        m_row = jnp.maximum(g["inter"], g["dmax"])
        m_new = jnp.maximum(g["carry"], g["lmax"])
        g.update(m_row=m_row, w_intra=jnp.exp(g["dmat"] - m_row), w_inter=jnp.exp(g["inter"] - m_row),
                 m_new=m_new, w_k=jnp.exp(g["d_last"] - m_new), decay=jnp.exp(g["carry"] - m_new))
    outs = []
    for (b, h), g, f in zip(chains, gate, first):
        cst = c_ref[b, h]
        nst = n_ref[b, h]
        s = f["qk"] * g["w_intra"]
        num = _dot(s.astype(mxu_dtype), f["vx"]) + g["w_inter"] * f["qc"]
        den = jnp.sum(s, axis=-1, keepdims=True) + g["w_inter"] * jnp.sum(f["q"] * nst, axis=-1, keepdims=True)
        outs.append(num / jnp.maximum(jnp.abs(den), jnp.exp(-g["m_row"])))
        kw = f["k"] * g["w_k"]
        c_ref[b, h] = g["decay"] * cst + _dot_tn(kw.astype(mxu_dtype), f["vx"])
        n_ref[b, h] = g["decay"] * nst + jnp.sum(kw, axis=0, keepdims=True)
        m_ref[b, h] = g["m_new"]
    for (b, h), sl, hh in zip(chains, sls, outs):
        mu = jnp.mean(hh, axis=-1, keepdims=True)
        hc = hh - mu
        var = jnp.mean(hc * hc, axis=-1, keepdims=True)
        hn = hc * lax.rsqrt(var + LN_EPS) * ng_ref[h]
        h_ref[b, :, sl] = (hn * _sigmoid(o_ref[b, :, sl])).astype(h_ref.dtype)


def _mlstm(proj, gc, grc, norm_g, c0, n0, m0, layer, nb, t, bpb, dims):
    hm = dims[0]
    dm = hm * HEAD_DIM
    lc = math.gcd(t, CHUNK)
    nc = t // lc
    mxu_dtype = BF16 if lc >= 16 else F32

    def col(j):
        return pl.BlockSpec((bpb, lc, dm), lambda b, c: (b, c, j))

    st4 = pl.BlockSpec((bpb, hm, HEAD_DIM, HEAD_DIM), lambda b, c: (b, 0, 0, 0))
    stn = pl.BlockSpec((bpb, hm, 1, HEAD_DIM), lambda b, c: (b, 0, 0, 0))
    stm = pl.BlockSpec((bpb, hm, 1, 1), lambda b, c: (b, 0, 0, 0))
    return pl.pallas_call(
        functools.partial(_mlstm_kernel, hm=hm, lc=lc, bpb=bpb, mxu_dtype=mxu_dtype),
        grid=(nb // bpb, nc),
        in_specs=[col(0), col(1), col(2), col(3),
                  pl.BlockSpec((bpb, lc, LANES), lambda b, c: (b, c, 0)),
                  pl.BlockSpec((bpb, None, 2 * GATE_ROWS, lc), lambda b, c: (b, c, 0, 0)),
                  pl.BlockSpec((None, hm, 1, HEAD_DIM), lambda b, c: (layer, 0, 0, 0)),
                  st4, stn, stm],
        out_specs=[pl.BlockSpec((bpb, lc, dm), lambda b, c: (b, c, 0)), st4, stn, stm],
        out_shape=[jax.ShapeDtypeStruct((nb, t, dm), mxu_dtype),
                   jax.ShapeDtypeStruct((nb, hm, HEAD_DIM, HEAD_DIM), F32),
                   jax.ShapeDtypeStruct((nb, hm, 1, HEAD_DIM), F32),
                   jax.ShapeDtypeStruct((nb, hm, 1, 1), F32)],
        name="mlstm",
        compiler_params=_params("arbitrary", "arbitrary"),
    )(proj, proj, proj, proj, gc, grc, norm_g, c0, n0, m0)


def _split(a):
    hi = a.astype(BF16)
    return hi, (a - hi.astype(F32)).astype(BF16)


def _mm3(a, b):
    return _dot(a[0], b[0]) + (_dot(a[0], b[1]) + _dot(a[1], b[0]))


def _mm3_nt(a, b):
    return _dot_nt(a[0], b[0]) + (_dot_nt(a[0], b[1]) + _dot_nt(a[1], b[0]))


def _unit_lower_solve(a, rhs, lc, split_bf16):
    r_i = lax.broadcasted_iota(jnp.int32, (lc, lc), 0)
    c_i = lax.broadcasted_iota(jnp.int32, (lc, lc), 1)
    eye = jnp.where(r_i == c_i, 1.0, 0.0).astype(F32)
    if split_bf16:
        prep, mm = _split, _mm3
    else:
        prep, mm = (lambda x: x), (lambda x, y: _dot(x, y, HI))
    npows = [-x for x in a]
    invs = [eye + n for n in npows]
    width = 2
    while width < lc:
        ns = [prep(n) for n in npows]
        npows = [mm(n, n) for n in ns]
        invs = [inv + mm(prep(inv), prep(n)) for inv, n in zip(invs, npows)]
        width *= 2
    return [mm(prep(inv), prep(r)) for inv, r in zip(invs, rhs)]


def _gdn_prep_kernel(q_ref, k_ref, v_ref, pq_ref, pk_ref, pv_ref, b0_ref, gc_ref, gr_ref, cw_ref,
                     w_ref, u_ref, qg_ref, kd_ref, qk_ref, buf_ref, xp_ref, *, hm, hf, hg, lc, cb):
    dg = hg * HEAD_DIM
    keep = SUBLANES
    tb = cb * lc
    split_bf16 = lc >= 16
    prev = jnp.concatenate([pq_ref[keep - 3:keep, :], pk_ref[keep - 3:keep, :], pv_ref[keep - 3:keep, :]], axis=1)
    xp_ref[keep - 3:keep, :] = jnp.where(pl.program_id(1) == 0, b0_ref[...], prev)
    xp_ref[keep:keep + tb, 0:dg] = q_ref[...]
    xp_ref[keep:keep + tb, dg:2 * dg] = k_ref[...]
    xp_ref[keep:keep + tb, 2 * dg:3 * dg] = v_ref[...]
    conv = xp_ref[keep:keep + tb, :] * cw_ref[3:4, :]
    for j in range(3):
        conv = conv + xp_ref[keep - 3 + j:keep - 3 + j + tb, :] * cw_ref[j:j + 1, :]
    buf_ref[...] = xp_ref[keep + tb - 3:keep + tb, :]
    act = _silu(conv)

    r_i = lax.broadcasted_iota(jnp.int32, (lc, lc), 0)
    c_i = lax.broadcasted_iota(jnp.int32, (lc, lc), 1)
    incl = c_i <= r_i
    strict = c_i < r_i
    ch_g = 2 * hm + hf
    ch_b = ch_g + hg
    scale = HEAD_DIM ** -0.5
    chains = [(c, h) for c in range(cb) for h in range(hg)]
    amats, rhss, qks = [], [], []
    for c, h in chains:
        rows = slice(c * lc, (c + 1) * lc)
        sl = slice(h * HEAD_DIM, (h + 1) * HEAD_DIM)
        gc = gc_ref[rows, :]
        gr = gr_ref[c]
        cq = act[rows, h * HEAD_DIM:(h + 1) * HEAD_DIM]
        ck = act[rows, dg + h * HEAD_DIM:dg + (h + 1) * HEAD_DIM]
        cv = act[rows, 2 * dg + h * HEAD_DIM:2 * dg + (h + 1) * HEAD_DIM]
        qn = cq * lax.rsqrt(jnp.sum(cq * cq, axis=-1, keepdims=True) + NORM_EPS) * scale
        kn = ck * lax.rsqrt(jnp.sum(ck * ck, axis=-1, keepdims=True) + NORM_EPS)
        g_col = gc[:, ch_g + h:ch_g + h + 1]
        g_row = gr[ch_g + h:ch_g + h + 1, :]
        beta = gc[:, ch_b + h:ch_b + h + 1]
        decay = jnp.exp(jnp.where(incl, g_col - g_row, NEG_INF))
        if split_bf16:
            ks = _split(kn)
            kk = _mm3_nt(ks, ks)
            qk = _dot_nt(qn.astype(BF16), ks[0])
        else:
            kk = _dot_nt(kn, kn, HI)
            qk = _dot_nt(qn, kn)
        amats.append(jnp.where(strict, beta * kk * decay, 0.0))
        eg = jnp.exp(g_col)
        rhss.append(jnp.concatenate([kn * (beta * eg), cv * beta], axis=-1))
        g_last = g_col[lc - 1:lc, :]
        qg_ref[rows, sl] = (qn * eg).astype(qg_ref.dtype)
        kd_ref[rows, sl] = (kn * jnp.exp(g_last - g_col)).astype(kd_ref.dtype)
        qks.append((qk * decay).astype(qk_ref.dtype))
        if h == hg - 1:
            qk_ref[rows, :] = jnp.concatenate(qks[-hg:], axis=1)
    sols = _unit_lower_solve(amats, rhss, lc, split_bf16)
    for (c, h), sol in zip(chains, sols):
        rows = slice(c * lc, (c + 1) * lc)
        sl = slice(h * HEAD_DIM, (h + 1) * HEAD_DIM)
        w_ref[rows, sl] = sol[:, :HEAD_DIM].astype(w_ref.dtype)
        u_ref[rows, sl] = sol[:, HEAD_DIM:]


def _gdn_scan_kernel(w_ref, u_ref, qg_ref, kd_ref, qk_ref, z_ref, gc_ref, ng_ref, s0_ref,
                     h_ref, s_ref, *, hm, hf, hg, lc, bpb):
    @pl.when(pl.program_id(1) == 0)
    def _():
        s_ref[...] = s0_ref[...]

    ch_g = 2 * hm + hf
    xd = w_ref.dtype
    chains = [(b, h) for b in range(bpb) for h in range(hg)]
    sls = [slice(h * HEAD_DIM, (h + 1) * HEAD_DIM) for _, h in chains]
    first = []
    for (b, h), sl in zip(chains, sls):
        sx = s_ref[b, h].astype(xd)
        first.append((_dot(w_ref[b, :, sl], sx), _dot(qg_ref[b, :, sl], sx)))
    outs = []
    for (b, h), sl, (ws, qs) in zip(chains, sls, first):
        ux = (u_ref[b, :, sl] - ws).astype(xd)
        outs.append(qs + _dot(qk_ref[b][:, h * lc:(h + 1) * lc], ux))
        g_last = gc_ref[b, lc - SUBLANES:lc, :][SUBLANES - 1:SUBLANES, ch_g + h:ch_g + h + 1]
        s_ref[b, h] = jnp.exp(g_last) * s_ref[b, h] + _dot_tn(kd_ref[b, :, sl], ux)
    for (b, h), sl, o in zip(chains, sls, outs):
        on = o * lax.rsqrt(jnp.mean(o * o, axis=-1, keepdims=True) + NORM_EPS) * ng_ref[...]
        h_ref[b, :, sl] = (on * _silu(z_ref[b, :, sl])).astype(h_ref.dtype)


def _gdn(proj, gc, grc, conv_w, norm_g, s0, b0, layer, nb, t, bpb, dims):
    hm, hf, hg = dims
    dg = hg * HEAD_DIM
    lc = math.gcd(t, CHUNK)
    nc = t // lc
    xd = BF16 if lc >= 16 else F32
    base = (4 * hm * HEAD_DIM + 3 * hf * HEAD_DIM) // dg
    cb = _pick(nc, (2, 1))
    tb = cb * lc
    nblk = t // tb
    m = nb * t
    n_all = proj.shape[-1]
    proj2 = proj.reshape(m, n_all)

    def col(j):
        return pl.BlockSpec((tb, dg), lambda b, i: (b * nblk + i, base + j))

    def prev(j):
        return pl.BlockSpec((SUBLANES, dg),
                            lambda b, i: (jnp.maximum((b * t + i * tb) // SUBLANES - 1, 0), base + j))

    def rowblk(width):
        return pl.BlockSpec((tb, width), lambda b, i: (b * nblk + i, 0))

    w, u0, qg, kd, qk, b1 = pl.pallas_call(
        functools.partial(_gdn_prep_kernel, hm=hm, hf=hf, hg=hg, lc=lc, cb=cb),
        grid=(nb, nblk),
        in_specs=[col(0), col(1), col(2), prev(0), prev(1), prev(2),
                  pl.BlockSpec((None, 3, 3 * dg), lambda b, i: (b, 0, 0)),
                  rowblk(LANES),
                  pl.BlockSpec((None, cb, 2 * GATE_ROWS, lc), lambda b, i: (b, i, 0, 0)),
                  pl.BlockSpec((None, 4, 3 * dg), lambda b, i: (layer, 0, 0))],
        out_specs=[rowblk(dg), rowblk(dg), rowblk(dg), rowblk(dg), rowblk(hg * lc),
                   pl.BlockSpec((None, 3, 3 * dg), lambda b, i: (b, 0, 0))],
        out_shape=[jax.ShapeDtypeStruct((m, dg), xd), jax.ShapeDtypeStruct((m, dg), F32),
                   jax.ShapeDtypeStruct((m, dg), xd), jax.ShapeDtypeStruct((m, dg), xd),
                   jax.ShapeDtypeStruct((m, hg * lc), xd), jax.ShapeDtypeStruct((nb, 3, 3 * dg), F32)],
        scratch_shapes=[pltpu.VMEM((tb + SUBLANES, 3 * dg), F32)],
        name="gdn_prep",
        compiler_params=_params("arbitrary", "arbitrary"),
    )(proj2, proj2, proj2, proj2, proj2, proj2, b0, gc.reshape(m, LANES), grc, conv_w)

    def seq(width):
        return pl.BlockSpec((bpb, lc, width), lambda b, c: (b, c, 0))

    st4 = pl.BlockSpec((bpb, hg, HEAD_DIM, HEAD_DIM), lambda b, c: (b, 0, 0, 0))
    hgx, s1 = pl.pallas_call(
        functools.partial(_gdn_scan_kernel, hm=hm, hf=hf, hg=hg, lc=lc, bpb=bpb),
        grid=(nb // bpb, nc),
        in_specs=[seq(dg), seq(dg), seq(dg), seq(dg), seq(hg * lc),
                  pl.BlockSpec((bpb, lc, dg), lambda b, c: (b, c, base + 3)),
                  seq(LANES),
                  pl.BlockSpec((None, 1, HEAD_DIM), lambda b, c: (layer, 0, 0)),
                  st4],
        out_specs=[seq(dg), st4],
        out_shape=[jax.ShapeDtypeStruct((nb, t, dg), xd),
                   jax.ShapeDtypeStruct((nb, hg, HEAD_DIM, HEAD_DIM), F32)],
        name="gdn_scan",
        compiler_params=_params("arbitrary", "arbitrary"),
    )(w.reshape(nb, t, dg), u0.reshape(nb, t, dg), qg.reshape(nb, t, dg), kd.reshape(nb, t, dg),
      qk.reshape(nb, t, hg * lc), proj, gc, norm_g, s0)
    return hgx, s1, b1


def _fox_prompt_kernel(qt_ref, kt_ref, q_ref, k_ref, v_ref, gr_ref, o_ref, q_s, m_s, l_s, acc_s, *, hf, tb):
    qi = qt_ref[pl.program_id(1)]
    ki = kt_ref[pl.program_id(1)]
    rep = tb // LANES

    @pl.when(ki == 0)
    def _():
        q_s[...] = (q_ref[...] * HEAD_DIM ** -0.5).astype(BF16)
        m_s[...] = jnp.full(m_s.shape, NEG_INF, F32)
        l_s[...] = jnp.zeros(l_s.shape, F32)
        acc_s[...] = jnp.zeros(acc_s.shape, F32)

    def block(masked):
        if masked:
            visible = (lax.broadcasted_iota(jnp.int32, (tb, tb), 1) <= lax.broadcasted_iota(jnp.int32, (tb, tb), 0))
        for h in range(hf):
            sl = slice(h * HEAD_DIM, (h + 1) * HEAD_DIM)
            k = k_ref[:, sl].astype(BF16)
            v = v_ref[:, sl].astype(BF16)
            s = _dot_nt(q_s[:, sl], k) - gr_ref[h:h + 1, :]
            if masked:
                s = jnp.where(visible, s, NEG_INF)
            m_prev = m_s[h]
            m_new = jnp.maximum(m_prev, jnp.max(s, axis=-1, keepdims=True))
            alpha = jnp.exp(m_prev - m_new)
            p = jnp.exp(s - jnp.concatenate([m_new] * rep, axis=1))
            l_s[h] = alpha * l_s[h] + jnp.sum(p, axis=-1, keepdims=True)
            acc_s[:, sl] = alpha * acc_s[:, sl] + _dot(p.astype(BF16), v)
            m_s[h] = m_new

    @pl.when(ki < qi)
    def _():
        block(False)

    @pl.when(ki == qi)
    def _():
        block(True)
        for h in range(hf):
            sl = slice(h * HEAD_DIM, (h + 1) * HEAD_DIM)
            o_ref[:, sl] = (acc_s[:, sl] / l_s[h]).astype(BF16)


def _fox_prompt(proj, grf, nb, t, dims):
    hm, hf, hg = dims
    df = hf * HEAD_DIM
    base = (4 * hm * HEAD_DIM) // df
    tb = _pick(t, (512, 256, 128))
    nq = t // tb
    pairs = [(i, j) for i in range(nq) for j in range(i + 1)]
    q_tbl = jnp.asarray([p[0] for p in pairs], jnp.int32)
    k_tbl = jnp.asarray([p[1] for p in pairs], jnp.int32)
    return pl.pallas_call(
        functools.partial(_fox_prompt_kernel, hf=hf, tb=tb),
        grid_spec=pltpu.PrefetchScalarGridSpec(
            num_scalar_prefetch=2,
            grid=(nb, len(pairs)),
            in_specs=[pl.BlockSpec((tb, df), lambda b, s, qt, kt: (b * nq + qt[s], base)),
                      pl.BlockSpec((tb, df), lambda b, s, qt, kt: (b * nq + kt[s], base + 1)),
                      pl.BlockSpec((tb, df), lambda b, s, qt, kt: (b * nq + kt[s], base + 2)),
                      pl.BlockSpec((None, hf, tb), lambda b, s, qt, kt: (b, 0, kt[s]))],
            out_specs=pl.BlockSpec((tb, df), lambda b, s, qt, kt: (b * nq + qt[s], 0)),
            scratch_shapes=[pltpu.VMEM((tb, df), BF16), pltpu.VMEM((hf, tb, LANES), F32),
                            pltpu.VMEM((hf, tb, LANES), F32), pltpu.VMEM((tb, df), F32)]),
        out_shape=jax.ShapeDtypeStruct((nb * t, df), BF16),
        name="fox_prompt",
        compiler_params=_params("arbitrary", "arbitrary"),
    )(q_tbl, k_tbl, proj, proj, proj, grf)


def _fox_pool_kernel(x_ref, w_ref, t_ref, *, hf, page):
    n = hf * page
    x = x_ref[...]
    x1 = x.astype(BF16)
    r1 = x - x1.astype(F32)
    x2 = r1.astype(BF16)
    x3 = (r1 - x2.astype(F32)).astype(BF16)
    ri = lax.broadcasted_iota(jnp.int32, (n, n), 0)
    ci = lax.broadcasted_iota(jnp.int32, (n, n), 1)
    later = ((ri // page) == (ci % hf)) & ((ri % page) > (ci // hf))
    m_suf = jnp.where(later, 1.0, 0.0).astype(BF16)
    rj = lax.broadcasted_iota(jnp.int32, (n, LANES), 0)
    lj = lax.broadcasted_iota(jnp.int32, (n, LANES), 1)
    m_tot = jnp.where((rj // page) == (lj % hf), 1.0, 0.0).astype(BF16)
    w_ref[...] = _dot(x1, m_suf) + (_dot(x2, m_suf) + _dot(x3, m_suf))
    t_ref[...] = _dot(x1, m_tot) + (_dot(x2, m_tot) + _dot(x3, m_tot))


def _fox_pool(logf_hs, hf, page):
    r, n = logf_hs.shape
    rb = _pick(r, (512, 256, 128, 64, 32, 16, 8))
    return pl.pallas_call(
        functools.partial(_fox_pool_kernel, hf=hf, page=page),
        grid=(r // rb,),
        in_specs=[pl.BlockSpec((rb, n), lambda i: (i, 0))],
        out_specs=[pl.BlockSpec((rb, n), lambda i: (i, 0)), pl.BlockSpec((rb, LANES), lambda i: (i, 0))],
        out_shape=[jax.ShapeDtypeStruct((r, n), F32), jax.ShapeDtypeStruct((r, LANES), F32)],
        name="fox_pool",
        compiler_params=_params("arbitrary"),
    )(logf_hs)


def _fox_sample_kernel(pt_ref, q_ref, cr_ref, kn_ref, vn_ref, *rest, hf, nsteps, rpp, npp):
    kp = rest[0:npp]
    vp = rest[npp:2 * npp]
    wp = rest[2 * npp:3 * npp]
    tp = rest[3 * npp:4 * npp]
    o_ref, m_s, l_s, acc_s, carry_s = rest[4 * npp:]
    p = pl.program_id(1)
    q = (q_ref[...] * HEAD_DIM ** -0.5).astype(BF16)
    nr = q.shape[0]

    @pl.when(p == 0)
    def _():
        ri = lax.broadcasted_iota(jnp.int32, (nr, nr), 0)
        ci = lax.broadcasted_iota(jnp.int32, (nr, nr), 1)
        ok = ((ri % hf) == (ci % hf)) & ((ci // hf) <= (ri // hf))
        s = jnp.where(ok, _dot_nt(q, kn_ref[...].astype(BF16)) - cr_ref[...], NEG_INF)
        m0 = jnp.max(s, axis=-1, keepdims=True)
        e = jnp.exp(s - m0)
        m_s[...] = m0
        l_s[...] = jnp.sum(e, axis=-1, keepdims=True)
        acc_s[...] = _dot(e.astype(BF16), vn_ref[...].astype(BF16))
        carry_s[...] = jnp.zeros_like(carry_s)

    ncol = rpp * LANES
    ri = lax.broadcasted_iota(jnp.int32, (nr, ncol), 0)
    ci = lax.broadcasted_iota(jnp.int32, (nr, ncol), 1)
    head_mask = jnp.where((ri % hf) == (ci % hf), 0.0, NEG_INF).astype(F32)
    carry = carry_s[...]
    scores = []
    for g in range(npp):
        bias_row = jnp.concatenate([wp[g][r:r + 1, :] + carry for r in range(rpp)], axis=1)
        scores.append(_dot_nt(q, kp[g][...].astype(BF16)) + (head_mask + bias_row))
        carry = carry + tp[g][...]
    carry_s[...] = carry
    top = scores[0]
    for g in range(1, npp):
        top = jnp.maximum(top, scores[g])
    m_prev = m_s[...]
    m_new = jnp.maximum(m_prev, jnp.max(top, axis=-1, keepdims=True))
    alpha = jnp.exp(m_prev - m_new)
    acc = alpha * acc_s[...]
    tot = None
    for g in range(npp):
        e = jnp.exp(scores[g] - m_new)
        tot = e if tot is None else tot + e
        acc = acc + _dot(e.astype(BF16), vp[g][...].astype(BF16))
    l_s[...] = alpha * l_s[...] + jnp.sum(tot, axis=-1, keepdims=True)
    acc_s[...] = acc
    m_s[...] = m_new

    @pl.when(p == nsteps - 1)
    def _():
        o_ref[...] = (acc / l_s[...]).astype(BF16)


def _fox_sample(page_table, q2, cum_row, kn2, vn2, kpool, vpool, wpool, tpool, layer, hf):
    nb, npg = page_table.shape
    nr = q2.shape[1]
    nphys, pcols = kpool.shape[1], kpool.shape[2]
    rpp = pcols // LANES
    npp = _pick(npg, (16, 8, 4, 2, 1))
    nsteps = npg // npp

    def page(b, p, pt, g):
        return pt[b * npg + (npg - 1 - (p * npp + g))]

    per_b3 = lambda b, p, pt: (b, 0, 0)
    kv_specs = [pl.BlockSpec((None, None, pcols, HEAD_DIM),
                             lambda b, p, pt, g=g: (layer, page(b, p, pt, g), 0, 0)) for g in range(npp)]
    w_specs = [pl.BlockSpec((None, rpp, LANES),
                            lambda b, p, pt, g=g: (layer * nphys + page(b, p, pt, g), 0, 0)) for g in range(npp)]
    t_specs = [pl.BlockSpec((None, 1, LANES),
                            lambda b, p, pt, g=g: (layer * nphys + page(b, p, pt, g), 0, 0)) for g in range(npp)]
    return pl.pallas_call(
        functools.partial(_fox_sample_kernel, hf=hf, nsteps=nsteps, rpp=rpp, npp=npp),
        grid_spec=pltpu.PrefetchScalarGridSpec(
            num_scalar_prefetch=1,
            grid=(nb, nsteps),
            in_specs=[pl.BlockSpec((None, nr, HEAD_DIM), per_b3),
                      pl.BlockSpec((None, 1, nr), per_b3),
                      pl.BlockSpec((None, nr, HEAD_DIM), per_b3),
                      pl.BlockSpec((None, nr, HEAD_DIM), per_b3)] + kv_specs + kv_specs + w_specs + t_specs,
            out_specs=pl.BlockSpec((None, nr, HEAD_DIM), per_b3),
            scratch_shapes=[pltpu.VMEM((nr, 1), F32), pltpu.VMEM((nr, 1), F32),
                            pltpu.VMEM((nr, HEAD_DIM), F32), pltpu.VMEM((1, LANES), F32)]),
        out_shape=jax.ShapeDtypeStruct((nb, nr, HEAD_DIM), BF16),
        name="fox_decode",
        compiler_params=_params("arbitrary", "arbitrary"),
    )(page_table.reshape(-1), q2, cum_row, kn2, vn2, *([kpool] * npp), *([vpool] * npp),
      *([wpool] * npp), *([tpool] * npp))


def _regroup_w_in(w_in, b_in, dims):
    hm, hf, hg = dims
    dm, df, dg = hm * HEAD_DIM, hf * HEAD_DIM, hg * HEAD_DIM
    sizes = (dm, dm, dm, dm, hm, hm, df, df, df, hf, dg, dg, dg, dg, hg, hg)
    offs = [0]
    for s in sizes:
        offs.append(offs[-1] + s)
    order = (0, 1, 2, 3, 6, 7, 8, 10, 11, 12, 13, 4, 5, 9, 14, 15)
    n_main = 4 * dm + 3 * df + 4 * dg
    n_gate = 2 * hm + hf + 2 * hg
    n_tot = -(-(n_main + LANES) // 512) * 512
    pad = n_tot - n_main - n_gate

    def regroup(a, axis):
        parts = [lax.slice_in_dim(a, offs[i], offs[i + 1], axis=axis) for i in order]
        zshape = list(a.shape)
        zshape[axis] = pad
        parts.append(jnp.zeros(zshape, a.dtype))
        return jnp.concatenate(parts, axis=axis)

    w_t = regroup(jnp.swapaxes(w_in, 1, 2).astype(BF16), 1)
    return w_t, regroup(b_in, 1)[:, None, :], n_main // LANES


def _gate_param_cols(a_log, dt_bias, dims):
    hm, hf, hg = dims
    depth = a_log.shape[0]
    lead = 2 * hm + hf
    z0 = jnp.zeros((depth, lead), F32)
    z1 = jnp.zeros((depth, GATE_ROWS - lead - hg), F32)
    a_col = jnp.concatenate([z0, a_log, z1], axis=1)[:, :, None]
    dt_col = jnp.concatenate([z0, dt_bias, z1], axis=1)[:, :, None]
    return a_col, dt_col


def kernel(x_prompt, x_sample, cache_fox_k, cache_fox_v, cache_fox_logf, state_mlstm_C, state_mlstm_n, state_mlstm_m, state_gdn_S, state_gdn_conv, page_table, c_prompt, c_sample, w_ada, b_ada, ln_g, ln_b, ffn_w_gate, ffn_w_up, ffn_w_down, w_in, b_in, mlstm_norm_g, gdn_conv_w, gdn_A_log, gdn_dt_bias, gdn_norm_g, w_out):
    bp, tp, d = x_prompt.shape
    bs, ts, _ = x_sample.shape
    depth = w_ada.shape[0]
    hm, hf, hg = state_mlstm_C.shape[2], cache_fox_k.shape[3], state_gdn_S.shape[2]
    dims = (hm, hf, hg)
    dm, df, dg = hm * HEAD_DIM, hf * HEAD_DIM, hg * HEAD_DIM
    nphys, page = cache_fox_k.shape[1], cache_fox_k.shape[2]
    alpha = (2 * depth) ** 0.25
    assert 2 * hm + hf + 2 * hg <= GATE_ROWS and (page * hf) % LANES == 0 and LANES % hf == 0

    nrow = -(-(bp + bs) // 16) * 16
    c_all = jnp.concatenate([c_prompt, c_sample, jnp.zeros((nrow - bp - bs, d), F32)], axis=0)
    mod = _ada(c_all, w_ada, b_ada).reshape(depth, nrow, 3 * N_SUB, d)
    mod = jnp.transpose(mod, (0, 2, 1, 3))
    mod_p = mod[:, :, :bp, None, :]
    mod_s = jnp.repeat(mod[:, :, bp:bp + bs], ts, axis=2)[:, :, None, :, :]

    w_in_r, b_in_r, gate_blk = _regroup_w_in(w_in, b_in, dims)
    a_col, dt_col = _gate_param_cols(gdn_A_log, gdn_dt_bias, dims)
    ln_g4 = ln_g[:, :, None, :]
    ln_b4 = ln_b[:, :, None, :]
    norm_m = mlstm_norm_g[:, :, None, :]
    norm_g = gdn_norm_g[:, None, :]

    rpp = (page * hf) // LANES
    logf_hs = jnp.transpose(cache_fox_logf, (0, 1, 3, 2)).reshape(depth * nphys, hf * page)
    wpool, tpool = _fox_pool(logf_hs, hf, page)
    wpool = wpool.reshape(depth * nphys, rpp, LANES)
    tpool = tpool.reshape(depth * nphys, 1, LANES)
    kpool = cache_fox_k.reshape(depth, nphys, page * hf, HEAD_DIM)
    vpool = cache_fox_v.reshape(depth, nphys, page * hf, HEAD_DIM)

    o_f = 4 * dm
    o_g = o_f + 3 * df

    def mixers(l, h, nb, t, mlstm_state, gdn_state, sample):
        m_rows = nb * t
        proj = _win(h, w_in_r, b_in_r, l)
        lc = math.gcd(t, CHUNK)
        if sample:
            gc, grc, grf = _gates(proj, gate_blk, a_col, dt_col, l, 1, m_rows, lc, t, dims)
            grc = grc.reshape(nb, t // lc, 2 * GATE_ROWS, lc)
        else:
            gc, grc, grf = _gates(proj, gate_blk, a_col, dt_col, l, nb, t, lc, t, dims)
        c0, n0, m0 = mlstm_state
        proj3 = proj.reshape(nb, t, proj.shape[-1])
        gc3 = gc.reshape(nb, t, LANES)
        bpb = 1 if sample else nb
        hmx, c1, n1, m1 = _mlstm(proj3, gc3, grc, norm_m, c0, n0[:, :, None, :], m0[:, :, None, None],
                                 l, nb, t, bpb, dims)
        s0, b0 = gdn_state
        hgx, s1, b1 = _gdn(proj3, gc3, grc, gdn_conv_w, norm_g, s0, b0, l, nb, t, bpb, dims)
        hmx = hmx.reshape(m_rows, dm)
        hgx = hgx.reshape(m_rows, dg)
        fk = proj[:, o_f + df:o_f + 2 * df]
        fv = proj[:, o_f + 2 * df:o_f + 3 * df]
        logf = gc[:, GATE_ROWS + 2 * hm:GATE_ROWS + 2 * hm + hf]
        if sample:
            nr = t * hf
            q2 = proj[:, o_f:o_f + df].reshape(nb, nr, HEAD_DIM)
            cum = gc[:, 2 * hm:2 * hm + hf].reshape(nb, nr)
            hfx = _fox_sample(page_table, q2, cum[:, None, :], fk.reshape(nb, nr, HEAD_DIM),
                              fv.reshape(nb, nr, HEAD_DIM), kpool, vpool, wpool, tpool, l, hf)
            hfx = hfx.reshape(m_rows, df)
        else:
            hfx = _fox_prompt(proj, grf, nb, t, dims)
        new_state = (fk.reshape(nb, t, hf, HEAD_DIM), fv.reshape(nb, t, hf, HEAD_DIM), logf.reshape(nb, t, hf),
                     c1, n1[:, :, 0, :], m1[:, :, 0, 0], s1, b1)
        return (hmx.astype(BF16), hfx, hgx.astype(BF16)), new_state

    xp = x_prompt.reshape(bp * tp, d)
    xs = x_sample.reshape(bs * ts, d)
    hp = _modulate(xp, mod_p, 0, 0, tp)
    hs = _modulate(xs, mod_s, 0, 0, bs * ts)
    zero_mlstm = (jnp.zeros((bp, hm, HEAD_DIM, HEAD_DIM), F32), jnp.zeros((bp, hm, HEAD_DIM), F32),
                  jnp.zeros((bp, hm), F32))
    zero_gdn = (jnp.zeros((bp, hg, HEAD_DIM, HEAD_DIM), F32), jnp.zeros((bp, 3, 3 * dg), F32))

    def proj_ln(parts_p, parts_s, w, w_prefix, l, j, nxt, coef):
        return _proj_ln(parts_p, parts_s, w, w_prefix, xp, xs, mod_p, mod_s, l, j, nxt, ln_g4, ln_b4, tp, alpha, coef)

    st_p, st_s = [], []
    for l in range(depth):
        ap, as_ = _gateup(hp, hs, ffn_w_gate, ffn_w_up, l, 0)
        (xp, hp), (xs, hs) = proj_ln([ap], [as_], ffn_w_down, (l, 0), l, 0, (l, 1), MACARON_W)
        mixed_p, sp = mixers(l, hp, bp, tp, zero_mlstm, zero_gdn, False)
        mixed_s, ss = mixers(l, hs, bs, ts, (state_mlstm_C[l], state_mlstm_n[l], state_mlstm_m[l]),
                             (state_gdn_S[l], state_gdn_conv[l]), True)
        (xp, hp), (xs, hs) = proj_ln(mixed_p, mixed_s, w_out, (l,), l, 1, (l, 2), 1.0)
        ap, as_ = _gateup(hp, hs, ffn_w_gate, ffn_w_up, l, 1)
        nxt = (l + 1, 0) if l + 1 < depth else None
        (xp, hp), (xs, hs) = proj_ln([ap], [as_], ffn_w_down, (l, 1), l, 2, nxt, MACARON_W)
        st_p.append(sp)
        st_s.append(ss)
    pn = [jnp.stack(a) for a in zip(*st_p)]
    sn = [jnp.stack(a) for a in zip(*st_s)]
    return (xp.reshape(bp, tp, d), xs.reshape(bs, ts, d), pn[0], pn[1], pn[2], pn[3], pn[4], pn[5], pn[6], pn[7],
            sn[0], sn[1], sn[2], sn[3], sn[4], sn[5], sn[6], sn[7])
```

```python
import functools
import math

import jax
import jax.numpy as jnp
from jax import lax
from jax.experimental import pallas as pl
from jax.experimental.pallas import tpu as pltpu

F32 = jnp.float32
BF16 = jnp.bfloat16

HEAD_DIM = 128
LANES = 128
SUBLANES = 8
CHUNK = 64
N_SUB = 3
LN_EPS = 1e-5
NORM_EPS = 1e-6
MACARON_W = 0.5
GATE_ROWS = 32
VMEM_LIMIT_BYTES = 56 * 1024 * 1024
HI = lax.Precision.HIGHEST
NEG_INF = float("-inf")


def _params(*sem):
    return pltpu.CompilerParams(dimension_semantics=sem, vmem_limit_bytes=VMEM_LIMIT_BYTES)


def _dot(a, b, precision=None):
    return jnp.dot(a, b, preferred_element_type=F32, precision=precision)


def _dot_nt(a, b, precision=None):
    return lax.dot_general(a, b, (((1,), (1,)), ((), ())), preferred_element_type=F32, precision=precision)


def _dot_tn(a, b, precision=None):
    return lax.dot_general(a, b, (((0,), (0,)), ((), ())), preferred_element_type=F32, precision=precision)


def _sigmoid(x):
    return 1.0 / (1.0 + jnp.exp(-x))


def _silu(x):
    return x * _sigmoid(x)


def _softplus(x):
    return jnp.maximum(x, 0.0) + jnp.log1p(jnp.exp(-jnp.abs(x)))


def _log_sigmoid(x):
    return -_softplus(-x)


def _pick(m, candidates):
    for c in candidates:
        if m % c == 0:
            return c
    return m


def _ada_kernel(c_ref, w_ref, b_ref, o_ref):
    c = c_ref[...]
    sc = _silu(c).astype(BF16)
    o_ref[...] = _dot(sc, w_ref[...].astype(BF16)) + b_ref[...]


def _ada(c_all, w_ada, b_ada):
    depth, d, n = w_ada.shape
    rows = c_all.shape[0]
    bn = _pick(n, (1024, 512, 256, 128))
    return pl.pallas_call(
        _ada_kernel,
        grid=(depth, n // bn),
        in_specs=[pl.BlockSpec((rows, d), lambda l, j: (0, 0)),
                  pl.BlockSpec((None, d, bn), lambda l, j: (l, 0, j)),
                  pl.BlockSpec((None, 1, bn), lambda l, j: (l, 0, j))],
        out_specs=pl.BlockSpec((None, rows, bn), lambda l, j: (l, 0, j)),
        out_shape=jax.ShapeDtypeStruct((depth, rows, n), F32),
        name="ada",
        compiler_params=_params("arbitrary", "arbitrary"),
    )(c_all, w_ada, b_ada.reshape(depth, 1, n))


def _modulate_kernel(x_ref, sc_ref, sh_ref, o_ref):
    o_ref[...] = (x_ref[...] * (1.0 + sc_ref[...]) + sh_ref[...]).astype(BF16)


def _mod_spec(mod, layer, slot, rows_per_group, bm):
    r, d = mod.shape[3], mod.shape[4]
    if r == 1:
        return pl.BlockSpec((None, None, None, 1, d),
                            lambda m, *_: (layer, slot, (m * bm) // rows_per_group, 0, 0))
    return pl.BlockSpec((None, None, None, bm, d), lambda m, *_: (layer, slot, 0, m, 0))


def _modulate(x, mod, layer, j, rows_per_group):
    m, d = x.shape
    bm = _pick(rows_per_group, (512, 256, 128, 64))
    return pl.pallas_call(
        _modulate_kernel,
        grid=(m // bm,),
        in_specs=[pl.BlockSpec((bm, d), lambda i: (i, 0)),
                  _mod_spec(mod, layer, 3 * j + 1, rows_per_group, bm),
                  _mod_spec(mod, layer, 3 * j + 0, rows_per_group, bm)],
        out_specs=pl.BlockSpec((bm, d), lambda i: (i, 0)),
        out_shape=jax.ShapeDtypeStruct((m, d), BF16),
        name="modulate",
        compiler_params=_params("arbitrary"),
    )(x, mod, mod)


def _gateup_kernel(h_ref, hs_ref, wg_ref, wu_ref, o_ref, os_ref, wg_s, wu_s, *, last):
    i = pl.program_id(1)

    @pl.when(i == 0)
    def _():
        wg_s[...] = wg_ref[...].astype(BF16)
        wu_s[...] = wu_ref[...].astype(BF16)

    def swiglu_half(h):
        g = _dot(h, wg_s[...])
        u = _dot(h, wu_s[...])
        return (_silu(g) * u).astype(BF16)

    o_ref[...] = swiglu_half(h_ref[...])

    @pl.when(i == last)
    def _():
        os_ref[...] = swiglu_half(hs_ref[...])


def _gateup(h, hs, w_gate, w_up, layer, j):
    m, d = h.shape
    ms = hs.shape[0]
    f = w_gate.shape[-1]
    bm = _pick(m, (1024, 512, 256, 128, 64))
    bn = _pick(f, (512, 256, 128))
    wspec = pl.BlockSpec((None, None, d, bn), lambda n, i: (layer, j, 0, n))
    return pl.pallas_call(
        functools.partial(_gateup_kernel, last=m // bm - 1),
        grid=(f // bn, m // bm),
        in_specs=[pl.BlockSpec((bm, d), lambda n, i: (i, 0)), pl.BlockSpec((ms, d), lambda n, i: (0, 0)),
                  wspec, wspec],
        out_specs=[pl.BlockSpec((bm, bn), lambda n, i: (i, n)), pl.BlockSpec((ms, bn), lambda n, i: (0, n))],
        out_shape=[jax.ShapeDtypeStruct((m, f), BF16), jax.ShapeDtypeStruct((ms, f), BF16)],
        scratch_shapes=[pltpu.VMEM((d, bn), BF16), pltpu.VMEM((d, bn), BF16)],
        name="gateup",
        compiler_params=_params("arbitrary", "arbitrary"),
    )(h, hs, w_gate, w_up)


def _proj_ln_kernel(*refs, widths, nk, nm, bk, alpha, coef, emit_h):
    npart = len(widths)
    nin = npart + 2 + (2 if emit_h else 0)
    nout = 2 if emit_h else 1
    w_ref, lng_ref, lnb_ref = refs[0:3]
    ins = [refs[3 + s * nin:3 + (s + 1) * nin] for s in range(2)]
    outs = [refs[3 + 2 * nin + s * nout:3 + 2 * nin + (s + 1) * nout] for s in range(2)]
    w_s = refs[3 + 2 * nin + 2 * nout]
    i = pl.program_id(0)

    @pl.when(i < nk)
    def _():
        w_s[pl.ds(pl.multiple_of(i * bk, bk), bk), :] = w_ref[...].astype(BF16)

    def project(s):
        acc = None
        off = 0
        for a_ref, wd in zip(ins[s][:npart], widths):
            term = _dot(a_ref[...], w_s[off:off + wd, :])
            acc = term if acc is None else acc + term
            off += wd
        return acc

    def finish(s, acc):
        x_ref, gate_ref = ins[s][npart], ins[s][npart + 1]
        y = alpha * x_ref[...] + (coef * (1.0 + gate_ref[...])) * acc
        mu = jnp.mean(y, axis=-1, keepdims=True)
        yc = y - mu
        var = jnp.mean(yc * yc, axis=-1, keepdims=True)
        xn = yc * lax.rsqrt(var + LN_EPS) * lng_ref[...] + lnb_ref[...]
        outs[s][0][...] = xn
        if emit_h:
            sc_ref, sh_ref = ins[s][npart + 2], ins[s][npart + 3]
            outs[s][1][...] = (xn * (1.0 + sc_ref[...]) + sh_ref[...]).astype(BF16)

    @pl.when((i >= nk) & (i < nk + nm))
    def _():
        finish(0, project(0))

    @pl.when(i == nk + nm)
    def _():
        finish(1, project(1))


def _proj_ln(parts_p, parts_s, w, w_prefix, x_p, x_s, mod_p, mod_s, layer, j, nxt, ln_g, ln_b, rows_per_group,
             alpha, coef):
    widths = tuple(a.shape[1] for a in parts_p)
    k = sum(widths)
    m, d = x_p.shape
    ms = x_s.shape[0]
    bm = _pick(rows_per_group, (256, 128, 64))
    bk = _pick(k, (512, 256, 128))
    nk = k // bk
    nm = m // bm
    emit_h = nxt is not None
    npre = len(w_prefix)

    def row(i):
        return jnp.clip(i - nk, 0, nm - 1)

    def modspec_p(lyr, slot):
        return pl.BlockSpec((None, None, None, 1, d),
                            lambda i: (lyr, slot, (row(i) * bm) // rows_per_group, 0, 0))

    def modspec_s(lyr, slot):
        return pl.BlockSpec((None, None, None, ms, d), lambda i: (lyr, slot, 0, 0, 0))

    in_specs = [pl.BlockSpec((None,) * npre + (bk, d), lambda i: tuple(w_prefix) + (jnp.minimum(i, nk - 1), 0)),
                pl.BlockSpec((None, None, 1, d), lambda i: (layer, j, 0, 0)),
                pl.BlockSpec((None, None, 1, d), lambda i: (layer, j, 0, 0))]
    args = [w, ln_g, ln_b]
    in_specs += [pl.BlockSpec((bm, wd), lambda i: (row(i), 0)) for wd in widths]
    in_specs += [pl.BlockSpec((bm, d), lambda i: (row(i), 0)), modspec_p(layer, 3 * j + 2)]
    args += list(parts_p) + [x_p, mod_p]
    if emit_h:
        in_specs += [modspec_p(nxt[0], 3 * nxt[1] + 1), modspec_p(nxt[0], 3 * nxt[1] + 0)]
        args += [mod_p, mod_p]
    in_specs += [pl.BlockSpec((ms, wd), lambda i: (0, 0)) for wd in widths]
    in_specs += [pl.BlockSpec((ms, d), lambda i: (0, 0)), modspec_s(layer, 3 * j + 2)]
    args += list(parts_s) + [x_s, mod_s]
    if emit_h:
        in_specs += [modspec_s(nxt[0], 3 * nxt[1] + 1), modspec_s(nxt[0], 3 * nxt[1] + 0)]
        args += [mod_s, mod_s]
    out_specs = [pl.BlockSpec((bm, d), lambda i: (row(i), 0))]
    out_shape = [jax.ShapeDtypeStruct((m, d), F32)]
    if emit_h:
        out_specs.append(pl.BlockSpec((bm, d), lambda i: (row(i), 0)))
        out_shape.append(jax.ShapeDtypeStruct((m, d), BF16))
    out_specs.append(pl.BlockSpec((ms, d), lambda i: (0, 0)))
    out_shape.append(jax.ShapeDtypeStruct((ms, d), F32))
    if emit_h:
        out_specs.append(pl.BlockSpec((ms, d), lambda i: (0, 0)))
        out_shape.append(jax.ShapeDtypeStruct((ms, d), BF16))
    out = pl.pallas_call(
        functools.partial(_proj_ln_kernel, widths=widths, nk=nk, nm=nm, bk=bk, alpha=alpha, coef=coef,
                          emit_h=emit_h),
        grid=(nk + nm + 1,),
        in_specs=in_specs,
        out_specs=out_specs,
        out_shape=out_shape,
        scratch_shapes=[pltpu.VMEM((k, d), BF16)],
        name="proj_ln",
        compiler_params=_params("arbitrary"),
    )(*args)
    if emit_h:
        return (out[0], out[1]), (out[2], out[3])
    return (out[0], None), (out[1], None)


def _win_kernel(h_ref, w_ref, b_ref, o_ref):
    o_ref[...] = _dot_nt(h_ref[...], w_ref[...]) + b_ref[...]


def _win(h, w, b, layer):
    m, d = h.shape
    n = w.shape[1]
    bm = _pick(m, (2048, 1024, 512, 256, 128, 64))
    bn = _pick(n, (768, 512, 256, 128))
    return pl.pallas_call(
        _win_kernel,
        grid=(m // bm, n // bn),
        in_specs=[pl.BlockSpec((bm, d), lambda i, c: (i, 0)),
                  pl.BlockSpec((None, bn, d), lambda i, c: (layer, c, 0)),
                  pl.BlockSpec((None, 1, bn), lambda i, c: (layer, 0, c))],
        out_specs=pl.BlockSpec((bm, bn), lambda i, c: (i, c)),
        out_shape=jax.ShapeDtypeStruct((m, n), F32),
        name="w_in",
        compiler_params=_params("arbitrary", "arbitrary"),
    )(h, w, b)


def _gates_kernel(x_ref, a_ref, dt_ref, gc_ref, grc_ref, grf_ref, carry_ref, *, hm, hf, hg, lc, seg, tb):
    t = pl.program_id(1)

    @pl.when(t == 0)
    def _():
        carry_ref[...] = jnp.zeros_like(carry_ref)

    z = jnp.transpose(x_ref[...])[:GATE_ROWS]
    ch = lax.broadcasted_iota(jnp.int32, (GATE_ROWS, 1), 0)
    is_mi = ch < hm
    is_mf = (ch >= hm) & (ch < 2 * hm)
    is_ff = (ch >= 2 * hm) & (ch < 2 * hm + hf)
    is_ga = (ch >= 2 * hm + hf) & (ch < 2 * hm + hf + hg)
    is_gb = (ch >= 2 * hm + hf + hg) & (ch < 2 * hm + hf + 2 * hg)
    ls = _log_sigmoid(z)
    gg = -jnp.exp(a_ref[...]) * _softplus(z + dt_ref[...])
    sg = _sigmoid(z)
    s_i = lax.broadcasted_iota(jnp.int32, (tb, tb), 0)
    t_i = lax.broadcasted_iota(jnp.int32, (tb, tb), 1)
    m_loc = jnp.where((s_i <= t_i) & ((s_i // lc) == (t_i // lc)), 1.0, 0.0).astype(F32)
    m_seg = jnp.where((s_i <= t_i) & ((s_i // seg) == (t_i // seg)), 1.0, 0.0).astype(F32)
    v_loc = jnp.where(is_mf, ls, jnp.where(is_ga, gg, 0.0))
    v_seg = jnp.where(is_ff, ls, 0.0)
    cum_loc = _dot(v_loc, m_loc, HI)
    cum_seg = _dot(v_seg, m_seg, HI) + carry_ref[...]
    if seg > tb:
        carry_ref[...] = cum_seg[:, tb - 1:tb]
    bank0 = jnp.where(is_mi, z, jnp.where(is_mf | is_ga, cum_loc,
                                          jnp.where(is_ff, cum_seg, jnp.where(is_gb, sg, 0.0))))
    bank1 = jnp.where(is_ff, ls, 0.0)
    rows = jnp.concatenate([bank0, bank1], axis=0)
    full = jnp.concatenate([rows, jnp.zeros((LANES - 2 * GATE_ROWS, tb), F32)], axis=0)
    gc_ref[...] = jnp.transpose(full)
    for c in range(tb // lc):
        grc_ref[c] = rows[:, c * lc:(c + 1) * lc]
    grf_ref[...] = bank0[2 * hm:2 * hm + hf]


def _gates(proj, gate_blk, a_col, dt_col, layer, nb, t, lc, seg, dims):
    hm, hf, hg = dims
    tb = _pick(t, (512, 256, 128, 64))
    nt = t // tb
    return pl.pallas_call(
        functools.partial(_gates_kernel, hm=hm, hf=hf, hg=hg, lc=lc, seg=seg, tb=tb),
        grid=(nb, nt),
        in_specs=[pl.BlockSpec((tb, LANES), lambda b, i: (b * nt + i, gate_blk)),
                  pl.BlockSpec((None, GATE_ROWS, 1), lambda b, i: (layer, 0, 0)),
                  pl.BlockSpec((None, GATE_ROWS, 1), lambda b, i: (layer, 0, 0))],
        out_specs=[pl.BlockSpec((tb, LANES), lambda b, i: (b * nt + i, 0)),
                   pl.BlockSpec((None, tb // lc, 2 * GATE_ROWS, lc), lambda b, i: (b, i, 0, 0)),
                   pl.BlockSpec((None, hf, tb), lambda b, i: (b, 0, i))],
        out_shape=[jax.ShapeDtypeStruct((nb * t, LANES), F32),
                   jax.ShapeDtypeStruct((nb, t // lc, 2 * GATE_ROWS, lc), F32),
                   jax.ShapeDtypeStruct((nb, hf, t), F32)],
        scratch_shapes=[pltpu.VMEM((GATE_ROWS, 1), F32)],
        name="gates",
        compiler_params=_params("arbitrary", "arbitrary"),
    )(proj, a_col, dt_col)


def _mlstm_kernel(q_ref, k_ref, v_ref, o_ref, gc_ref, gr_ref, ng_ref, c0_ref, n0_ref, m0_ref,
                  h_ref, c_ref, n_ref, m_ref, *, hm, lc, bpb, mxu_dtype):
    @pl.when(pl.program_id(1) == 0)
    def _():
        c_ref[...] = c0_ref[...]
        n_ref[...] = n0_ref[...]
        m_ref[...] = m0_ref[...]

    r_i = lax.broadcasted_iota(jnp.int32, (lc, lc), 0)
    c_i = lax.broadcasted_iota(jnp.int32, (lc, lc), 1)
    causal = c_i <= r_i
    scale = HEAD_DIM ** -0.5
    chains = [(b, h) for b in range(bpb) for h in range(hm)]
    sls = [slice(h * HEAD_DIM, (h + 1) * HEAD_DIM) for _, h in chains]
    first = []
    for (b, h), sl in zip(chains, sls):
        q = q_ref[b, :, sl]
        k = k_ref[b, :, sl] * scale
        qx = q.astype(mxu_dtype)
        vx = v_ref[b, :, sl].astype(mxu_dtype)
        first.append(dict(q=q, k=k, vx=vx, qk=_dot_nt(qx, k.astype(mxu_dtype)),
                          qc=_dot(qx, c_ref[b, h].astype(mxu_dtype))))
    gate = []
    for b, h in chains:
        gc = gc_ref[b]
        gr = gr_ref[b]
        li_col = gc[:, h:h + 1]
        b_col = gc[:, hm + h:hm + h + 1]
        li_row = gr[h:h + 1, :]
        b_row = gr[hm + h:hm + h + 1, :]
        mst = m_ref[b, h]
        dmat = jnp.where(causal, b_col - b_row + li_row, NEG_INF)
        inter = b_col + mst
        b_last = b_col[lc - 1:lc, :]
        d_last = b_last - b_col + li_col
        gate.append(dict(dmat=dmat, inter=inter, d_last=d_last, carry=b_last + mst,
                         dmax=jnp.max(dmat, axis=-1, keepdims=True), lmax=jnp.max(d_last, axis=0, keepdims=True)))
    for g in gate:
        m_row = jnp.maximum(g["inter"], g["dmax"])
        m_new = jnp.maximum(g["carry"], g["lmax"])
        g.update(m_row=m_row, w_intra=jnp.exp(g["dmat"] - m_row), w_inter=jnp.exp(g["inter"] - m_row),
                 m_new=m_new, w_k=jnp.exp(g["d_last"] - m_new), decay=jnp.exp(g["carry"] - m_new))
    outs = []
    for (b, h), g, f in zip(chains, gate, first):
        cst = c_ref[b, h]
        nst = n_ref[b, h]
        s = f["qk"] * g["w_intra"]
        num = _dot(s.astype(mxu_dtype), f["vx"]) + g["w_inter"] * f["qc"]
        den = jnp.sum(s, axis=-1, keepdims=True) + g["w_inter"] * jnp.sum(f["q"] * nst, axis=-1, keepdims=True)
        outs.append(num / jnp.maximum(jnp.abs(den), jnp.exp(-g["m_row"])))
        kw = f["k"] * g["w_k"]
        c_ref[b, h] = g["decay"] * cst + _dot_tn(kw.astype(mxu_dtype), f["vx"])
        n_ref[b, h] = g["decay"] * nst + jnp.sum(kw, axis=0, keepdims=True)
        m_ref[b, h] = g["m_new"]
    for (b, h), sl, hh in zip(chains, sls, outs):
        mu = jnp.mean(hh, axis=-1, keepdims=True)
        hc = hh - mu
        var = jnp.mean(hc * hc, axis=-1, keepdims=True)
        hn = hc * lax.rsqrt(var + LN_EPS) * ng_ref[h]
        h_ref[b, :, sl] = (hn * _sigmoid(o_ref[b, :, sl])).astype(h_ref.dtype)


def _mlstm(proj, gc, grc, norm_g, c0, n0, m0, layer, nb, t, bpb, dims):
    hm = dims[0]
    dm = hm * HEAD_DIM
    lc = math.gcd(t, CHUNK)
    nc = t // lc
    mxu_dtype = BF16 if lc >= 16 else F32

    def col(j):
        return pl.BlockSpec((bpb, lc, dm), lambda b, c: (b, c, j))

    st4 = pl.BlockSpec((bpb, hm, HEAD_DIM, HEAD_DIM), lambda b, c: (b, 0, 0, 0))
    stn = pl.BlockSpec((bpb, hm, 1, HEAD_DIM), lambda b, c: (b, 0, 0, 0))
    stm = pl.BlockSpec((bpb, hm, 1, 1), lambda b, c: (b, 0, 0, 0))
    return pl.pallas_call(
        functools.partial(_mlstm_kernel, hm=hm, lc=lc, bpb=bpb, mxu_dtype=mxu_dtype),
        grid=(nb // bpb, nc),
        in_specs=[col(0), col(1), col(2), col(3),
                  pl.BlockSpec((bpb, lc, LANES), lambda b, c: (b, c, 0)),
                  pl.BlockSpec((bpb, None, 2 * GATE_ROWS, lc), lambda b, c: (b, c, 0, 0)),
                  pl.BlockSpec((None, hm, 1, HEAD_DIM), lambda b, c: (layer, 0, 0, 0)),
                  st4, stn, stm],
        out_specs=[pl.BlockSpec((bpb, lc, dm), lambda b, c: (b, c, 0)), st4, stn, stm],
        out_shape=[jax.ShapeDtypeStruct((nb, t, dm), mxu_dtype),
                   jax.ShapeDtypeStruct((nb, hm, HEAD_DIM, HEAD_DIM), F32),
                   jax.ShapeDtypeStruct((nb, hm, 1, HEAD_DIM), F32),
                   jax.ShapeDtypeStruct((nb, hm, 1, 1), F32)],
        name="mlstm",
        compiler_params=_params("arbitrary", "arbitrary"),
    )(proj, proj, proj, proj, gc, grc, norm_g, c0, n0, m0)


def _split(a):
    hi = a.astype(BF16)
    return hi, (a - hi.astype(F32)).astype(BF16)


def _mm3(a, b):
    return _dot(a[0], b[0]) + (_dot(a[0], b[1]) + _dot(a[1], b[0]))


def _mm3_nt(a, b):
    return _dot_nt(a[0], b[0]) + (_dot_nt(a[0], b[1]) + _dot_nt(a[1], b[0]))


def _unit_lower_solve(a, rhs, lc, split_bf16):
    r_i = lax.broadcasted_iota(jnp.int32, (lc, lc), 0)
    c_i = lax.broadcasted_iota(jnp.int32, (lc, lc), 1)
    eye = jnp.where(r_i == c_i, 1.0, 0.0).astype(F32)
    if split_bf16:
        prep, mm = _split, _mm3
    else:
        prep, mm = (lambda x: x), (lambda x, y: _dot(x, y, HI))
    npows = [-x for x in a]
    invs = [eye + n for n in npows]
    width = 2
    while width < lc:
        ns = [prep(n) for n in npows]
        npows = [mm(n, n) for n in ns]
        invs = [inv + mm(prep(inv), prep(n)) for inv, n in zip(invs, npows)]
        width *= 2
    return [mm(prep(inv), prep(r)) for inv, r in zip(invs, rhs)]


def _gdn_prep_kernel(q_ref, k_ref, v_ref, pq_ref, pk_ref, pv_ref, b0_ref, gc_ref, gr_ref, cw_ref,
                     w_ref, u_ref, qg_ref, kd_ref, qk_ref, buf_ref, xp_ref, *, hm, hf, hg, lc, cb):
    dg = hg * HEAD_DIM
    keep = SUBLANES
    tb = cb * lc
    split_bf16 = lc >= 16
    prev = jnp.concatenate([pq_ref[keep - 3:keep, :], pk_ref[keep - 3:keep, :], pv_ref[keep - 3:keep, :]], axis=1)
    xp_ref[keep - 3:keep, :] = jnp.where(pl.program_id(1) == 0, b0_ref[...], prev)
    xp_ref[keep:keep + tb, 0:dg] = q_ref[...]
    xp_ref[keep:keep + tb, dg:2 * dg] = k_ref[...]
    xp_ref[keep:keep + tb, 2 * dg:3 * dg] = v_ref[...]
    conv = xp_ref[keep:keep + tb, :] * cw_ref[3:4, :]
    for j in range(3):
        conv = conv + xp_ref[keep - 3 + j:keep - 3 + j + tb, :] * cw_ref[j:j + 1, :]
    buf_ref[...] = xp_ref[keep + tb - 3:keep + tb, :]
    act = _silu(conv)

    r_i = lax.broadcasted_iota(jnp.int32, (lc, lc), 0)
    c_i = lax.broadcasted_iota(jnp.int32, (lc, lc), 1)
    incl = c_i <= r_i
    strict = c_i < r_i
    ch_g = 2 * hm + hf
    ch_b = ch_g + hg
    scale = HEAD_DIM ** -0.5
    chains = [(c, h) for c in range(cb) for h in range(hg)]
    amats, rhss, qks = [], [], []
    for c, h in chains:
        rows = slice(c * lc, (c + 1) * lc)
        sl = slice(h * HEAD_DIM, (h + 1) * HEAD_DIM)
        gc = gc_ref[rows, :]
        gr = gr_ref[c]
        cq = act[rows, h * HEAD_DIM:(h + 1) * HEAD_DIM]
        ck = act[rows, dg + h * HEAD_DIM:dg + (h + 1) * HEAD_DIM]
        cv = act[rows, 2 * dg + h * HEAD_DIM:2 * dg + (h + 1) * HEAD_DIM]
        qn = cq * lax.rsqrt(jnp.sum(cq * cq, axis=-1, keepdims=True) + NORM_EPS) * scale
        kn = ck * lax.rsqrt(jnp.sum(ck * ck, axis=-1, keepdims=True) + NORM_EPS)
        g_col = gc[:, ch_g + h:ch_g + h + 1]
        g_row = gr[ch_g + h:ch_g + h + 1, :]
        beta = gc[:, ch_b + h:ch_b + h + 1]
        decay = jnp.exp(jnp.where(incl, g_col - g_row, NEG_INF))
        if split_bf16:
            ks = _split(kn)
            kk = _mm3_nt(ks, ks)
            qk = _dot_nt(qn.astype(BF16), ks[0])
        else:
            kk = _dot_nt(kn, kn, HI)
            qk = _dot_nt(qn, kn)
        amats.append(jnp.where(strict, beta * kk * decay, 0.0))
        eg = jnp.exp(g_col)
        rhss.append(jnp.concatenate([kn * (beta * eg), cv * beta], axis=-1))
        g_last = g_col[lc - 1:lc, :]
        qg_ref[rows, sl] = (qn * eg).astype(qg_ref.dtype)
        kd_ref[rows, sl] = (kn * jnp.exp(g_last - g_col)).astype(kd_ref.dtype)
        qks.append((qk * decay).astype(qk_ref.dtype))
        if h == hg - 1:
            qk_ref[rows, :] = jnp.concatenate(qks[-hg:], axis=1)
    sols = _unit_lower_solve(amats, rhss, lc, split_bf16)
    for (c, h), sol in zip(chains, sols):
        rows = slice(c * lc, (c + 1) * lc)
        sl = slice(h * HEAD_DIM, (h + 1) * HEAD_DIM)
        w_ref[rows, sl] = sol[:, :HEAD_DIM].astype(w_ref.dtype)
        u_ref[rows, sl] = sol[:, HEAD_DIM:]


def _gdn_scan_kernel(w_ref, u_ref, qg_ref, kd_ref, qk_ref, z_ref, gc_ref, ng_ref, s0_ref,
                     h_ref, s_ref, *, hm, hf, hg, lc, bpb):
    @pl.when(pl.program_id(1) == 0)
    def _():
        s_ref[...] = s0_ref[...]

    ch_g = 2 * hm + hf
    xd = w_ref.dtype
    chains = [(b, h) for b in range(bpb) for h in range(hg)]
    sls = [slice(h * HEAD_DIM, (h + 1) * HEAD_DIM) for _, h in chains]
    first = []
    for (b, h), sl in zip(chains, sls):
        sx = s_ref[b, h].astype(xd)
        first.append((_dot(w_ref[b, :, sl], sx), _dot(qg_ref[b, :, sl], sx)))
    outs = []
    for (b, h), sl, (ws, qs) in zip(chains, sls, first):
        ux = (u_ref[b, :, sl] - ws).astype(xd)
        outs.append(qs + _dot(qk_ref[b][:, h * lc:(h + 1) * lc], ux))
        g_last = gc_ref[b, lc - SUBLANES:lc, :][SUBLANES - 1:SUBLANES, ch_g + h:ch_g + h + 1]
        s_ref[b, h] = jnp.exp(g_last) * s_ref[b, h] + _dot_tn(kd_ref[b, :, sl], ux)
    for (b, h), sl, o in zip(chains, sls, outs):
        on = o * lax.rsqrt(jnp.mean(o * o, axis=-1, keepdims=True) + NORM_EPS) * ng_ref[...]
        h_ref[b, :, sl] = (on * _silu(z_ref[b, :, sl])).astype(h_ref.dtype)


def _gdn(proj, gc, grc, conv_w, norm_g, s0, b0, layer, nb, t, bpb, dims):
    hm, hf, hg = dims
    dg = hg * HEAD_DIM
    lc = math.gcd(t, CHUNK)
    nc = t // lc
    xd = BF16 if lc >= 16 else F32
    base = (4 * hm * HEAD_DIM + 3 * hf * HEAD_DIM) // dg
    cb = _pick(nc, (4, 2, 1))
    tb = cb * lc
    nblk = t // tb
    m = nb * t
    n_all = proj.shape[-1]
    proj2 = proj.reshape(m, n_all)

    def col(j):
        return pl.BlockSpec((tb, dg), lambda b, i: (b * nblk + i, base + j))

    def prev(j):
        return pl.BlockSpec((SUBLANES, dg),
                            lambda b, i: (jnp.maximum((b * t + i * tb) // SUBLANES - 1, 0), base + j))

    def rowblk(width):
        return pl.BlockSpec((tb, width), lambda b, i: (b * nblk + i, 0))

    w, u0, qg, kd, qk, b1 = pl.pallas_call(
        functools.partial(_gdn_prep_kernel, hm=hm, hf=hf, hg=hg, lc=lc, cb=cb),
        grid=(nb, nblk),
        in_specs=[col(0), col(1), col(2), prev(0), prev(1), prev(2),
                  pl.BlockSpec((None, 3, 3 * dg), lambda b, i: (b, 0, 0)),
                  rowblk(LANES),
                  pl.BlockSpec((None, cb, 2 * GATE_ROWS, lc), lambda b, i: (b, i, 0, 0)),
                  pl.BlockSpec((None, 4, 3 * dg), lambda b, i: (layer, 0, 0))],
        out_specs=[rowblk(dg), rowblk(dg), rowblk(dg), rowblk(dg), rowblk(hg * lc),
                   pl.BlockSpec((None, 3, 3 * dg), lambda b, i: (b, 0, 0))],
        out_shape=[jax.ShapeDtypeStruct((m, dg), xd), jax.ShapeDtypeStruct((m, dg), F32),
                   jax.ShapeDtypeStruct((m, dg), xd), jax.ShapeDtypeStruct((m, dg), xd),
                   jax.ShapeDtypeStruct((m, hg * lc), xd), jax.ShapeDtypeStruct((nb, 3, 3 * dg), F32)],
        scratch_shapes=[pltpu.VMEM((tb + SUBLANES, 3 * dg), F32)],
        name="gdn_prep",
        compiler_params=_params("arbitrary", "arbitrary"),
    )(proj2, proj2, proj2, proj2, proj2, proj2, b0, gc.reshape(m, LANES), grc, conv_w)

    def seq(width):
        return pl.BlockSpec((bpb, lc, width), lambda b, c: (b, c, 0))

    st4 = pl.BlockSpec((bpb, hg, HEAD_DIM, HEAD_DIM), lambda b, c: (b, 0, 0, 0))
    hgx, s1 = pl.pallas_call(
        functools.partial(_gdn_scan_kernel, hm=hm, hf=hf, hg=hg, lc=lc, bpb=bpb),
        grid=(nb // bpb, nc),
        in_specs=[seq(dg), seq(dg), seq(dg), seq(dg), seq(hg * lc),
                  pl.BlockSpec((bpb, lc, dg), lambda b, c: (b, c, base + 3)),
                  seq(LANES),
                  pl.BlockSpec((None, 1, HEAD_DIM), lambda b, c: (layer, 0, 0)),
                  st4],
        out_specs=[seq(dg), st4],
        out_shape=[jax.ShapeDtypeStruct((nb, t, dg), xd),
                   jax.ShapeDtypeStruct((nb, hg, HEAD_DIM, HEAD_DIM), F32)],
        name="gdn_scan",
        compiler_params=_params("arbitrary", "arbitrary"),
    )(w.reshape(nb, t, dg), u0.reshape(nb, t, dg), qg.reshape(nb, t, dg), kd.reshape(nb, t, dg),
      qk.reshape(nb, t, hg * lc), proj, gc, norm_g, s0)
    return hgx, s1, b1


def _fox_prompt_kernel(qt_ref, kt_ref, q_ref, k_ref, v_ref, gr_ref, o_ref, q_s, m_s, l_s, acc_s, *, hf, tb):
    qi = qt_ref[pl.program_id(1)]
    ki = kt_ref[pl.program_id(1)]
    rep = tb // LANES

    @pl.when(ki == 0)
    def _():
        q_s[...] = (q_ref[...] * HEAD_DIM ** -0.5).astype(BF16)
        m_s[...] = jnp.full(m_s.shape, NEG_INF, F32)
        l_s[...] = jnp.zeros(l_s.shape, F32)
        acc_s[...] = jnp.zeros(acc_s.shape, F32)

    def block(masked):
        if masked:
            visible = (lax.broadcasted_iota(jnp.int32, (tb, tb), 1) <= lax.broadcasted_iota(jnp.int32, (tb, tb), 0))
        for h in range(hf):
            sl = slice(h * HEAD_DIM, (h + 1) * HEAD_DIM)
            k = k_ref[:, sl].astype(BF16)
            v = v_ref[:, sl].astype(BF16)
            s = _dot_nt(q_s[:, sl], k) - gr_ref[h:h + 1, :]
            if masked:
                s = jnp.where(visible, s, NEG_INF)
            m_prev = m_s[h]
            m_new = jnp.maximum(m_prev, jnp.max(s, axis=-1, keepdims=True))
            alpha = jnp.exp(m_prev - m_new)
            p = jnp.exp(s - jnp.concatenate([m_new] * rep, axis=1))
            l_s[h] = alpha * l_s[h] + jnp.sum(p, axis=-1, keepdims=True)
            acc_s[:, sl] = alpha * acc_s[:, sl] + _dot(p.astype(BF16), v)
            m_s[h] = m_new

    @pl.when(ki < qi)
    def _():
        block(False)

    @pl.when(ki == qi)
    def _():
        block(True)
        for h in range(hf):
            sl = slice(h * HEAD_DIM, (h + 1) * HEAD_DIM)
            o_ref[:, sl] = (acc_s[:, sl] / l_s[h]).astype(BF16)


def _fox_prompt(proj, grf, nb, t, dims):
    hm, hf, hg = dims
    df = hf * HEAD_DIM
    base = (4 * hm * HEAD_DIM) // df
    tb = _pick(t, (512, 256, 128))
    nq = t // tb
    pairs = [(i, j) for i in range(nq) for j in range(i + 1)]
    q_tbl = jnp.asarray([p[0] for p in pairs], jnp.int32)
    k_tbl = jnp.asarray([p[1] for p in pairs], jnp.int32)
    return pl.pallas_call(
        functools.partial(_fox_prompt_kernel, hf=hf, tb=tb),
        grid_spec=pltpu.PrefetchScalarGridSpec(
            num_scalar_prefetch=2,
            grid=(nb, len(pairs)),
            in_specs=[pl.BlockSpec((tb, df), lambda b, s, qt, kt: (b * nq + qt[s], base)),
                      pl.BlockSpec((tb, df), lambda b, s, qt, kt: (b * nq + kt[s], base + 1)),
                      pl.BlockSpec((tb, df), lambda b, s, qt, kt: (b * nq + kt[s], base + 2)),
                      pl.BlockSpec((None, hf, tb), lambda b, s, qt, kt: (b, 0, kt[s]))],
            out_specs=pl.BlockSpec((tb, df), lambda b, s, qt, kt: (b * nq + qt[s], 0)),
            scratch_shapes=[pltpu.VMEM((tb, df), BF16), pltpu.VMEM((hf, tb, LANES), F32),
                            pltpu.VMEM((hf, tb, LANES), F32), pltpu.VMEM((tb, df), F32)]),
        out_shape=jax.ShapeDtypeStruct((nb * t, df), BF16),
        name="fox_prompt",
        compiler_params=_params("arbitrary", "arbitrary"),
    )(q_tbl, k_tbl, proj, proj, proj, grf)


def _fox_pool_kernel(x_ref, w_ref, t_ref, *, hf, page):
    n = hf * page
    x = x_ref[...]
    x1 = x.astype(BF16)
    r1 = x - x1.astype(F32)
    x2 = r1.astype(BF16)
    x3 = (r1 - x2.astype(F32)).astype(BF16)
    ri = lax.broadcasted_iota(jnp.int32, (n, n), 0)
    ci = lax.broadcasted_iota(jnp.int32, (n, n), 1)
    later = ((ri // page) == (ci % hf)) & ((ri % page) > (ci // hf))
    m_suf = jnp.where(later, 1.0, 0.0).astype(BF16)
    rj = lax.broadcasted_iota(jnp.int32, (n, LANES), 0)
    lj = lax.broadcasted_iota(jnp.int32, (n, LANES), 1)
    m_tot = jnp.where((rj // page) == (lj % hf), 1.0, 0.0).astype(BF16)
    w_ref[...] = _dot(x1, m_suf) + (_dot(x2, m_suf) + _dot(x3, m_suf))
    t_ref[...] = _dot(x1, m_tot) + (_dot(x2, m_tot) + _dot(x3, m_tot))


def _fox_pool(logf_hs, hf, page):
    r, n = logf_hs.shape
    rb = _pick(r, (512, 256, 128, 64, 32, 16, 8))
    return pl.pallas_call(
        functools.partial(_fox_pool_kernel, hf=hf, page=page),
        grid=(r // rb,),
        in_specs=[pl.BlockSpec((rb, n), lambda i: (i, 0))],
        out_specs=[pl.BlockSpec((rb, n), lambda i: (i, 0)), pl.BlockSpec((rb, LANES), lambda i: (i, 0))],
        out_shape=[jax.ShapeDtypeStruct((r, n), F32), jax.ShapeDtypeStruct((r, LANES), F32)],
        name="fox_pool",
        compiler_params=_params("arbitrary"),
    )(logf_hs)


def _fox_sample_kernel(pt_ref, q_ref, cr_ref, kn_ref, vn_ref, *rest, hf, nsteps, rpp, npp):
    kp = rest[0:npp]
    vp = rest[npp:2 * npp]
    wp = rest[2 * npp:3 * npp]
    tp = rest[3 * npp:4 * npp]
    o_ref, m_s, l_s, acc_s, carry_s = rest[4 * npp:]
    p = pl.program_id(1)
    q = (q_ref[...] * HEAD_DIM ** -0.5).astype(BF16)
    nr = q.shape[0]

    @pl.when(p == 0)
    def _():
        ri = lax.broadcasted_iota(jnp.int32, (nr, nr), 0)
        ci = lax.broadcasted_iota(jnp.int32, (nr, nr), 1)
        ok = ((ri % hf) == (ci % hf)) & ((ci // hf) <= (ri // hf))
        s = jnp.where(ok, _dot_nt(q, kn_ref[...].astype(BF16)) - cr_ref[...], NEG_INF)
        m0 = jnp.max(s, axis=-1, keepdims=True)
        e = jnp.exp(s - m0)
        m_s[...] = m0
        l_s[...] = jnp.sum(e, axis=-1, keepdims=True)
        acc_s[...] = _dot(e.astype(BF16), vn_ref[...].astype(BF16))
        carry_s[...] = jnp.zeros_like(carry_s)

    ncol = rpp * LANES
    ri = lax.broadcasted_iota(jnp.int32, (nr, ncol), 0)
    ci = lax.broadcasted_iota(jnp.int32, (nr, ncol), 1)
    head_mask = jnp.where((ri % hf) == (ci % hf), 0.0, NEG_INF).astype(F32)
    carry = carry_s[...]
    scores = []
    for g in range(npp):
        bias_row = jnp.concatenate([wp[g][r:r + 1, :] + carry for r in range(rpp)], axis=1)
        scores.append(_dot_nt(q, kp[g][...].astype(BF16)) + (head_mask + bias_row))
        carry = carry + tp[g][...]
    carry_s[...] = carry
    top = scores[0]
    for g in range(1, npp):
        top = jnp.maximum(top, scores[g])
    m_prev = m_s[...]
    m_new = jnp.maximum(m_prev, jnp.max(top, axis=-1, keepdims=True))
    alpha = jnp.exp(m_prev - m_new)
    acc = alpha * acc_s[...]
    tot = None
    for g in range(npp):
        e = jnp.exp(scores[g] - m_new)
        tot = e if tot is None else tot + e
        acc = acc + _dot(e.astype(BF16), vp[g][...].astype(BF16))
    l_s[...] = alpha * l_s[...] + jnp.sum(tot, axis=-1, keepdims=True)
    acc_s[...] = acc
    m_s[...] = m_new

    @pl.when(p == nsteps - 1)
    def _():
        o_ref[...] = (acc / l_s[...]).astype(BF16)


def _fox_sample(page_table, q2, cum_row, kn2, vn2, kpool, vpool, wpool, tpool, layer, hf):
    nb, npg = page_table.shape
    nr = q2.shape[1]
    nphys, pcols = kpool.shape[1], kpool.shape[2]
    rpp = pcols // LANES
    npp = _pick(npg, (16, 8, 4, 2, 1))
    nsteps = npg // npp

    def page(b, p, pt, g):
        return pt[b * npg + (npg - 1 - (p * npp + g))]

    per_b3 = lambda b, p, pt: (b, 0, 0)
    kv_specs = [pl.BlockSpec((None, None, pcols, HEAD_DIM),
                             lambda b, p, pt, g=g: (layer, page(b, p, pt, g), 0, 0)) for g in range(npp)]
    w_specs = [pl.BlockSpec((None, rpp, LANES),
                            lambda b, p, pt, g=g: (layer * nphys + page(b, p, pt, g), 0, 0)) for g in range(npp)]
    t_specs = [pl.BlockSpec((None, 1, LANES),
                            lambda b, p, pt, g=g: (layer * nphys + page(b, p, pt, g), 0, 0)) for g in range(npp)]
    return pl.pallas_call(
        functools.partial(_fox_sample_kernel, hf=hf, nsteps=nsteps, rpp=rpp, npp=npp),
        grid_spec=pltpu.PrefetchScalarGridSpec(
            num_scalar_prefetch=1,
            grid=(nb, nsteps),
            in_specs=[pl.BlockSpec((None, nr, HEAD_DIM), per_b3),
                      pl.BlockSpec((None, 1, nr), per_b3),
                      pl.BlockSpec((None, nr, HEAD_DIM), per_b3),
                      pl.BlockSpec((None, nr, HEAD_DIM), per_b3)] + kv_specs + kv_specs + w_specs + t_specs,
            out_specs=pl.BlockSpec((None, nr, HEAD_DIM), per_b3),
            scratch_shapes=[pltpu.VMEM((nr, 1), F32), pltpu.VMEM((nr, 1), F32),
                            pltpu.VMEM((nr, HEAD_DIM), F32), pltpu.VMEM((1, LANES), F32)]),
        out_shape=jax.ShapeDtypeStruct((nb, nr, HEAD_DIM), BF16),
        name="fox_decode",
        compiler_params=_params("arbitrary", "arbitrary"),
    )(page_table.reshape(-1), q2, cum_row, kn2, vn2, *([kpool] * npp), *([vpool] * npp),
      *([wpool] * npp), *([tpool] * npp))


def _regroup_w_in(w_in, b_in, dims):
    hm, hf, hg = dims
    dm, df, dg = hm * HEAD_DIM, hf * HEAD_DIM, hg * HEAD_DIM
    sizes = (dm, dm, dm, dm, hm, hm, df, df, df, hf, dg, dg, dg, dg, hg, hg)
    offs = [0]
    for s in sizes:
        offs.append(offs[-1] + s)
    order = (0, 1, 2, 3, 6, 7, 8, 10, 11, 12, 13, 4, 5, 9, 14, 15)
    n_main = 4 * dm + 3 * df + 4 * dg
    n_gate = 2 * hm + hf + 2 * hg
    n_tot = -(-(n_main + LANES) // 512) * 512
    pad = n_tot - n_main - n_gate

    def regroup(a, axis):
        parts = [lax.slice_in_dim(a, offs[i], offs[i + 1], axis=axis) for i in order]
        zshape = list(a.shape)
        zshape[axis] = pad
        parts.append(jnp.zeros(zshape, a.dtype))
        return jnp.concatenate(parts, axis=axis)

    w_t = regroup(jnp.swapaxes(w_in, 1, 2).astype(BF16), 1)
    return w_t, regroup(b_in, 1)[:, None, :], n_main // LANES


def _gate_param_cols(a_log, dt_bias, dims):
    hm, hf, hg = dims
    depth = a_log.shape[0]
    lead = 2 * hm + hf
    z0 = jnp.zeros((depth, lead), F32)
    z1 = jnp.zeros((depth, GATE_ROWS - lead - hg), F32)
    a_col = jnp.concatenate([z0, a_log, z1], axis=1)[:, :, None]
    dt_col = jnp.concatenate([z0, dt_bias, z1], axis=1)[:, :, None]
    return a_col, dt_col


def kernel(x_prompt, x_sample, cache_fox_k, cache_fox_v, cache_fox_logf, state_mlstm_C, state_mlstm_n, state_mlstm_m, state_gdn_S, state_gdn_conv, page_table, c_prompt, c_sample, w_ada, b_ada, ln_g, ln_b, ffn_w_gate, ffn_w_up, ffn_w_down, w_in, b_in, mlstm_norm_g, gdn_conv_w, gdn_A_log, gdn_dt_bias, gdn_norm_g, w_out):
    bp, tp, d = x_prompt.shape
    bs, ts, _ = x_sample.shape
    depth = w_ada.shape[0]
    hm, hf, hg = state_mlstm_C.shape[2], cache_fox_k.shape[3], state_gdn_S.shape[2]
    dims = (hm, hf, hg)
    dm, df, dg = hm * HEAD_DIM, hf * HEAD_DIM, hg * HEAD_DIM
    nphys, page = cache_fox_k.shape[1], cache_fox_k.shape[2]
    alpha = (2 * depth) ** 0.25
    assert 2 * hm + hf + 2 * hg <= GATE_ROWS and (page * hf) % LANES == 0 and LANES % hf == 0

    nrow = -(-(bp + bs) // 16) * 16
    c_all = jnp.concatenate([c_prompt, c_sample, jnp.zeros((nrow - bp - bs, d), F32)], axis=0)
    mod = _ada(c_all, w_ada, b_ada).reshape(depth, nrow, 3 * N_SUB, d)
    mod = jnp.transpose(mod, (0, 2, 1, 3))
    mod_p = mod[:, :, :bp, None, :]
    mod_s = jnp.repeat(mod[:, :, bp:bp + bs], ts, axis=2)[:, :, None, :, :]

    w_in_r, b_in_r, gate_blk = _regroup_w_in(w_in, b_in, dims)
    a_col, dt_col = _gate_param_cols(gdn_A_log, gdn_dt_bias, dims)
    ln_g4 = ln_g[:, :, None, :]
    ln_b4 = ln_b[:, :, None, :]
    norm_m = mlstm_norm_g[:, :, None, :]
    norm_g = gdn_norm_g[:, None, :]

    rpp = (page * hf) // LANES
    logf_hs = jnp.transpose(cache_fox_logf, (0, 1, 3, 2)).reshape(depth * nphys, hf * page)
    wpool, tpool = _fox_pool(logf_hs, hf, page)
    wpool = wpool.reshape(depth * nphys, rpp, LANES)
    tpool = tpool.reshape(depth * nphys, 1, LANES)
    kpool = cache_fox_k.reshape(depth, nphys, page * hf, HEAD_DIM)
    vpool = cache_fox_v.reshape(depth, nphys, page * hf, HEAD_DIM)

    o_f = 4 * dm
    o_g = o_f + 3 * df

    def mixers(l, h, nb, t, mlstm_state, gdn_state, sample):
        m_rows = nb * t
        proj = _win(h, w_in_r, b_in_r, l)
        lc = math.gcd(t, CHUNK)
        if sample:
            gc, grc, grf = _gates(proj, gate_blk, a_col, dt_col, l, 1, m_rows, lc, t, dims)
            grc = grc.reshape(nb, t // lc, 2 * GATE_ROWS, lc)
        else:
            gc, grc, grf = _gates(proj, gate_blk, a_col, dt_col, l, nb, t, lc, t, dims)
        c0, n0, m0 = mlstm_state
        proj3 = proj.reshape(nb, t, proj.shape[-1])
        gc3 = gc.reshape(nb, t, LANES)
        bpb = 1 if sample else nb
        hmx, c1, n1, m1 = _mlstm(proj3, gc3, grc, norm_m, c0, n0[:, :, None, :], m0[:, :, None, None],
                                 l, nb, t, bpb, dims)
        s0, b0 = gdn_state
        hgx, s1, b1 = _gdn(proj3, gc3, grc, gdn_conv_w, norm_g, s0, b0, l, nb, t, bpb, dims)
        hmx = hmx.reshape(m_rows, dm)
        hgx = hgx.reshape(m_rows, dg)
        fk = proj[:, o_f + df:o_f + 2 * df]
        fv = proj[:, o_f + 2 * df:o_f + 3 * df]
        logf = gc[:, GATE_ROWS + 2 * hm:GATE_ROWS + 2 * hm + hf]
        if sample:
            nr = t * hf
            q2 = proj[:, o_f:o_f + df].reshape(nb, nr, HEAD_DIM)
            cum = gc[:, 2 * hm:2 * hm + hf].reshape(nb, nr)
            hfx = _fox_sample(page_table, q2, cum[:, None, :], fk.reshape(nb, nr, HEAD_DIM),
                              fv.reshape(nb, nr, HEAD_DIM), kpool, vpool, wpool, tpool, l, hf)
            hfx = hfx.reshape(m_rows, df)
        else:
            hfx = _fox_prompt(proj, grf, nb, t, dims)
        new_state = (fk.reshape(nb, t, hf, HEAD_DIM), fv.reshape(nb, t, hf, HEAD_DIM), logf.reshape(nb, t, hf),
                     c1, n1[:, :, 0, :], m1[:, :, 0, 0], s1, b1)
        return (hmx.astype(BF16), hfx, hgx.astype(BF16)), new_state

    xp = x_prompt.reshape(bp * tp, d)
    xs = x_sample.reshape(bs * ts, d)
    hp = _modulate(xp, mod_p, 0, 0, tp)
    hs = _modulate(xs, mod_s, 0, 0, bs * ts)
    zero_mlstm = (jnp.zeros((bp, hm, HEAD_DIM, HEAD_DIM), F32), jnp.zeros((bp, hm, HEAD_DIM), F32),
                  jnp.zeros((bp, hm), F32))
    zero_gdn = (jnp.zeros((bp, hg, HEAD_DIM, HEAD_DIM), F32), jnp.zeros((bp, 3, 3 * dg), F32))

    def proj_ln(parts_p, parts_s, w, w_prefix, l, j, nxt, coef):
        return _proj_ln(parts_p, parts_s, w, w_prefix, xp, xs, mod_p, mod_s, l, j, nxt, ln_g4, ln_b4, tp, alpha, coef)

    st_p, st_s = [], []
    for l in range(depth):
        ap, as_ = _gateup(hp, hs, ffn_w_gate, ffn_w_up, l, 0)
        (xp, hp), (xs, hs) = proj_ln([ap], [as_], ffn_w_down, (l, 0), l, 0, (l, 1), MACARON_W)
        mixed_p, sp = mixers(l, hp, bp, tp, zero_mlstm, zero_gdn, False)
        mixed_s, ss = mixers(l, hs, bs, ts, (state_mlstm_C[l], state_mlstm_n[l], state_mlstm_m[l]),
                             (state_gdn_S[l], state_gdn_conv[l]), True)
        (xp, hp), (xs, hs) = proj_ln(mixed_p, mixed_s, w_out, (l,), l, 1, (l, 2), 1.0)
        ap, as_ = _gateup(hp, hs, ffn_w_gate, ffn_w_up, l, 1)
        nxt = (l + 1, 0) if l + 1 < depth else None
        (xp, hp), (xs, hs) = proj_ln([ap], [as_], ffn_w_down, (l, 1), l, 2, nxt, MACARON_W)
        st_p.append(sp)
        st_s.append(ss)
    pn = [jnp.stack(a) for a in zip(*st_p)]
    sn = [jnp.stack(a) for a in zip(*st_s)]
    return (xp.reshape(bp, tp, d), xs.reshape(bs, ts, d), pn[0], pn[1], pn[2], pn[3], pn[4], pn[5], pn[6], pn[7],
            sn[0], sn[1], sn[2], sn[3], sn[4], sn[5], sn[6], sn[7])
```

```python
import functools
import math

import jax
import jax.numpy as jnp
from jax import lax
from jax.experimental import pallas as pl
from jax.experimental.pallas import tpu as pltpu

F32 = jnp.float32
BF16 = jnp.bfloat16

HEAD_DIM = 128
LANES = 128
SUBLANES = 8
CHUNK = 64
N_SUB = 3
LN_EPS = 1e-5
NORM_EPS = 1e-6
MACARON_W = 0.5
GATE_ROWS = 32
VMEM_LIMIT_BYTES = 56 * 1024 * 1024
HI = lax.Precision.HIGHEST
NEG_INF = float("-inf")


def _params(*sem):
    return pltpu.CompilerParams(dimension_semantics=sem, vmem_limit_bytes=VMEM_LIMIT_BYTES)


def _dot(a, b, precision=None):
    return jnp.dot(a, b, preferred_element_type=F32, precision=precision)


def _dot_nt(a, b, precision=None):
    return lax.dot_general(a, b, (((1,), (1,)), ((), ())), preferred_element_type=F32, precision=precision)


def _dot_tn(a, b, precision=None):
    return lax.dot_general(a, b, (((0,), (0,)), ((), ())), preferred_element_type=F32, precision=precision)


def _sigmoid(x):
    return 1.0 / (1.0 + jnp.exp(-x))


def _silu(x):
    return x * _sigmoid(x)


def _softplus(x):
    return jnp.maximum(x, 0.0) + jnp.log1p(jnp.exp(-jnp.abs(x)))


def _log_sigmoid(x):
    return -_softplus(-x)


def _pick(m, candidates):
    for c in candidates:
        if m % c == 0:
            return c
    return m


def _ada_kernel(c_ref, w_ref, b_ref, o_ref):
    c = c_ref[...]
    sc = _silu(c).astype(BF16)
    o_ref[...] = _dot(sc, w_ref[...].astype(BF16)) + b_ref[...]


def _ada(c_all, w_ada, b_ada):
    depth, d, n = w_ada.shape
    rows = c_all.shape[0]
    bn = _pick(n, (1024, 512, 256, 128))
    return pl.pallas_call(
        _ada_kernel,
        grid=(depth, n // bn),
        in_specs=[pl.BlockSpec((rows, d), lambda l, j: (0, 0)),
                  pl.BlockSpec((None, d, bn), lambda l, j: (l, 0, j)),
                  pl.BlockSpec((None, 1, bn), lambda l, j: (l, 0, j))],
        out_specs=pl.BlockSpec((None, rows, bn), lambda l, j: (l, 0, j)),
        out_shape=jax.ShapeDtypeStruct((depth, rows, n), F32),
        name="ada",
        compiler_params=_params("arbitrary", "arbitrary"),
    )(c_all, w_ada, b_ada.reshape(depth, 1, n))


def _modulate_kernel(x_ref, sc_ref, sh_ref, o_ref):
    o_ref[...] = (x_ref[...] * (1.0 + sc_ref[...]) + sh_ref[...]).astype(BF16)


def _mod_spec(mod, layer, slot, rows_per_group, bm):
    r, d = mod.shape[3], mod.shape[4]
    if r == 1:
        return pl.BlockSpec((None, None, None, 1, d),
                            lambda m, *_: (layer, slot, (m * bm) // rows_per_group, 0, 0))
    return pl.BlockSpec((None, None, None, bm, d), lambda m, *_: (layer, slot, 0, m, 0))


def _modulate(x, mod, layer, j, rows_per_group):
    m, d = x.shape
    bm = _pick(rows_per_group, (512, 256, 128, 64))
    return pl.pallas_call(
        _modulate_kernel,
        grid=(m // bm,),
        in_specs=[pl.BlockSpec((bm, d), lambda i: (i, 0)),
                  _mod_spec(mod, layer, 3 * j + 1, rows_per_group, bm),
                  _mod_spec(mod, layer, 3 * j + 0, rows_per_group, bm)],
        out_specs=pl.BlockSpec((bm, d), lambda i: (i, 0)),
        out_shape=jax.ShapeDtypeStruct((m, d), BF16),
        name="modulate",
        compiler_params=_params("arbitrary"),
    )(x, mod, mod)


def _gateup_kernel(h_ref, hs_ref, wg_ref, wu_ref, o_ref, os_ref, wg_s, wu_s, *, last):
    i = pl.program_id(1)

    @pl.when(i == 0)
    def _():
        wg_s[...] = wg_ref[...].astype(BF16)
        wu_s[...] = wu_ref[...].astype(BF16)

    def swiglu_half(h):
        g = _dot(h, wg_s[...])
        u = _dot(h, wu_s[...])
        return (_silu(g) * u).astype(BF16)

    o_ref[...] = swiglu_half(h_ref[...])

    @pl.when(i == last)
    def _():
        os_ref[...] = swiglu_half(hs_ref[...])


def _gateup(h, hs, w_gate, w_up, layer, j):
    m, d = h.shape
    ms = hs.shape[0]
    f = w_gate.shape[-1]
    bm = _pick(m, (1024, 512, 256, 128, 64))
    bn = _pick(f, (512, 256, 128))
    wspec = pl.BlockSpec((None, None, d, bn), lambda n, i: (layer, j, 0, n))
    return pl.pallas_call(
        functools.partial(_gateup_kernel, last=m // bm - 1),
        grid=(f // bn, m // bm),
        in_specs=[pl.BlockSpec((bm, d), lambda n, i: (i, 0)), pl.BlockSpec((ms, d), lambda n, i: (0, 0)),
                  wspec, wspec],
        out_specs=[pl.BlockSpec((bm, bn), lambda n, i: (i, n)), pl.BlockSpec((ms, bn), lambda n, i: (0, n))],
        out_shape=[jax.ShapeDtypeStruct((m, f), BF16), jax.ShapeDtypeStruct((ms, f), BF16)],
        scratch_shapes=[pltpu.VMEM((d, bn), BF16), pltpu.VMEM((d, bn), BF16)],
        name="gateup",
        compiler_params=_params("arbitrary", "arbitrary"),
    )(h, hs, w_gate, w_up)


def _proj_ln_kernel(*refs, widths, nk, nm, bk, alpha, coef, emit_h):
    npart = len(widths)
    nin = npart + 2 + (2 if emit_h else 0)
    nout = 2 if emit_h else 1
    w_ref, lng_ref, lnb_ref = refs[0:3]
    ins = [refs[3 + s * nin:3 + (s + 1) * nin] for s in range(2)]
    outs = [refs[3 + 2 * nin + s * nout:3 + 2 * nin + (s + 1) * nout] for s in range(2)]
    w_s = refs[3 + 2 * nin + 2 * nout]
    i = pl.program_id(0)

    @pl.when(i < nk)
    def _():
        w_s[pl.ds(pl.multiple_of(i * bk, bk), bk), :] = w_ref[...].astype(BF16)

    def project(s):
        acc = None
        off = 0
        for a_ref, wd in zip(ins[s][:npart], widths):
            term = _dot(a_ref[...], w_s[off:off + wd, :])
            acc = term if acc is None else acc + term
            off += wd
        return acc

    def finish(s, acc):
        x_ref, gate_ref = ins[s][npart], ins[s][npart + 1]
        y = alpha * x_ref[...] + (coef * (1.0 + gate_ref[...])) * acc
        mu = jnp.mean(y, axis=-1, keepdims=True)
        yc = y - mu
        var = jnp.mean(yc * yc, axis=-1, keepdims=True)
        xn = yc * lax.rsqrt(var + LN_EPS) * lng_ref[...] + lnb_ref[...]
        outs[s][0][...] = xn
        if emit_h:
            sc_ref, sh_ref = ins[s][npart + 2], ins[s][npart + 3]
            outs[s][1][...] = (xn * (1.0 + sc_ref[...]) + sh_ref[...]).astype(BF16)

    @pl.when((i >= nk) & (i < nk + nm))
    def _():
        finish(0, project(0))

    @pl.when(i == nk + nm)
    def _():
        finish(1, project(1))


def _proj_ln(parts_p, parts_s, w, w_prefix, x_p, x_s, mod_p, mod_s, layer, j, nxt, ln_g, ln_b, rows_per_group,
             alpha, coef):
    widths = tuple(a.shape[1] for a in parts_p)
    k = sum(widths)
    m, d = x_p.shape
    ms = x_s.shape[0]
    bm = _pick(rows_per_group, (256, 128, 64))
    bk = _pick(k, (512, 256, 128))
    nk = k // bk
    nm = m // bm
    emit_h = nxt is not None
    npre = len(w_prefix)

    def row(i):
        return jnp.clip(i - nk, 0, nm - 1)

    def modspec_p(lyr, slot):
        return pl.BlockSpec((None, None, None, 1, d),
                            lambda i: (lyr, slot, (row(i) * bm) // rows_per_group, 0, 0))

    def modspec_s(lyr, slot):
        return pl.BlockSpec((None, None, None, ms, d), lambda i: (lyr, slot, 0, 0, 0))

    in_specs = [pl.BlockSpec((None,) * npre + (bk, d), lambda i: tuple(w_prefix) + (jnp.minimum(i, nk - 1), 0)),
                pl.BlockSpec((None, None, 1, d), lambda i: (layer, j, 0, 0)),
                pl.BlockSpec((None, None, 1, d), lambda i: (layer, j, 0, 0))]
    args = [w, ln_g, ln_b]
    in_specs += [pl.BlockSpec((bm, wd), lambda i: (row(i), 0)) for wd in widths]
    in_specs += [pl.BlockSpec((bm, d), lambda i: (row(i), 0)), modspec_p(layer, 3 * j + 2)]
    args += list(parts_p) + [x_p, mod_p]
    if emit_h:
        in_specs += [modspec_p(nxt[0], 3 * nxt[1] + 1), modspec_p(nxt[0], 3 * nxt[1] + 0)]
        args += [mod_p, mod_p]
    in_specs += [pl.BlockSpec((ms, wd), lambda i: (0, 0)) for wd in widths]
    in_specs += [pl.BlockSpec((ms, d), lambda i: (0, 0)), modspec_s(layer, 3 * j + 2)]
    args += list(parts_s) + [x_s, mod_s]
    if emit_h:
        in_specs += [modspec_s(nxt[0], 3 * nxt[1] + 1), modspec_s(nxt[0], 3 * nxt[1] + 0)]
        args += [mod_s, mod_s]
    out_specs = [pl.BlockSpec((bm, d), lambda i: (row(i), 0))]
    out_shape = [jax.ShapeDtypeStruct((m, d), F32)]
    if emit_h:
        out_specs.append(pl.BlockSpec((bm, d), lambda i: (row(i), 0)))
        out_shape.append(jax.ShapeDtypeStruct((m, d), BF16))
    out_specs.append(pl.BlockSpec((ms, d), lambda i: (0, 0)))
    out_shape.append(jax.ShapeDtypeStruct((ms, d), F32))
    if emit_h:
        out_specs.append(pl.BlockSpec((ms, d), lambda i: (0, 0)))
        out_shape.append(jax.ShapeDtypeStruct((ms, d), BF16))
    out = pl.pallas_call(
        functools.partial(_proj_ln_kernel, widths=widths, nk=nk, nm=nm, bk=bk, alpha=alpha, coef=coef,
                          emit_h=emit_h),
        grid=(nk + nm + 1,),
        in_specs=in_specs,
        out_specs=out_specs,
        out_shape=out_shape,
        scratch_shapes=[pltpu.VMEM((k, d), BF16)],
        name="proj_ln",
        compiler_params=_params("arbitrary"),
    )(*args)
    if emit_h:
        return (out[0], out[1]), (out[2], out[3])
    return (out[0], None), (out[1], None)


def _win_kernel(h_ref, w_ref, b_ref, o_ref):
    o_ref[...] = _dot_nt(h_ref[...], w_ref[...]) + b_ref[...]


def _win(h, w, b, layer):
    m, d = h.shape
    n = w.shape[1]
    bm = _pick(m, (2048, 1024, 512, 256, 128, 64))
    bn = _pick(n, (768, 512, 256, 128))
    return pl.pallas_call(
        _win_kernel,
        grid=(m // bm, n // bn),
        in_specs=[pl.BlockSpec((bm, d), lambda i, c: (i, 0)),
                  pl.BlockSpec((None, bn, d), lambda i, c: (layer, c, 0)),
                  pl.BlockSpec((None, 1, bn), lambda i, c: (layer, 0, c))],
        out_specs=pl.BlockSpec((bm, bn), lambda i, c: (i, c)),
        out_shape=jax.ShapeDtypeStruct((m, n), F32),
        name="w_in",
        compiler_params=_params("arbitrary", "arbitrary"),
    )(h, w, b)


def _gates_kernel(x_ref, a_ref, dt_ref, gc_ref, grc_ref, grf_ref, carry_ref, *, hm, hf, hg, lc, seg, tb):
    t = pl.program_id(1)

    @pl.when(t == 0)
    def _():
        carry_ref[...] = jnp.zeros_like(carry_ref)

    z = jnp.transpose(x_ref[...])[:GATE_ROWS]
    ch = lax.broadcasted_iota(jnp.int32, (GATE_ROWS, 1), 0)
    is_mi = ch < hm
    is_mf = (ch >= hm) & (ch < 2 * hm)
    is_ff = (ch >= 2 * hm) & (ch < 2 * hm + hf)
    is_ga = (ch >= 2 * hm + hf) & (ch < 2 * hm + hf + hg)
    is_gb = (ch >= 2 * hm + hf + hg) & (ch < 2 * hm + hf + 2 * hg)
    ls = _log_sigmoid(z)
    gg = -jnp.exp(a_ref[...]) * _softplus(z + dt_ref[...])
    sg = _sigmoid(z)
    s_i = lax.broadcasted_iota(jnp.int32, (tb, tb), 0)
    t_i = lax.broadcasted_iota(jnp.int32, (tb, tb), 1)
    m_loc = jnp.where((s_i <= t_i) & ((s_i // lc) == (t_i // lc)), 1.0, 0.0).astype(F32)
    m_seg = jnp.where((s_i <= t_i) & ((s_i // seg) == (t_i // seg)), 1.0, 0.0).astype(F32)
    v_loc = jnp.where(is_mf, ls, jnp.where(is_ga, gg, 0.0))
    v_seg = jnp.where(is_ff, ls, 0.0)
    cum_loc = _dot(v_loc, m_loc, HI)
    cum_seg = _dot(v_seg, m_seg, HI) + carry_ref[...]
    if seg > tb:
        carry_ref[...] = cum_seg[:, tb - 1:tb]
    bank0 = jnp.where(is_mi, z, jnp.where(is_mf | is_ga, cum_loc,
                                          jnp.where(is_ff, cum_seg, jnp.where(is_gb, sg, 0.0))))
    bank1 = jnp.where(is_ff, ls, 0.0)
    rows = jnp.concatenate([bank0, bank1], axis=0)
    full = jnp.concatenate([rows, jnp.zeros((LANES - 2 * GATE_ROWS, tb), F32)], axis=0)
    gc_ref[...] = jnp.transpose(full)
    for c in range(tb // lc):
        grc_ref[c] = rows[:, c * lc:(c + 1) * lc]
    grf_ref[...] = bank0[2 * hm:2 * hm + hf]


def _gates(proj, gate_blk, a_col, dt_col, layer, nb, t, lc, seg, dims):
    hm, hf, hg = dims
    tb = _pick(t, (512, 256, 128, 64))
    nt = t // tb
    return pl.pallas_call(
        functools.partial(_gates_kernel, hm=hm, hf=hf, hg=hg, lc=lc, seg=seg, tb=tb),
        grid=(nb, nt),
        in_specs=[pl.BlockSpec((tb, LANES), lambda b, i: (b * nt + i, gate_blk)),
                  pl.BlockSpec((None, GATE_ROWS, 1), lambda b, i: (layer, 0, 0)),
                  pl.BlockSpec((None, GATE_ROWS, 1), lambda b, i: (layer, 0, 0))],
        out_specs=[pl.BlockSpec((tb, LANES), lambda b, i: (b * nt + i, 0)),
                   pl.BlockSpec((None, tb // lc, 2 * GATE_ROWS, lc), lambda b, i: (b, i, 0, 0)),
                   pl.BlockSpec((None, hf, tb), lambda b, i: (b, 0, i))],
        out_shape=[jax.ShapeDtypeStruct((nb * t, LANES), F32),
                   jax.ShapeDtypeStruct((nb, t // lc, 2 * GATE_ROWS, lc), F32),
                   jax.ShapeDtypeStruct((nb, hf, t), F32)],
        scratch_shapes=[pltpu.VMEM((GATE_ROWS, 1), F32)],
        name="gates",
        compiler_params=_params("arbitrary", "arbitrary"),
    )(proj, a_col, dt_col)


def _mlstm_kernel(q_ref, k_ref, v_ref, o_ref, gc_ref, gr_ref, ng_ref, c0_ref, n0_ref, m0_ref,
                  h_ref, c_ref, n_ref, m_ref, *, hm, lc, bpb, mxu_dtype):
    @pl.when(pl.program_id(1) == 0)
    def _():
        c_ref[...] = c0_ref[...]
        n_ref[...] = n0_ref[...]
        m_ref[...] = m0_ref[...]

    r_i = lax.broadcasted_iota(jnp.int32, (lc, lc), 0)
    c_i = lax.broadcasted_iota(jnp.int32, (lc, lc), 1)
    causal = c_i <= r_i
    scale = HEAD_DIM ** -0.5
    chains = [(b, h) for b in range(bpb) for h in range(hm)]
    sls = [slice(h * HEAD_DIM, (h + 1) * HEAD_DIM) for _, h in chains]
    first = []
    for (b, h), sl in zip(chains, sls):
        q = q_ref[b, :, sl]
        k = k_ref[b, :, sl] * scale
        qx = q.astype(mxu_dtype)
        vx = v_ref[b, :, sl].astype(mxu_dtype)
        first.append(dict(q=q, k=k, vx=vx, qk=_dot_nt(qx, k.astype(mxu_dtype)),
                          qc=_dot(qx, c_ref[b, h].astype(mxu_dtype))))
    gate = []
    for b, h in chains:
        gc = gc_ref[b]
        gr = gr_ref[b]
        li_col = gc[:, h:h + 1]
        b_col = gc[:, hm + h:hm + h + 1]
        li_row = gr[h:h + 1, :]
        b_row = gr[hm + h:hm + h + 1, :]
        mst = m_ref[b, h]
        dmat = jnp.where(causal, b_col - b_row + li_row, NEG_INF)
        inter = b_col + mst
        b_last = b_col[lc - 1:lc, :]
        d_last = b_last - b_col + li_col
        gate.append(dict(dmat=dmat, inter=inter, d_last=d_last, carry=b_last + mst,
                         dmax=jnp.max(dmat, axis=-1, keepdims=True), lmax=jnp.max(d_last, axis=0, keepdims=True)))
    for g in gate:
        m_row = jnp.maximum(g["inter"], g["dmax"])
        m_new = jnp.maximum(g["carry"], g["lmax"])
        g.update(m_row=m_row, w_intra=jnp.exp(g["dmat"] - m_row), w_inter=jnp.exp(g["inter"] - m_row),
                 m_new=m_new, w_k=jnp.exp(g["d_last"] - m_new), decay=jnp.exp(g["carry"] - m_new))
    outs = []
    for (b, h), g, f in zip(chains, gate, first):
        cst = c_ref[b, h]
        nst = n_ref[b, h]
        s = f["qk"] * g["w_intra"]
        num = _dot(s.astype(mxu_dtype), f["vx"]) + g["w_inter"] * f["qc"]
        den = jnp.sum(s, axis=-1, keepdims=True) + g["w_inter"] * jnp.sum(f["q"] * nst, axis=-1, keepdims=True)
        outs.append(num / jnp.maximum(jnp.abs(den), jnp.exp(-g["m_row"])))
        kw = f["k"] * g["w_k"]
        c_ref[b, h] = g["decay"] * cst + _dot_tn(kw.astype(mxu_dtype), f["vx"])
        n_ref[b, h] = g["decay"] * nst + jnp.sum(kw, axis=0, keepdims=True)
        m_ref[b, h] = g["m_new"]
    for (b, h), sl, hh in zip(chains, sls, outs):
        mu = jnp.mean(hh, axis=-1, keepdims=True)
        hc = hh - mu
        var = jnp.mean(hc * hc, axis=-1, keepdims=True)
        hn = hc * lax.rsqrt(var + LN_EPS) * ng_ref[h]
        h_ref[b, :, sl] = (hn * _sigmoid(o_ref[b, :, sl])).astype(h_ref.dtype)


def _mlstm(proj, gc, grc, norm_g, c0, n0, m0, layer, nb, t, bpb, dims):
    hm = dims[0]
    dm = hm * HEAD_DIM
    lc = math.gcd(t, CHUNK)
    nc = t // lc
    mxu_dtype = BF16 if lc >= 16 else F32

    def col(j):
        return pl.BlockSpec((bpb, lc, dm), lambda b, c: (b, c, j))

    st4 = pl.BlockSpec((bpb, hm, HEAD_DIM, HEAD_DIM), lambda b, c: (b, 0, 0, 0))
    stn = pl.BlockSpec((bpb, hm, 1, HEAD_DIM), lambda b, c: (b, 0, 0, 0))
    stm = pl.BlockSpec((bpb, hm, 1, 1), lambda b, c: (b, 0, 0, 0))
    return pl.pallas_call(
        functools.partial(_mlstm_kernel, hm=hm, lc=lc, bpb=bpb, mxu_dtype=mxu_dtype),
        grid=(nb // bpb, nc),
        in_specs=[col(0), col(1), col(2), col(3),
                  pl.BlockSpec((bpb, lc, LANES), lambda b, c: (b, c, 0)),
                  pl.BlockSpec((bpb, None, 2 * GATE_ROWS, lc), lambda b, c: (b, c, 0, 0)),
                  pl.BlockSpec((None, hm, 1, HEAD_DIM), lambda b, c: (layer, 0, 0, 0)),
                  st4, stn, stm],
        out_specs=[pl.BlockSpec((bpb, lc, dm), lambda b, c: (b, c, 0)), st4, stn, stm],
        out_shape=[jax.ShapeDtypeStruct((nb, t, dm), mxu_dtype),
                   jax.ShapeDtypeStruct((nb, hm, HEAD_DIM, HEAD_DIM), F32),
                   jax.ShapeDtypeStruct((nb, hm, 1, HEAD_DIM), F32),
                   jax.ShapeDtypeStruct((nb, hm, 1, 1), F32)],
        name="mlstm",
        compiler_params=_params("arbitrary", "arbitrary"),
    )(proj, proj, proj, proj, gc, grc, norm_g, c0, n0, m0)


def _split(a):
    hi = a.astype(BF16)
    return hi, (a - hi.astype(F32)).astype(BF16)


def _mm3(a, b):
    return _dot(a[0], b[0]) + (_dot(a[0], b[1]) + _dot(a[1], b[0]))


def _mm3_nt(a, b):
    return _dot_nt(a[0], b[0]) + (_dot_nt(a[0], b[1]) + _dot_nt(a[1], b[0]))


def _unit_lower_solve(a, rhs, lc, split_bf16):
    r_i = lax.broadcasted_iota(jnp.int32, (lc, lc), 0)
    c_i = lax.broadcasted_iota(jnp.int32, (lc, lc), 1)
    eye = jnp.where(r_i == c_i, 1.0, 0.0).astype(F32)
    if split_bf16:
        prep, mm = _split, _mm3
    else:
        prep, mm = (lambda x: x), (lambda x, y: _dot(x, y, HI))
    npows = [-x for x in a]
    invs = [eye + n for n in npows]
    width = 2
    while width < lc:
        ns = [prep(n) for n in npows]
        npows = [mm(n, n) for n in ns]
        invs = [inv + mm(prep(inv), prep(n)) for inv, n in zip(invs, npows)]
        width *= 2
    return [mm(prep(inv), prep(r)) for inv, r in zip(invs, rhs)]


def _gdn_prep_kernel(q_ref, k_ref, v_ref, pq_ref, pk_ref, pv_ref, b0_ref, gc_ref, gr_ref, cw_ref,
                     w_ref, u_ref, qg_ref, kd_ref, qk_ref, buf_ref, xp_ref, *, hm, hf, hg, lc, cb):
    dg = hg * HEAD_DIM
    keep = SUBLANES
    tb = cb * lc
    split_bf16 = lc >= 16
    prev = jnp.concatenate([pq_ref[keep - 3:keep, :], pk_ref[keep - 3:keep, :], pv_ref[keep - 3:keep, :]], axis=1)
    xp_ref[keep - 3:keep, :] = jnp.where(pl.program_id(1) == 0, b0_ref[...], prev)
    xp_ref[keep:keep + tb, 0:dg] = q_ref[...]
    xp_ref[keep:keep + tb, dg:2 * dg] = k_ref[...]
    xp_ref[keep:keep + tb, 2 * dg:3 * dg] = v_ref[...]
    conv = xp_ref[keep:keep + tb, :] * cw_ref[3:4, :]
    for j in range(3):
        conv = conv + xp_ref[keep - 3 + j:keep - 3 + j + tb, :] * cw_ref[j:j + 1, :]
    buf_ref[...] = xp_ref[keep + tb - 3:keep + tb, :]
    act = _silu(conv)

    r_i = lax.broadcasted_iota(jnp.int32, (lc, lc), 0)
    c_i = lax.broadcasted_iota(jnp.int32, (lc, lc), 1)
    incl = c_i <= r_i
    strict = c_i < r_i
    ch_g = 2 * hm + hf
    ch_b = ch_g + hg
    scale = HEAD_DIM ** -0.5
    chains = [(c, h) for c in range(cb) for h in range(hg)]
    amats, rhss, qks = [], [], []
    for c, h in chains:
        rows = slice(c * lc, (c + 1) * lc)
        sl = slice(h * HEAD_DIM, (h + 1) * HEAD_DIM)
        gc = gc_ref[rows, :]
        gr = gr_ref[c]
        cq = act[rows, h * HEAD_DIM:(h + 1) * HEAD_DIM]
        ck = act[rows, dg + h * HEAD_DIM:dg + (h + 1) * HEAD_DIM]
        cv = act[rows, 2 * dg + h * HEAD_DIM:2 * dg + (h + 1) * HEAD_DIM]
        qn = cq * lax.rsqrt(jnp.sum(cq * cq, axis=-1, keepdims=True) + NORM_EPS) * scale
        kn = ck * lax.rsqrt(jnp.sum(ck * ck, axis=-1, keepdims=True) + NORM_EPS)
        g_col = gc[:, ch_g + h:ch_g + h + 1]
        g_row = gr[ch_g + h:ch_g + h + 1, :]
        beta = gc[:, ch_b + h:ch_b + h + 1]
        decay = jnp.exp(jnp.where(incl, g_col - g_row, NEG_INF))
        if split_bf16:
            ks = _split(kn)
            kk = _mm3_nt(ks, ks)
            qk = _dot_nt(qn.astype(BF16), ks[0])
        else:
            kk = _dot_nt(kn, kn, HI)
            qk = _dot_nt(qn, kn)
        amats.append(jnp.where(strict, beta * kk * decay, 0.0))
        eg = jnp.exp(g_col)
        rhss.append(jnp.concatenate([kn * (beta * eg), cv * beta], axis=-1))
        g_last = g_col[lc - 1:lc, :]
        qg_ref[rows, sl] = (qn * eg).astype(qg_ref.dtype)
        kd_ref[rows, sl] = (kn * jnp.exp(g_last - g_col)).astype(kd_ref.dtype)
        qks.append((qk * decay).astype(qk_ref.dtype))
        if h == hg - 1:
            qk_ref[rows, :] = jnp.concatenate(qks[-hg:], axis=1)
    sols = _unit_lower_solve(amats, rhss, lc, split_bf16)
    for (c, h), sol in zip(chains, sols):
        rows = slice(c * lc, (c + 1) * lc)
        sl = slice(h * HEAD_DIM, (h + 1) * HEAD_DIM)
        w_ref[rows, sl] = sol[:, :HEAD_DIM].astype(w_ref.dtype)
        u_ref[rows, sl] = sol[:, HEAD_DIM:]


def _gdn_scan_kernel(w_ref, u_ref, qg_ref, kd_ref, qk_ref, z_ref, gc_ref, ng_ref, s0_ref,
                     h_ref, s_ref, *, hm, hf, hg, lc, bpb):
    @pl.when(pl.program_id(1) == 0)
    def _():
        s_ref[...] = s0_ref[...]

    ch_g = 2 * hm + hf
    xd = w_ref.dtype
    chains = [(b, h) for b in range(bpb) for h in range(hg)]
    sls = [slice(h * HEAD_DIM, (h + 1) * HEAD_DIM) for _, h in chains]
    first = []
    for (b, h), sl in zip(chains, sls):
        sx = s_ref[b, h].astype(xd)
        first.append((_dot(w_ref[b, :, sl], sx), _dot(qg_ref[b, :, sl], sx)))
    outs = []
    for (b, h), sl, (ws, qs) in zip(chains, sls, first):
        ux = (u_ref[b, :, sl] - ws).astype(xd)
        outs.append(qs + _dot(qk_ref[b][:, h * lc:(h + 1) * lc], ux))
        g_last = gc_ref[b, lc - SUBLANES:lc, :][SUBLANES - 1:SUBLANES, ch_g + h:ch_g + h + 1]
        s_ref[b, h] = jnp.exp(g_last) * s_ref[b, h] + _dot_tn(kd_ref[b, :, sl], ux)
    for (b, h), sl, o in zip(chains, sls, outs):
        on = o * lax.rsqrt(jnp.mean(o * o, axis=-1, keepdims=True) + NORM_EPS) * ng_ref[...]
        h_ref[b, :, sl] = (on * _silu(z_ref[b, :, sl])).astype(h_ref.dtype)


def _gdn(proj, gc, grc, conv_w, norm_g, s0, b0, layer, nb, t, bpb, dims):
    hm, hf, hg = dims
    dg = hg * HEAD_DIM
    lc = math.gcd(t, CHUNK)
    nc = t // lc
    xd = BF16 if lc >= 16 else F32
    base = (4 * hm * HEAD_DIM + 3 * hf * HEAD_DIM) // dg
    cb = _pick(nc, (4, 2, 1))
    tb = cb * lc
    nblk = t // tb
    m = nb * t
    n_all = proj.shape[-1]
    proj2 = proj.reshape(m, n_all)

    def col(j):
        return pl.BlockSpec((tb, dg), lambda b, i: (b * nblk + i, base + j))

    def prev(j):
        return pl.BlockSpec((SUBLANES, dg),
                            lambda b, i: (jnp.maximum((b * t + i * tb) // SUBLANES - 1, 0), base + j))

    def rowblk(width):
        return pl.BlockSpec((tb, width), lambda b, i: (b * nblk + i, 0))

    w, u0, qg, kd, qk, b1 = pl.pallas_call(
        functools.partial(_gdn_prep_kernel, hm=hm, hf=hf, hg=hg, lc=lc, cb=cb),
        grid=(nb, nblk),
        in_specs=[col(0), col(1), col(2), prev(0), prev(1), prev(2),
                  pl.BlockSpec((None, 3, 3 * dg), lambda b, i: (b, 0, 0)),
                  rowblk(LANES),
                  pl.BlockSpec((None, cb, 2 * GATE_ROWS, lc), lambda b, i: (b, i, 0, 0)),
                  pl.BlockSpec((None, 4, 3 * dg), lambda b, i: (layer, 0, 0))],
        out_specs=[rowblk(dg), rowblk(dg), rowblk(dg), rowblk(dg), rowblk(hg * lc),
                   pl.BlockSpec((None, 3, 3 * dg), lambda b, i: (b, 0, 0))],
        out_shape=[jax.ShapeDtypeStruct((m, dg), xd), jax.ShapeDtypeStruct((m, dg), F32),
                   jax.ShapeDtypeStruct((m, dg), xd), jax.ShapeDtypeStruct((m, dg), xd),
                   jax.ShapeDtypeStruct((m, hg * lc), xd), jax.ShapeDtypeStruct((nb, 3, 3 * dg), F32)],
        scratch_shapes=[pltpu.VMEM((tb + SUBLANES, 3 * dg), F32)],
        name="gdn_prep",
        compiler_params=_params("arbitrary", "arbitrary"),
    )(proj2, proj2, proj2, proj2, proj2, proj2, b0, gc.reshape(m, LANES), grc, conv_w)

    def seq(width):
        return pl.BlockSpec((bpb, lc, width), lambda b, c: (b, c, 0))

    st4 = pl.BlockSpec((bpb, hg, HEAD_DIM, HEAD_DIM), lambda b, c: (b, 0, 0, 0))
    hgx, s1 = pl.pallas_call(
        functools.partial(_gdn_scan_kernel, hm=hm, hf=hf, hg=hg, lc=lc, bpb=bpb),
        grid=(nb // bpb, nc),
        in_specs=[seq(dg), seq(dg), seq(dg), seq(dg), seq(hg * lc),
                  pl.BlockSpec((bpb, lc, dg), lambda b, c: (b, c, base + 3)),
                  seq(LANES),
                  pl.BlockSpec((None, 1, HEAD_DIM), lambda b, c: (layer, 0, 0)),
                  st4],
        out_specs=[seq(dg), st4],
        out_shape=[jax.ShapeDtypeStruct((nb, t, dg), xd),
                   jax.ShapeDtypeStruct((nb, hg, HEAD_DIM, HEAD_DIM), F32)],
        name="gdn_scan",
        compiler_params=_params("arbitrary", "arbitrary"),
    )(w.reshape(nb, t, dg), u0.reshape(nb, t, dg), qg.reshape(nb, t, dg), kd.reshape(nb, t, dg),
      qk.reshape(nb, t, hg * lc), proj, gc, norm_g, s0)
    return hgx, s1, b1


def _fox_prompt_kernel(qt_ref, kt_ref, q_ref, k_ref, v_ref, gr_ref, o_ref, q_s, m_s, l_s, acc_s, *, hf, tb):
    qi = qt_ref[pl.program_id(1)]
    ki = kt_ref[pl.program_id(1)]
    rep = tb // LANES

    @pl.when(ki == 0)
    def _():
        q_s[...] = (q_ref[...] * HEAD_DIM ** -0.5).astype(BF16)
        m_s[...] = jnp.full(m_s.shape, NEG_INF, F32)
        l_s[...] = jnp.zeros(l_s.shape, F32)
        acc_s[...] = jnp.zeros(acc_s.shape, F32)

    def block(masked):
        nsub = 2 if tb % (2 * LANES) == 0 else 1
        sub = tb // nsub
        for h, kh in [(h, kh) for h in range(hf) for kh in range(nsub)]:
            sl = slice(h * HEAD_DIM, (h + 1) * HEAD_DIM)
            cols = slice(kh * sub, (kh + 1) * sub)
            k = k_ref[cols, sl].astype(BF16)
            v = v_ref[cols, sl].astype(BF16)
            s = _dot_nt(q_s[:, sl], k) - gr_ref[h:h + 1, cols]
            if masked:
                visible = (kh * sub + lax.broadcasted_iota(jnp.int32, (tb, sub), 1)
                           <= lax.broadcasted_iota(jnp.int32, (tb, sub), 0))
                s = jnp.where(visible, s, NEG_INF)
            m_prev = m_s[h]
            m_new = jnp.maximum(m_prev, jnp.max(s, axis=-1, keepdims=True))
            alpha = jnp.exp(m_prev - m_new)
            p = jnp.exp(s - jnp.concatenate([m_new] * (sub // LANES), axis=1))
            l_s[h] = alpha * l_s[h] + jnp.sum(p, axis=-1, keepdims=True)
            acc_s[:, sl] = alpha * acc_s[:, sl] + _dot(p.astype(BF16), v)
            m_s[h] = m_new

    @pl.when(ki < qi)
    def _():
        block(False)

    @pl.when(ki == qi)
    def _():
        block(True)
        for h in range(hf):
            sl = slice(h * HEAD_DIM, (h + 1) * HEAD_DIM)
            o_ref[:, sl] = (acc_s[:, sl] / l_s[h]).astype(BF16)


def _fox_prompt(proj, grf, nb, t, dims):
    hm, hf, hg = dims
    df = hf * HEAD_DIM
    base = (4 * hm * HEAD_DIM) // df
    tb = _pick(t, (512, 256, 128))
    nq = t // tb
    pairs = [(i, j) for i in range(nq) for j in range(i + 1)]
    q_tbl = jnp.asarray([p[0] for p in pairs], jnp.int32)
    k_tbl = jnp.asarray([p[1] for p in pairs], jnp.int32)
    return pl.pallas_call(
        functools.partial(_fox_prompt_kernel, hf=hf, tb=tb),
        grid_spec=pltpu.PrefetchScalarGridSpec(
            num_scalar_prefetch=2,
            grid=(nb, len(pairs)),
            in_specs=[pl.BlockSpec((tb, df), lambda b, s, qt, kt: (b * nq + qt[s], base)),
                      pl.BlockSpec((tb, df), lambda b, s, qt, kt: (b * nq + kt[s], base + 1)),
                      pl.BlockSpec((tb, df), lambda b, s, qt, kt: (b * nq + kt[s], base + 2)),
                      pl.BlockSpec((None, hf, tb), lambda b, s, qt, kt: (b, 0, kt[s]))],
            out_specs=pl.BlockSpec((tb, df), lambda b, s, qt, kt: (b * nq + qt[s], 0)),
            scratch_shapes=[pltpu.VMEM((tb, df), BF16), pltpu.VMEM((hf, tb, LANES), F32),
                            pltpu.VMEM((hf, tb, LANES), F32), pltpu.VMEM((tb, df), F32)]),
        out_shape=jax.ShapeDtypeStruct((nb * t, df), BF16),
        name="fox_prompt",
        compiler_params=_params("arbitrary", "arbitrary"),
    )(q_tbl, k_tbl, proj, proj, proj, grf)


def _fox_pool_kernel(x_ref, w_ref, t_ref, *, hf, page):
    n = hf * page
    x = x_ref[...]
    x1 = x.astype(BF16)
    r1 = x - x1.astype(F32)
    x2 = r1.astype(BF16)
    x3 = (r1 - x2.astype(F32)).astype(BF16)
    ri = lax.broadcasted_iota(jnp.int32, (n, n), 0)
    ci = lax.broadcasted_iota(jnp.int32, (n, n), 1)
    later = ((ri // page) == (ci % hf)) & ((ri % page) > (ci // hf))
    m_suf = jnp.where(later, 1.0, 0.0).astype(BF16)
    rj = lax.broadcasted_iota(jnp.int32, (n, LANES), 0)
    lj = lax.broadcasted_iota(jnp.int32, (n, LANES), 1)
    m_tot = jnp.where((rj // page) == (lj % hf), 1.0, 0.0).astype(BF16)
    w_ref[...] = _dot(x1, m_suf) + (_dot(x2, m_suf) + _dot(x3, m_suf))
    t_ref[...] = _dot(x1, m_tot) + (_dot(x2, m_tot) + _dot(x3, m_tot))


def _fox_pool(logf_hs, hf, page):
    r, n = logf_hs.shape
    rb = _pick(r, (512, 256, 128, 64, 32, 16, 8))
    return pl.pallas_call(
        functools.partial(_fox_pool_kernel, hf=hf, page=page),
        grid=(r // rb,),
        in_specs=[pl.BlockSpec((rb, n), lambda i: (i, 0))],
        out_specs=[pl.BlockSpec((rb, n), lambda i: (i, 0)), pl.BlockSpec((rb, LANES), lambda i: (i, 0))],
        out_shape=[jax.ShapeDtypeStruct((r, n), F32), jax.ShapeDtypeStruct((r, LANES), F32)],
        name="fox_pool",
        compiler_params=_params("arbitrary"),
    )(logf_hs)


def _fox_sample_kernel(pt_ref, q_ref, cr_ref, kn_ref, vn_ref, *rest, hf, nsteps, rpp, npp):
    kp = rest[0:npp]
    vp = rest[npp:2 * npp]
    wp = rest[2 * npp:3 * npp]
    tp = rest[3 * npp:4 * npp]
    o_ref, m_s, l_s, acc_s, carry_s = rest[4 * npp:]
    p = pl.program_id(1)
    q = (q_ref[...] * HEAD_DIM ** -0.5).astype(BF16)
    nr = q.shape[0]

    @pl.when(p == 0)
    def _():
        ri = lax.broadcasted_iota(jnp.int32, (nr, nr), 0)
        ci = lax.broadcasted_iota(jnp.int32, (nr, nr), 1)
        ok = ((ri % hf) == (ci % hf)) & ((ci // hf) <= (ri // hf))
        s = jnp.where(ok, _dot_nt(q, kn_ref[...].astype(BF16)) - cr_ref[...], NEG_INF)
        m0 = jnp.max(s, axis=-1, keepdims=True)
        e = jnp.exp(s - m0)
        m_s[...] = m0
        l_s[...] = jnp.sum(e, axis=-1, keepdims=True)
        acc_s[...] = _dot(e.astype(BF16), vn_ref[...].astype(BF16))
        carry_s[...] = jnp.zeros_like(carry_s)

    ncol = rpp * LANES
    ri = lax.broadcasted_iota(jnp.int32, (nr, ncol), 0)
    ci = lax.broadcasted_iota(jnp.int32, (nr, ncol), 1)
    head_mask = jnp.where((ri % hf) == (ci % hf), 0.0, NEG_INF).astype(F32)
    carry = carry_s[...]
    scores = []
    for g in range(npp):
        bias_row = jnp.concatenate([wp[g][r:r + 1, :] + carry for r in range(rpp)], axis=1)
        scores.append(_dot_nt(q, kp[g][...].astype(BF16)) + (head_mask + bias_row))
        carry = carry + tp[g][...]
    carry_s[...] = carry
    top = scores[0]
    for g in range(1, npp):
        top = jnp.maximum(top, scores[g])
    m_prev = m_s[...]
    m_new = jnp.maximum(m_prev, jnp.max(top, axis=-1, keepdims=True))
    alpha = jnp.exp(m_prev - m_new)
    acc = alpha * acc_s[...]
    tot = None
    for g in range(npp):
        e = jnp.exp(scores[g] - m_new)
        tot = e if tot is None else tot + e
        acc = acc + _dot(e.astype(BF16), vp[g][...].astype(BF16))
    l_s[...] = alpha * l_s[...] + jnp.sum(tot, axis=-1, keepdims=True)
    acc_s[...] = acc
    m_s[...] = m_new

    @pl.when(p == nsteps - 1)
    def _():
        o_ref[...] = (acc / l_s[...]).astype(BF16)


def _fox_sample(page_table, q2, cum_row, kn2, vn2, kpool, vpool, wpool, tpool, layer, hf):
    nb, npg = page_table.shape
    nr = q2.shape[1]
    nphys, pcols = kpool.shape[1], kpool.shape[2]
    rpp = pcols // LANES
    npp = _pick(npg, (16, 8, 4, 2, 1))
    nsteps = npg // npp

    def page(b, p, pt, g):
        return pt[b * npg + (npg - 1 - (p * npp + g))]

    per_b3 = lambda b, p, pt: (b, 0, 0)
    kv_specs = [pl.BlockSpec((None, None, pcols, HEAD_DIM),
                             lambda b, p, pt, g=g: (layer, page(b, p, pt, g), 0, 0)) for g in range(npp)]
    w_specs = [pl.BlockSpec((None, rpp, LANES),
                            lambda b, p, pt, g=g: (layer * nphys + page(b, p, pt, g), 0, 0)) for g in range(npp)]
    t_specs = [pl.BlockSpec((None, 1, LANES),
                            lambda b, p, pt, g=g: (layer * nphys + page(b, p, pt, g), 0, 0)) for g in range(npp)]
    return pl.pallas_call(
        functools.partial(_fox_sample_kernel, hf=hf, nsteps=nsteps, rpp=rpp, npp=npp),
        grid_spec=pltpu.PrefetchScalarGridSpec(
            num_scalar_prefetch=1,
            grid=(nb, nsteps),
            in_specs=[pl.BlockSpec((None, nr, HEAD_DIM), per_b3),
                      pl.BlockSpec((None, 1, nr), per_b3),
                      pl.BlockSpec((None, nr, HEAD_DIM), per_b3),
                      pl.BlockSpec((None, nr, HEAD_DIM), per_b3)] + kv_specs + kv_specs + w_specs + t_specs,
            out_specs=pl.BlockSpec((None, nr, HEAD_DIM), per_b3),
            scratch_shapes=[pltpu.VMEM((nr, 1), F32), pltpu.VMEM((nr, 1), F32),
                            pltpu.VMEM((nr, HEAD_DIM), F32), pltpu.VMEM((1, LANES), F32)]),
        out_shape=jax.ShapeDtypeStruct((nb, nr, HEAD_DIM), BF16),
        name="fox_decode",
        compiler_params=_params("arbitrary", "arbitrary"),
    )(page_table.reshape(-1), q2, cum_row, kn2, vn2, *([kpool] * npp), *([vpool] * npp),
      *([wpool] * npp), *([tpool] * npp))


def _regroup_w_in(w_in, b_in, dims):
    hm, hf, hg = dims
    dm, df, dg = hm * HEAD_DIM, hf * HEAD_DIM, hg * HEAD_DIM
    sizes = (dm, dm, dm, dm, hm, hm, df, df, df, hf, dg, dg, dg, dg, hg, hg)
    offs = [0]
    for s in sizes:
        offs.append(offs[-1] + s)
    order = (0, 1, 2, 3, 6, 7, 8, 10, 11, 12, 13, 4, 5, 9, 14, 15)
    n_main = 4 * dm + 3 * df + 4 * dg
    n_gate = 2 * hm + hf + 2 * hg
    n_tot = -(-(n_main + LANES) // 512) * 512
    pad = n_tot - n_main - n_gate

    def regroup(a, axis):
        parts = [lax.slice_in_dim(a, offs[i], offs[i + 1], axis=axis) for i in order]
        zshape = list(a.shape)
        zshape[axis] = pad
        parts.append(jnp.zeros(zshape, a.dtype))
        return jnp.concatenate(parts, axis=axis)

    w_t = regroup(jnp.swapaxes(w_in, 1, 2).astype(BF16), 1)
    return w_t, regroup(b_in, 1)[:, None, :], n_main // LANES


def _gate_param_cols(a_log, dt_bias, dims):
    hm, hf, hg = dims
    depth = a_log.shape[0]
    lead = 2 * hm + hf
    z0 = jnp.zeros((depth, lead), F32)
    z1 = jnp.zeros((depth, GATE_ROWS - lead - hg), F32)
    a_col = jnp.concatenate([z0, a_log, z1], axis=1)[:, :, None]
    dt_col = jnp.concatenate([z0, dt_bias, z1], axis=1)[:, :, None]
    return a_col, dt_col


def kernel(x_prompt, x_sample, cache_fox_k, cache_fox_v, cache_fox_logf, state_mlstm_C, state_mlstm_n, state_mlstm_m, state_gdn_S, state_gdn_conv, page_table, c_prompt, c_sample, w_ada, b_ada, ln_g, ln_b, ffn_w_gate, ffn_w_up, ffn_w_down, w_in, b_in, mlstm_norm_g, gdn_conv_w, gdn_A_log, gdn_dt_bias, gdn_norm_g, w_out):
    bp, tp, d = x_prompt.shape
    bs, ts, _ = x_sample.shape
    depth = w_ada.shape[0]
    hm, hf, hg = state_mlstm_C.shape[2], cache_fox_k.shape[3], state_gdn_S.shape[2]
    dims = (hm, hf, hg)
    dm, df, dg = hm * HEAD_DIM, hf * HEAD_DIM, hg * HEAD_DIM
    nphys, page = cache_fox_k.shape[1], cache_fox_k.shape[2]
    alpha = (2 * depth) ** 0.25
    assert 2 * hm + hf + 2 * hg <= GATE_ROWS and (page * hf) % LANES == 0 and LANES % hf == 0

    nrow = -(-(bp + bs) // 16) * 16
    c_all = jnp.concatenate([c_prompt, c_sample, jnp.zeros((nrow - bp - bs, d), F32)], axis=0)
    mod = _ada(c_all, w_ada, b_ada).reshape(depth, nrow, 3 * N_SUB, d)
    mod = jnp.transpose(mod, (0, 2, 1, 3))
    mod_p = mod[:, :, :bp, None, :]
    mod_s = jnp.repeat(mod[:, :, bp:bp + bs], ts, axis=2)[:, :, None, :, :]

    w_in_r, b_in_r, gate_blk = _regroup_w_in(w_in, b_in, dims)
    a_col, dt_col = _gate_param_cols(gdn_A_log, gdn_dt_bias, dims)
    ln_g4 = ln_g[:, :, None, :]
    ln_b4 = ln_b[:, :, None, :]
    norm_m = mlstm_norm_g[:, :, None, :]
    norm_g = gdn_norm_g[:, None, :]

    rpp = (page * hf) // LANES
    logf_hs = jnp.transpose(cache_fox_logf, (0, 1, 3, 2)).reshape(depth * nphys, hf * page)
    wpool, tpool = _fox_pool(logf_hs, hf, page)
    wpool = wpool.reshape(depth * nphys, rpp, LANES)
    tpool = tpool.reshape(depth * nphys, 1, LANES)
    kpool = cache_fox_k.reshape(depth, nphys, page * hf, HEAD_DIM)
    vpool = cache_fox_v.reshape(depth, nphys, page * hf, HEAD_DIM)

    o_f = 4 * dm
    o_g = o_f + 3 * df

    def mixers(l, h, nb, t, mlstm_state, gdn_state, sample):
        m_rows = nb * t
        proj = _win(h, w_in_r, b_in_r, l)
        lc = math.gcd(t, CHUNK)
        if sample:
            gc, grc, grf = _gates(proj, gate_blk, a_col, dt_col, l, 1, m_rows, lc, t, dims)
            grc = grc.reshape(nb, t // lc, 2 * GATE_ROWS, lc)
        else:
            gc, grc, grf = _gates(proj, gate_blk, a_col, dt_col, l, nb, t, lc, t, dims)
        c0, n0, m0 = mlstm_state
        proj3 = proj.reshape(nb, t, proj.shape[-1])
        gc3 = gc.reshape(nb, t, LANES)
        bpb = 1 if sample else nb
        hmx, c1, n1, m1 = _mlstm(proj3, gc3, grc, norm_m, c0, n0[:, :, None, :], m0[:, :, None, None],
                                 l, nb, t, bpb, dims)
        s0, b0 = gdn_state
        hgx, s1, b1 = _gdn(proj3, gc3, grc, gdn_conv_w, norm_g, s0, b0, l, nb, t, bpb, dims)
        hmx = hmx.reshape(m_rows, dm)
        hgx = hgx.reshape(m_rows, dg)
        fk = proj[:, o_f + df:o_f + 2 * df]
        fv = proj[:, o_f + 2 * df:o_f + 3 * df]
        logf = gc[:, GATE_ROWS + 2 * hm:GATE_ROWS + 2 * hm + hf]
        if sample:
            nr = t * hf
            q2 = proj[:, o_f:o_f + df].reshape(nb, nr, HEAD_DIM)
            cum = gc[:, 2 * hm:2 * hm + hf].reshape(nb, nr)
            hfx = _fox_sample(page_table, q2, cum[:, None, :], fk.reshape(nb, nr, HEAD_DIM),
                              fv.reshape(nb, nr, HEAD_DIM), kpool, vpool, wpool, tpool, l, hf)
            hfx = hfx.reshape(m_rows, df)
        else:
            hfx = _fox_prompt(proj, grf, nb, t, dims)
        new_state = (fk.reshape(nb, t, hf, HEAD_DIM), fv.reshape(nb, t, hf, HEAD_DIM), logf.reshape(nb, t, hf),
                     c1, n1[:, :, 0, :], m1[:, :, 0, 0], s1, b1)
        return (hmx.astype(BF16), hfx, hgx.astype(BF16)), new_state

    xp = x_prompt.reshape(bp * tp, d)
    xs = x_sample.reshape(bs * ts, d)
    hp = _modulate(xp, mod_p, 0, 0, tp)
    hs = _modulate(xs, mod_s, 0, 0, bs * ts)
    zero_mlstm = (jnp.zeros((bp, hm, HEAD_DIM, HEAD_DIM), F32), jnp.zeros((bp, hm, HEAD_DIM), F32),
                  jnp.zeros((bp, hm), F32))
    zero_gdn = (jnp.zeros((bp, hg, HEAD_DIM, HEAD_DIM), F32), jnp.zeros((bp, 3, 3 * dg), F32))

    def proj_ln(parts_p, parts_s, w, w_prefix, l, j, nxt, coef):
        return _proj_ln(parts_p, parts_s, w, w_prefix, xp, xs, mod_p, mod_s, l, j, nxt, ln_g4, ln_b4, tp, alpha, coef)

    st_p, st_s = [], []
    for l in range(depth):
        ap, as_ = _gateup(hp, hs, ffn_w_gate, ffn_w_up, l, 0)
        (xp, hp), (xs, hs) = proj_ln([ap], [as_], ffn_w_down, (l, 0), l, 0, (l, 1), MACARON_W)
        mixed_p, sp = mixers(l, hp, bp, tp, zero_mlstm, zero_gdn, False)
        mixed_s, ss = mixers(l, hs, bs, ts, (state_mlstm_C[l], state_mlstm_n[l], state_mlstm_m[l]),
                             (state_gdn_S[l], state_gdn_conv[l]), True)
        (xp, hp), (xs, hs) = proj_ln(mixed_p, mixed_s, w_out, (l,), l, 1, (l, 2), 1.0)
        ap, as_ = _gateup(hp, hs, ffn_w_gate, ffn_w_up, l, 1)
        nxt = (l + 1, 0) if l + 1 < depth else None
        (xp, hp), (xs, hs) = proj_ln([ap], [as_], ffn_w_down, (l, 1), l, 2, nxt, MACARON_W)
        st_p.append(sp)
        st_s.append(ss)
    pn = [jnp.stack(a) for a in zip(*st_p)]
    sn = [jnp.stack(a) for a in zip(*st_s)]
    return (xp.reshape(bp, tp, d), xs.reshape(bs, ts, d), pn[0], pn[1], pn[2], pn[3], pn[4], pn[5], pn[6], pn[7],
            sn[0], sn[1], sn[2], sn[3], sn[4], sn[5], sn[6], sn[7])
```
